```python
import jax
import jax.numpy as jnp
from jax import lax
import numpy as np

D_MODEL = 1024
BATCH = 8
SEQ = 2048
DEPTH = 1
DEC_BATCH = 128
DEC_SEQ = 4
PAST_LEN = 16384
PAGE_SIZE = 128

N_META = 16
CONV_WIDTH = 3
CONV_DIM = D_MODEL // 2
CONV_GROUPS = 8
RET_HEADS = 8
RET_HEAD_DIM = (D_MODEL // 2) // RET_HEADS
RET_DIM = RET_HEADS * RET_HEAD_DIM
MIX_DIM = CONV_DIM + RET_DIM
IN_COLS = 3 * CONV_DIM + 4 * RET_DIM
D_FF = -(-8 * D_MODEL // (3 * 256)) * 256
CHUNK = 128
ROPE_BASE = 10000.0
EPS = 1e-6
GN_EPS = 1e-5

kernel_name = 'hymba_conv_retnet_step'


def _log_gamma():
    return jnp.asarray(np.log1p(-(2.0 ** (-5.0 - np.arange(RET_HEADS)))), dtype=jnp.float32)


def _rmsnorm(x, g):
    xf = x.astype(jnp.float32)
    y = xf * lax.rsqrt(jnp.mean(xf * xf, axis=-1, keepdims=True) + EPS)
    return (y * g.astype(jnp.float32)).astype(x.dtype)


def _rope(t, pos):
    half = RET_HEAD_DIM // 2
    inv = ROPE_BASE ** (-jnp.arange(half, dtype=jnp.float32) / half)
    ang = pos[:, None] * inv[None, :]
    cos = jnp.cos(ang)[None, :, None, :]
    sin = jnp.sin(ang)[None, :, None, :]
    t1, t2 = t[..., :half], t[..., half:]
    return jnp.concatenate([t1 * cos - t2 * sin, t1 * sin + t2 * cos], axis=-1)


def _retention_block(S, q, k, v, log_gamma):
    L = q.shape[1]
    idx = jnp.arange(L, dtype=jnp.float32)
    diff = idx[:, None] - idx[None, :]
    decay = jnp.where(diff[None] >= 0,
                      jnp.exp(jnp.maximum(diff, 0.0)[None] * log_gamma[:, None, None]), 0.0)
    scores = jnp.einsum('nihd,njhd->nhij', q, k) * decay[None]
    intra = jnp.einsum('nhij,njhe->nihe', scores, v)
    q_decay = jnp.exp((idx + 1.0)[:, None] * log_gamma[None, :])
    cross = jnp.einsum('nihd,nhde->nihe', q * q_decay[None, :, :, None], S)
    k_decay = jnp.exp((L - 1.0 - idx)[:, None] * log_gamma[None, :])
    S_new = (jnp.exp(L * log_gamma)[None, :, None, None] * S
             + jnp.einsum('njhd,njhe->nhde', k * k_decay[None, :, :, None], v))
    return intra + cross, S_new


def _retention(q, k, v, S0, log_gamma):
    N, L, H, D = q.shape
    lead = L % CHUNK
    outs = []
    S = S0
    if lead:
        o, S = _retention_block(S, q[:, :lead], k[:, :lead], v[:, :lead], log_gamma)
        outs.append(o)
    n_chunks = (L - lead) // CHUNK
    if n_chunks:
        def split(t):
            return t[:, lead:].reshape(N, n_chunks, CHUNK, H, D).swapaxes(0, 1)

        def step(S_c, qkv):
            o_c, S_c = _retention_block(S_c, qkv[0], qkv[1], qkv[2], log_gamma)
            return S_c, o_c

        S, o = lax.scan(step, S, (split(q), split(k), split(v)))
        outs.append(o.swapaxes(0, 1).reshape(N, n_chunks * CHUNK, H, D))
    return jnp.concatenate(outs, axis=1), S


def _layer(x, pos, conv_buf, ret_state, log_gamma, norm1_g, w_in, w_conv, ret_norm_g, w_out,
           norm2_g, w_gate, w_up, w_down):
    N, L, _ = x.shape
    xn = _rmsnorm(x, norm1_g)
    z = xn @ w_in
    cuts = [CONV_DIM, 2 * CONV_DIM, 3 * CONV_DIM, 3 * CONV_DIM + RET_DIM,
            3 * CONV_DIM + 2 * RET_DIM, 3 * CONV_DIM + 3 * RET_DIM]
    b_gate, c_gate, h, q, k, v, g = jnp.split(z, cuts, axis=-1)

    u = c_gate * h
    ext = jnp.concatenate([conv_buf.astype(u.dtype), u], axis=1)
    conv = sum(w_conv[j] * ext[:, j:j + L] for j in range(CONV_WIDTH))
    conv_out = b_gate * conv
    new_conv_buf = ext[:, L:]

    qh = _rope(q.astype(jnp.float32).reshape(N, L, RET_HEADS, RET_HEAD_DIM), pos)
    kh = _rope(k.astype(jnp.float32).reshape(N, L, RET_HEADS, RET_HEAD_DIM), pos) * (RET_HEAD_DIM ** -0.5)
    vh = v.astype(jnp.float32).reshape(N, L, RET_HEADS, RET_HEAD_DIM)
    o, S_new = _retention(qh, kh, vh, ret_state.astype(jnp.float32), log_gamma)
    mu = jnp.mean(o, axis=-1, keepdims=True)
    var = jnp.mean(jnp.square(o - mu), axis=-1, keepdims=True)
    o = ((o - mu) * lax.rsqrt(var + GN_EPS)).reshape(N, L, RET_DIM) * ret_norm_g.astype(jnp.float32)
    ret_out = (jax.nn.silu(g.astype(jnp.float32)) * o).astype(x.dtype)

    x = x + jnp.concatenate([conv_out, ret_out], axis=-1) @ w_out
    xn = _rmsnorm(x, norm2_g)
    x = x + (jax.nn.silu(xn @ w_gate) * (xn @ w_up)) @ w_down
    return x, new_conv_buf, S_new.astype(x.dtype)


def setup_inputs(seed: int = 0) -> dict:
    key = jax.random.key(seed)
    ks = jax.random.split(key, 16)
    f32 = jnp.float32

    def nrm(k, shape, scale):
        return jax.random.normal(k, shape, f32) * scale

    return {
        'x_prompt': nrm(ks[0], (BATCH, SEQ, D_MODEL), 1.0),
        'x_sample': nrm(ks[1], (DEC_BATCH, DEC_SEQ, D_MODEL), 1.0),
        'state_conv': nrm(ks[2], (DEPTH, DEC_BATCH, CONV_WIDTH - 1, CONV_DIM), 1.0),
        'state_ret': nrm(ks[3], (DEPTH, DEC_BATCH, RET_HEADS, RET_HEAD_DIM, RET_HEAD_DIM), 1.0),
        'meta_tokens': nrm(ks[4], (N_META, D_MODEL), 1.0),
        'norm1_g': 1.0 + nrm(ks[5], (DEPTH, D_MODEL), 0.02),
        'w_in': nrm(ks[6], (DEPTH, D_MODEL, IN_COLS), D_MODEL ** -0.5),
        'w_conv': nrm(ks[7], (DEPTH, CONV_WIDTH, CONV_DIM), CONV_WIDTH ** -0.5),
        'ret_norm_g': 1.0 + nrm(ks[8], (DEPTH, RET_DIM), 0.02),
        'w_out': nrm(ks[9], (DEPTH, MIX_DIM, D_MODEL), MIX_DIM ** -0.5),
        'norm2_g': 1.0 + nrm(ks[10], (DEPTH, D_MODEL), 0.02),
        'w_gate': nrm(ks[11], (DEPTH, D_MODEL, D_FF), D_MODEL ** -0.5),
        'w_up': nrm(ks[12], (DEPTH, D_MODEL, D_FF), D_MODEL ** -0.5),
        'w_down': nrm(ks[13], (DEPTH, D_FF, D_MODEL), D_FF ** -0.5),
        'final_norm_g': 1.0 + nrm(ks[14], (D_MODEL,), 0.02),
    }


def reference(x_prompt, x_sample, state_conv, state_ret, meta_tokens, norm1_g, w_in, w_conv,
              ret_norm_g, w_out, norm2_g, w_gate, w_up, w_down, final_norm_g):
    log_gamma = _log_gamma()
    n_p = x_prompt.shape[0]
    meta = jnp.broadcast_to(meta_tokens.astype(x_prompt.dtype)[None], (n_p, N_META, D_MODEL))
    xp = jnp.concatenate([meta, x_prompt], axis=1)
    xs = x_sample
    pos_p = jnp.arange(xp.shape[1], dtype=jnp.float32)
    pos_s = PAST_LEN + jnp.arange(xs.shape[1], dtype=jnp.float32)
    conv_p, ret_p, conv_s, ret_s = [], [], [], []
    for l in range(DEPTH):
        params = (norm1_g[l], w_in[l], w_conv[l], ret_norm_g[l], w_out[l], norm2_g[l],
                  w_gate[l], w_up[l], w_down[l])
        zero_conv = jnp.zeros((n_p, CONV_WIDTH - 1, CONV_DIM), xp.dtype)
        zero_ret = jnp.zeros((n_p, RET_HEADS, RET_HEAD_DIM, RET_HEAD_DIM), jnp.float32)
        xp, cb_p, rs_p = _layer(xp, pos_p, zero_conv, zero_ret, log_gamma, *params)
        xs, cb_s, rs_s = _layer(xs, pos_s, state_conv[l], state_ret[l], log_gamma, *params)
        conv_p.append(cb_p)
        ret_p.append(rs_p)
        conv_s.append(cb_s)
        ret_s.append(rs_s)
    y_prompt = _rmsnorm(xp, final_norm_g)[:, N_META:]
    y_sample = _rmsnorm(xs, final_norm_g)
    return (y_prompt, y_sample, jnp.stack(conv_p), jnp.stack(ret_p), jnp.stack(conv_s), jnp.stack(ret_s))
```

```python
import functools

import numpy as np
import jax
import jax.numpy as jnp
from jax import lax
from jax.experimental import pallas as pl
from jax.experimental.pallas import tpu as pltpu

D_MODEL = 1024
N_META = 16
CONV_DIM = 512
RET_HEADS = 8
HEAD_DIM = 64
HALF_DIM = HEAD_DIM // 2
RET_DIM = RET_HEADS * HEAD_DIM
D_FF = 2816
PAST_LEN = 16384
ROPE_BASE = 10000.0
EPS = 1e-6
GN_EPS = 1e-5

OFF_B, OFF_C, OFF_H, OFF_Q, OFF_K, OFF_V, OFF_G = (i * 512 for i in range(7))

LANES = 128
GROUP = 256
HEADS_PER_GROUP = GROUP // HEAD_DIM
N_GROUPS = RET_DIM // GROUP
TILE = 256
FFN_TILE = 256
SEQ_BLOCK = 8
VMEM_LIMIT = 56 * 1024 * 1024

F32 = jnp.float32
BF16 = jnp.bfloat16

_LOG_GAMMA = np.log1p(-(2.0 ** (-5.0 - np.arange(RET_HEADS)))).astype(np.float32)


def _dot(a, b):
    return jnp.dot(a, b, preferred_element_type=F32)


def _dot_nt(a, b):
    return lax.dot_general(a, b, (((1,), (1,)), ((), ())), preferred_element_type=F32)


def _dot_tn(a, b):
    return lax.dot_general(a, b, (((0,), (0,)), ((), ())), preferred_element_type=F32)


def _rmsnorm(x, g):
    ms = jnp.mean(x * x, axis=-1, keepdims=True)
    return x * lax.rsqrt(ms + EPS) * g


def _silu(x):
    return x * jax.nn.sigmoid(x)


def _iota(shape, dim):
    return lax.broadcasted_iota(jnp.int32, shape, dim)


def _rope_grouped(t, cos, sin):
    t0, t1, t2, t3 = (t[:, i * LANES:(i + 1) * LANES] for i in range(4))
    return jnp.concatenate(
        [t0 * cos - t1 * sin, t1 * cos + t0 * sin, t2 * cos - t3 * sin, t3 * cos + t2 * sin], axis=1)


def _group_stats_norm(o, ones_blk, gain):
    mu = _dot(o.astype(BF16), ones_blk) * (1.0 / HEAD_DIM)
    d = o - mu
    var = _dot((d * d).astype(BF16), ones_blk) * (1.0 / HEAD_DIM)
    return d * lax.rsqrt(var + GN_EPS) * gain


def _ones_block():
    r = _iota((RET_DIM, RET_DIM), 0) >> 6
    c = _iota((RET_DIM, RET_DIM), 1) >> 6
    return jnp.where(r == c, 1.0, 0.0).astype(BF16)


def _prompt_mixer_kernel(x_ref, meta_ref, g1_ref, win_ref, wqk_ref, wconv_ref, rng_ref, wout_ref,
                         lgq_ref, lgv_ref, inv_ref,
                         x1_ref, convst_ref, retst_ref,
                         cos_ref, sin_ref, qdec_ref, kdec_ref, dmask_ref, ones_ref, smask_ref,
                         sdec_ref, tail_ref, state_ref, ubuf_ref, *, n_tiles):
    j = pl.program_id(0)
    n = pl.program_id(1)
    C = TILE

    def proj(xn, off):
        return _dot(xn, win_ref[:, off:off + 512])

    @pl.when((j == 0) & (n == 0))
    def _init():
        i_f = _iota((C, 1), 0).astype(F32)
        lgq = lgq_ref[...]
        qdec_ref[...] = jnp.exp((i_f + 1.0) * lgq)
        kdec_ref[...] = jnp.exp((C - 1.0 - i_f) * lgq)
        sdec_ref[...] = jnp.exp(C * lgv_ref[...])
        diff = (_iota((C, C), 0) - _iota((C, C), 1)).astype(F32)
        for h in range(RET_HEADS):
            dmask_ref[h] = jnp.where(
                diff >= 0, jnp.exp(jnp.maximum(diff, 0.0) * float(_LOG_GAMMA[h])), 0.0)
        ones_ref[...] = _ones_block()
        r = (_iota((GROUP, GROUP), 0) & (LANES - 1)) >> 5
        c = _iota((GROUP, GROUP), 1) >> 6
        smask = jnp.where(r == c, 1.0, 0.0)
        smask_ref[...] = smask

        xm = _rmsnorm(meta_ref[...], g1_ref[...]).astype(BF16)
        um = proj(xm, OFF_C) * proj(xm, OFF_H)
        km = _dot(xm, wqk_ref[:, RET_DIM:])
        vm = proj(xm, OFF_V).astype(BF16)
        m_f = _iota((N_META, 1), 0).astype(F32)
        ang = m_f * inv_ref[...]
        kmr = _rope_grouped(km, jnp.cos(ang), jnp.sin(ang))
        kmd = (kmr * jnp.exp((N_META - 1.0 - m_f) * lgq)).astype(BF16)
        for b in range(tail_ref.shape[0]):
            tail_ref[b] = um[N_META - 2:N_META, :]
            for g in range(N_GROUPS):
                sl = slice(g * GROUP, (g + 1) * GROUP)
                state_ref[b, g] = _dot_tn(kmd[:, sl], vm[:, sl]) * smask

    @pl.when(n == 0)
    def _rope_tables():
        pos = (N_META + j * C + _iota((C, LANES), 0)).astype(F32)
        ang = pos * inv_ref[...]
        cos_ref[...] = jnp.cos(ang)
        sin_ref[...] = jnp.sin(ang)

    x = x_ref[0]
    xn = _rmsnorm(x, g1_ref[...]).astype(BF16)

    zb = proj(xn, OFF_B)
    u = proj(xn, OFF_C) * proj(xn, OFF_H)
    ubuf_ref[6:8, :] = tail_ref[n]
    ubuf_ref[8:8 + C, :] = u
    wc = wconv_ref[...]
    conv = wc[0:1] * ubuf_ref[6:6 + C, :] + wc[1:2] * ubuf_ref[7:7 + C, :] + wc[2:3] * u
    conv_out = (zb * conv).astype(BF16)
    new_tail = u[C - 2:C, :]
    tail_ref[n] = new_tail

    cos = cos_ref[...]
    sin = sin_ref[...]
    qr = _rope_grouped(_dot(xn, wqk_ref[:, :RET_DIM]), cos, sin)
    kr = _rope_grouped(_dot(xn, wqk_ref[:, RET_DIM:]), cos, sin)
    vb = proj(xn, OFF_V).astype(BF16)
    qb = qr.astype(BF16)
    kb = kr.astype(BF16)
    qd = (qr * qdec_ref[...]).astype(BF16)
    kd = (kr * kdec_ref[...]).astype(BF16)
    head_of_qlane = (_iota((1, GROUP), 1) & (LANES - 1)) >> 5
    head_of_vlane = _iota((1, GROUP), 1) >> 6
    smask = smask_ref[...]
    o_parts = []
    new_states = []
    for g in range(N_GROUPS):
        sl = slice(g * GROUP, (g + 1) * GROUP)
        state = state_ref[n, g]
        acc = _dot(qd[:, sl], state.astype(BF16))
        for hh in range(HEADS_PER_GROUP):
            k_h = jnp.where(head_of_qlane == hh, kb[:, sl], jnp.zeros_like(kb[:, sl]))
            v_h = jnp.where(head_of_vlane == hh, vb[:, sl], jnp.zeros_like(vb[:, sl]))
            scores = _dot_nt(qb[:, sl], k_h) * dmask_ref[g * HEADS_PER_GROUP + hh]
            acc = acc + _dot(scores.astype(BF16), v_h)
        new_state = state * sdec_ref[:, sl] + _dot_tn(kd[:, sl], vb[:, sl]) * smask
        state_ref[n, g] = new_state
        new_states.append(new_state)
        o_parts.append(acc)
    o = jnp.concatenate(o_parts, axis=1)
    on = _group_stats_norm(o, ones_ref[...], rng_ref[...])
    ret_out = (_silu(proj(xn, OFF_G)) * on).astype(BF16)

    mix = jnp.concatenate([conv_out, ret_out], axis=1)
    x1_ref[0] = x + _dot(mix, wout_ref[...])

    @pl.when(j == n_tiles - 1)
    def _final_states():
        convst_ref[n] = new_tail
        for g in range(N_GROUPS):
            s = new_states[g]
            for hh in range(HEADS_PER_GROUP):
                h = g * HEADS_PER_GROUP + hh
                cols = slice(hh * HEAD_DIM, (hh + 1) * HEAD_DIM)
                retst_ref[n, h, 0:HALF_DIM, :] = s[hh * HALF_DIM:(hh + 1) * HALF_DIM, cols]
                retst_ref[n, h, HALF_DIM:HEAD_DIM, :] = (
                    s[LANES + hh * HALF_DIM:LANES + (hh + 1) * HALF_DIM, cols])


def _prompt_mixer(x_prompt, meta, g1, w_in_b, w_qk_b, w_conv, rng, w_out_b, lgq, lgv, inv_lane):
    n_seq, seq, _ = x_prompt.shape
    n_tiles = seq // TILE
    const2 = lambda j, n: (0, 0)
    full = lambda a: pl.BlockSpec(a.shape, const2)
    return pl.pallas_call(
        functools.partial(_prompt_mixer_kernel, n_tiles=n_tiles),
        grid=(n_tiles, n_seq),
        in_specs=[pl.BlockSpec((1, TILE, D_MODEL), lambda j, n: (n, j, 0)),
                  full(meta), full(g1), full(w_in_b), full(w_qk_b), full(w_conv), full(rng),
                  full(w_out_b), full(lgq), full(lgv), full(inv_lane)],
        out_specs=[pl.BlockSpec((1, TILE, D_MODEL), lambda j, n: (n, j, 0)),
                   pl.BlockSpec((n_seq, 2, CONV_DIM), lambda j, n: (0, 0, 0)),
                   pl.BlockSpec((n_seq, RET_HEADS, HEAD_DIM, HEAD_DIM), lambda j, n: (0, 0, 0, 0))],
        out_shape=[jax.ShapeDtypeStruct((n_seq, seq, D_MODEL), F32),
                   jax.ShapeDtypeStruct((n_seq, 2, CONV_DIM), F32),
                   jax.ShapeDtypeStruct((n_seq, RET_HEADS, HEAD_DIM, HEAD_DIM), F32)],
        scratch_shapes=[
            pltpu.VMEM((TILE, LANES), F32),
            pltpu.VMEM((TILE, LANES), F32),
            pltpu.VMEM((TILE, RET_DIM), F32),
            pltpu.VMEM((TILE, RET_DIM), F32),
            pltpu.VMEM((RET_HEADS, TILE, TILE), F32),
            pltpu.VMEM((RET_DIM, RET_DIM), BF16),
            pltpu.VMEM((GROUP, GROUP), F32),
            pltpu.VMEM((1, RET_DIM), F32),
            pltpu.VMEM((n_seq, 2, CONV_DIM), F32),
            pltpu.VMEM((n_seq, N_GROUPS, GROUP, GROUP), F32),
            pltpu.VMEM((TILE + 8, CONV_DIM), F32),
        ],
        compiler_params=pltpu.CompilerParams(
            dimension_semantics=("arbitrary", "arbitrary"), vmem_limit_bytes=VMEM_LIMIT),
        name="prompt_mixer",
    )(x_prompt, meta, g1, w_in_b, w_qk_b, w_conv, rng, w_out_b, lgq, lgv, inv_lane)


def _sample_mixer_kernel(x_ref, p1_ref, p2_ref, s_ref, g1_ref, win_ref, wconv_ref, rng_ref,
                         wout_ref, lgv_ref, inv_ref,
                         x1_ref, u_ref, snew_ref,
                         qd_ref, kd_ref, v_ref, o_ref, zg_ref, cvo_ref, ones_ref,
                         *, n_steps, dec_seq):
    s = pl.program_id(0)
    L = dec_seq
    rows = x_ref.shape[0]

    @pl.when(s == 0)
    def _dense_front():
        x = x_ref[...]
        xn = _rmsnorm(x, g1_ref[...]).astype(BF16)

        def proj(off):
            return _dot(xn, win_ref[:, off:off + 512])

        t_i = _iota((rows, 1), 0) & (L - 1)
        t_f = t_i.astype(F32)
        u = proj(OFF_C) * proj(OFF_H)
        u_ref[...] = u
        um1 = jnp.where(t_i == 0, p1_ref[...], pltpu.roll(u, 1, axis=0))
        um2 = jnp.where(t_i < 2, p2_ref[...], pltpu.roll(u, 2, axis=0))
        wc = wconv_ref[...]
        conv = wc[0:1] * um2 + wc[1:2] * um1 + wc[2:3] * u
        cvo_ref[...] = (proj(OFF_B) * conv).astype(BF16)
        zg_ref[...] = proj(OFF_G)

        ang = (t_f + float(PAST_LEN)) * inv_ref[...]
        cos1 = jnp.cos(ang)
        sin1 = jnp.sin(ang)
        cos = jnp.concatenate([cos1] * 4, axis=1)
        sin = jnp.concatenate([sin1] * 4, axis=1)
        first_half = (_iota((1, RET_DIM), 1) & (HEAD_DIM - 1)) < HALF_DIM
        sgn_sin = jnp.where(first_half, -sin, sin)

        def rope(t):
            rot = jnp.where(first_half, pltpu.roll(t, RET_DIM - HALF_DIM, axis=1),
                            pltpu.roll(t, HALF_DIM, axis=1))
            return t * cos + rot * sgn_sin

        lgv = lgv_ref[...]
        qr = rope(proj(OFF_Q))
        kr = rope(proj(OFF_K))
        v = proj(OFF_V)
        qd_ref[...] = qr * jnp.exp((t_f + 1.0) * lgv)
        kd_ref[...] = kr * jnp.exp((L - 1.0 - t_f) * lgv)
        v_ref[...] = v
        ones = _ones_block()
        ones_ref[...] = ones
        intra = jnp.zeros((rows, RET_DIM), F32)
        for d in range(L):
            k_d = kr if d == 0 else pltpu.roll(kr, d, axis=0)
            v_d = v if d == 0 else pltpu.roll(v, d, axis=0)
            p = _dot((qr * k_d).astype(BF16), ones)
            term = p * jnp.exp(float(d) * lgv) * v_d
            intra = intra + jnp.where(t_i >= d, term, 0.0)
        o_ref[...] = intra

    first_seq = _iota((2 * L, 1), 0) < L
    for pair in range(SEQ_BLOCK // 2):
        row0 = pl.multiple_of((s * SEQ_BLOCK + 2 * pair) * L, 2 * L)
        q8 = qd_ref[pl.ds(row0, 2 * L), :].astype(BF16)
        k8 = kd_ref[pl.ds(row0, 2 * L), :]
        v8 = v_ref[pl.ds(row0, 2 * L), :].astype(BF16)
        k_of = (jnp.where(first_seq, k8, 0.0).astype(BF16), jnp.where(first_seq, 0.0, k8).astype(BF16))
        cross = []
        for h in range(RET_HEADS):
            cols = slice(h * HEAD_DIM, (h + 1) * HEAD_DIM)
            per_seq = []
            for e in range(2):
                state = s_ref[2 * pair + e, h]
                per_seq.append(_dot(q8[:, cols], state.astype(BF16)))
                snew_ref[2 * pair + e, h] = (state * float(np.exp(np.float32(L) * _LOG_GAMMA[h]))
                                             + _dot_tn(k_of[e][:, cols], v8[:, cols]))
            cross.append(jnp.where(first_seq, per_seq[0], per_seq[1]))
        o_ref[pl.ds(row0, 2 * L), :] = o_ref[pl.ds(row0, 2 * L), :] + jnp.concatenate(cross, axis=1)

    @pl.when(s == n_steps - 1)
    def _dense_back():
        on = _group_stats_norm(o_ref[...], ones_ref[...], rng_ref[...])
        ret_out = (_silu(zg_ref[...]) * on).astype(BF16)
        mix = jnp.concatenate([cvo_ref[...], ret_out], axis=1)
        x1_ref[...] = x_ref[...] + _dot(mix, wout_ref[...])


def _sample_mixer(xs, p1, p2, state_ret, g1, w_in_b, w_conv, rng, w_out_b, lgv, inv_lane, dec_seq):
    rows = xs.shape[0]
    n_seq = state_ret.shape[0]
    n_steps = n_seq // SEQ_BLOCK
    full = lambda a: pl.BlockSpec(a.shape, lambda s: (0,) * a.ndim)
    state_spec = pl.BlockSpec((SEQ_BLOCK, RET_HEADS, HEAD_DIM, HEAD_DIM), lambda s: (s, 0, 0, 0))
    return pl.pallas_call(
        functools.partial(_sample_mixer_kernel, n_steps=n_steps, dec_seq=dec_seq),
        grid=(n_steps,),
        in_specs=[full(xs), full(p1), full(p2), state_spec, full(g1), full(w_in_b), full(w_conv),
                  full(rng), full(w_out_b), full(lgv), full(inv_lane)],
        out_specs=[pl.BlockSpec((rows, D_MODEL), lambda s: (0, 0)),
                   pl.BlockSpec((rows, CONV_DIM), lambda s: (0, 0)),
                   state_spec],
        out_shape=[jax.ShapeDtypeStruct((rows, D_MODEL), F32),
                   jax.ShapeDtypeStruct((rows, CONV_DIM), F32),
                   jax.ShapeDtypeStruct(state_ret.shape, F32)],
        scratch_shapes=[
            pltpu.VMEM((rows, RET_DIM), F32),
            pltpu.VMEM((rows, RET_DIM), F32),
            pltpu.VMEM((rows, RET_DIM), F32),
            pltpu.VMEM((rows, RET_DIM), F32),
            pltpu.VMEM((rows, RET_DIM), F32),
            pltpu.VMEM((rows, CONV_DIM), BF16),
            pltpu.VMEM((RET_DIM, RET_DIM), BF16),
        ],
        compiler_params=pltpu.CompilerParams(
            dimension_semantics=("arbitrary",), vmem_limit_bytes=VMEM_LIMIT),
        name="sample_mixer",
    )(xs, p1, p2, state_ret, g1, w_in_b, w_conv, rng, w_out_b, lgv, inv_lane)


def _ffn_kernel(x_ref, g2_ref, wg_ref, wu_ref, wd_ref, gf_ref, y_ref):
    x = x_ref[...]
    xn = _rmsnorm(x, g2_ref[...]).astype(BF16)
    hidden = (_silu(_dot(xn, wg_ref[...])) * _dot(xn, wu_ref[...])).astype(BF16)
    y_ref[...] = _rmsnorm(x + _dot(hidden, wd_ref[...]), gf_ref[...])


def _ffn(x, g2, wg_b, wu_b, wd_b, gf):
    rows = x.shape[0]
    full = lambda a: pl.BlockSpec(a.shape, lambda i: (0, 0))
    tile = pl.BlockSpec((FFN_TILE, D_MODEL), lambda i: (i, 0))
    return pl.pallas_call(
        _ffn_kernel,
        grid=(rows // FFN_TILE,),
        in_specs=[tile, full(g2), full(wg_b), full(wu_b), full(wd_b), full(gf)],
        out_specs=tile,
        out_shape=jax.ShapeDtypeStruct((rows, D_MODEL), F32),
        compiler_params=pltpu.CompilerParams(
            dimension_semantics=("arbitrary",), vmem_limit_bytes=VMEM_LIMIT),
        name="ffn",
    )(x, g2, wg_b, wu_b, wd_b, gf)


def _grouped_perm():
    idx = np.arange(RET_DIM).reshape(N_GROUPS, HEADS_PER_GROUP, 2, HALF_DIM)
    return idx.transpose(0, 2, 1, 3).reshape(-1)


def kernel(x_prompt, x_sample, state_conv, state_ret, meta_tokens, norm1_g, w_in, w_conv, ret_norm_g,
           w_out, norm2_g, w_gate, w_up, w_down, final_norm_g):
    n_p, seq, _ = x_prompt.shape
    n_s, dec_seq, _ = x_sample.shape
    assert norm1_g.shape[0] == 1 and seq % TILE == 0 and n_s % SEQ_BLOCK == 0
    assert dec_seq == 4 and (n_p * seq) % FFN_TILE == 0 and (n_s * dec_seq) % FFN_TILE == 0

    col_scale = np.ones((1, w_in.shape[-1]), np.float32)
    col_scale[:, OFF_K:OFF_V] = HEAD_DIM ** -0.5
    w_in_b = (w_in[0] * col_scale).astype(BF16)
    perm = _grouped_perm()
    w_qk_b = jnp.concatenate([w_in_b[:, OFF_Q:OFF_K][:, perm], w_in_b[:, OFF_K:OFF_V][:, perm]], axis=1)
    w_out_b = w_out[0].astype(BF16)
    wg_b, wu_b, wd_b = w_gate[0].astype(BF16), w_up[0].astype(BF16), w_down[0].astype(BF16)
    g1, g2, gf = norm1_g[0][None], norm2_g[0][None], final_norm_g[None]
    rng = ret_norm_g[0][None]

    head_of_qlane = (np.arange(RET_DIM) // GROUP) * HEADS_PER_GROUP + (np.arange(RET_DIM) % LANES) // HALF_DIM
    lgq = jnp.asarray(_LOG_GAMMA[head_of_qlane][None])
    lgv = jnp.asarray(_LOG_GAMMA[np.arange(RET_DIM) // HEAD_DIM][None])
    inv = ROPE_BASE ** (-jnp.arange(HALF_DIM, dtype=F32) / HALF_DIM)
    inv_lane = jnp.tile(inv, LANES // HALF_DIM)[None]

    x1_p, conv_p, ret_p = _prompt_mixer(x_prompt, meta_tokens, g1, w_in_b, w_qk_b, w_conv[0], rng,
                                        w_out_b, lgq, lgv, inv_lane)
    y_prompt = _ffn(x1_p.reshape(n_p * seq, D_MODEL), g2, wg_b, wu_b, wd_b, gf).reshape(n_p, seq, D_MODEL)

    sc = state_conv[0]
    zeros = jnp.zeros_like(sc[:, 0])
    p1 = jnp.stack([sc[:, 1], zeros, zeros, zeros], axis=1).reshape(n_s * dec_seq, CONV_DIM)
    p2 = jnp.stack([sc[:, 0], sc[:, 1], zeros, zeros], axis=1).reshape(n_s * dec_seq, CONV_DIM)
    x1_s, u_s, ret_s = _sample_mixer(x_sample.reshape(n_s * dec_seq, D_MODEL), p1, p2, state_ret[0], g1,
                                     w_in_b, w_conv[0], rng, w_out_b, lgv, inv_lane, dec_seq)
    y_sample = _ffn(x1_s, g2, wg_b, wu_b, wd_b, gf).reshape(n_s, dec_seq, D_MODEL)
    conv_s = u_s.reshape(n_s, dec_seq, CONV_DIM)[:, dec_seq - 2:]

    return (y_prompt, y_sample, conv_p[None], ret_p[None], conv_s[None], ret_s[None])
```

```python
import functools

import numpy as np
import jax
import jax.numpy as jnp
from jax import lax
from jax.experimental import pallas as pl
from jax.experimental.pallas import tpu as pltpu

D_MODEL = 1024
N_META = 16
CONV_DIM = 512
RET_HEADS = 8
HEAD_DIM = 64
HALF_DIM = HEAD_DIM // 2
RET_DIM = RET_HEADS * HEAD_DIM
D_FF = 2816
PAST_LEN = 16384
ROPE_BASE = 10000.0
EPS = 1e-6
GN_EPS = 1e-5

OFF_B, OFF_C, OFF_H, OFF_Q, OFF_K, OFF_V, OFF_G = (i * 512 for i in range(7))

LANES = 128
GROUP = 256
HEADS_PER_GROUP = GROUP // HEAD_DIM
N_GROUPS = RET_DIM // GROUP
TILE = 256
FFN_TILE = 512
FFN_CHUNK = 256
SEQ_BLOCK = 8
VMEM_LIMIT = 56 * 1024 * 1024

F32 = jnp.float32
BF16 = jnp.bfloat16

_LOG_GAMMA = np.log1p(-(2.0 ** (-5.0 - np.arange(RET_HEADS)))).astype(np.float32)


def _dot(a, b):
    return jnp.dot(a, b, preferred_element_type=F32)


def _dot_nt(a, b):
    return lax.dot_general(a, b, (((1,), (1,)), ((), ())), preferred_element_type=F32)


def _dot_tn(a, b):
    return lax.dot_general(a, b, (((0,), (0,)), ((), ())), preferred_element_type=F32)


def _rmsnorm(x, g):
    ms = jnp.mean(x * x, axis=-1, keepdims=True)
    return x * lax.rsqrt(ms + EPS) * g


def _silu(x):
    return x * jax.nn.sigmoid(x)


def _iota(shape, dim):
    return lax.broadcasted_iota(jnp.int32, shape, dim)


def _rope_grouped(t, cos, sin):
    t0, t1, t2, t3 = (t[:, i * LANES:(i + 1) * LANES] for i in range(4))
    return jnp.concatenate(
        [t0 * cos - t1 * sin, t1 * cos + t0 * sin, t2 * cos - t3 * sin, t3 * cos + t2 * sin], axis=1)


def _segment_sum(t, ones_blk):
    return jnp.concatenate(
        [_dot(t[:, g * GROUP:(g + 1) * GROUP].astype(BF16), ones_blk) for g in range(N_GROUPS)], axis=1)


def _group_stats_norm(o, ones_blk, gain):
    mu = _segment_sum(o, ones_blk) * (1.0 / HEAD_DIM)
    d = o - mu
    var = _segment_sum(d * d, ones_blk) * (1.0 / HEAD_DIM)
    return d * lax.rsqrt(var + GN_EPS) * gain


def _ones_block():
    r = _iota((GROUP, GROUP), 0) >> 6
    c = _iota((GROUP, GROUP), 1) >> 6
    return jnp.where(r == c, 1.0, 0.0).astype(BF16)


def _prompt_mixer_kernel(x_ref, meta_ref, g1_ref, win_ref, wqk_ref, wconv_ref, rng_ref, wout_ref,
                         lgq_ref, lgv_ref, inv_ref,
                         x1_ref, convst_ref, retst_ref,
                         cos_ref, sin_ref, qdec_ref, kdec_ref, dmask_ref, ones_ref, smask_ref,
                         sdec_ref, tail_ref, state_ref, ubuf_ref, *, n_tiles):
    j = pl.program_id(0)
    n = pl.program_id(1)
    C = TILE

    def proj(xn, off):
        return _dot(xn, win_ref[:, off:off + 512])

    @pl.when((j == 0) & (n == 0))
    def _init():
        i_f = _iota((C, 1), 0).astype(F32)
        lgq = lgq_ref[...]
        qdec_ref[...] = jnp.exp((i_f + 1.0) * lgq)
        kdec_ref[...] = jnp.exp((C - 1.0 - i_f) * lgq)
        sdec_ref[...] = jnp.exp(C * lgv_ref[...])
        diff = (_iota((C, C), 0) - _iota((C, C), 1)).astype(F32)
        for h in range(RET_HEADS):
            dmask_ref[h] = jnp.where(
                diff >= 0, jnp.exp(jnp.maximum(diff, 0.0) * float(_LOG_GAMMA[h])), 0.0)
        ones_ref[...] = _ones_block()
        r = (_iota((GROUP, GROUP), 0) & (LANES - 1)) >> 5
        c = _iota((GROUP, GROUP), 1) >> 6
        smask = jnp.where(r == c, 1.0, 0.0)
        smask_ref[...] = smask

        xm = _rmsnorm(meta_ref[...], g1_ref[...]).astype(BF16)
        um = proj(xm, OFF_C) * proj(xm, OFF_H)
        km = _dot(xm, wqk_ref[:, RET_DIM:])
        vm = proj(xm, OFF_V).astype(BF16)
        m_f = _iota((N_META, 1), 0).astype(F32)
        ang = m_f * inv_ref[...]
        kmr = _rope_grouped(km, jnp.cos(ang), jnp.sin(ang))
        kmd = (kmr * jnp.exp((N_META - 1.0 - m_f) * lgq)).astype(BF16)
        for b in range(tail_ref.shape[0]):
            tail_ref[b] = um[N_META - 2:N_META, :]
            for g in range(N_GROUPS):
                sl = slice(g * GROUP, (g + 1) * GROUP)
                state_ref[b, g] = _dot_tn(kmd[:, sl], vm[:, sl]) * smask

    @pl.when(n == 0)
    def _rope_tables():
        pos = (N_META + j * C + _iota((C, LANES), 0)).astype(F32)
        ang = pos * inv_ref[...]
        cos_ref[...] = jnp.cos(ang)
        sin_ref[...] = jnp.sin(ang)

    x = x_ref[0]
    xn = _rmsnorm(x, g1_ref[...]).astype(BF16)

    zb = proj(xn, OFF_B)
    u = proj(xn, OFF_C) * proj(xn, OFF_H)
    ubuf_ref[6:8, :] = tail_ref[n]
    ubuf_ref[8:8 + C, :] = u
    wc = wconv_ref[...]
    conv = wc[0:1] * ubuf_ref[6:6 + C, :] + wc[1:2] * ubuf_ref[7:7 + C, :] + wc[2:3] * u
    conv_out = (zb * conv).astype(BF16)
    new_tail = u[C - 2:C, :]
    tail_ref[n] = new_tail

    cos = cos_ref[...]
    sin = sin_ref[...]
    qr = _rope_grouped(_dot(xn, wqk_ref[:, :RET_DIM]), cos, sin)
    kr = _rope_grouped(_dot(xn, wqk_ref[:, RET_DIM:]), cos, sin)
    vb = proj(xn, OFF_V).astype(BF16)
    qb = qr.astype(BF16)
    kb = kr.astype(BF16)
    qd = (qr * qdec_ref[...]).astype(BF16)
    kd = (kr * kdec_ref[...]).astype(BF16)
    head_of_qlane = (_iota((1, GROUP), 1) & (LANES - 1)) >> 5
    head_of_vlane = _iota((1, GROUP), 1) >> 6
    smask = smask_ref[...]
    o_parts = []
    new_states = []
    for g in range(N_GROUPS):
        sl = slice(g * GROUP, (g + 1) * GROUP)
        state = state_ref[n, g]
        acc = _dot(qd[:, sl], state.astype(BF16))
        for hh in range(HEADS_PER_GROUP):
            k_h = jnp.where(head_of_qlane == hh, kb[:, sl], jnp.zeros_like(kb[:, sl]))
            v_h = jnp.where(head_of_vlane == hh, vb[:, sl], jnp.zeros_like(vb[:, sl]))
            scores = _dot_nt(qb[:, sl], k_h) * dmask_ref[g * HEADS_PER_GROUP + hh]
            acc = acc + _dot(scores.astype(BF16), v_h)
        new_state = state * sdec_ref[:, sl] + _dot_tn(kd[:, sl], vb[:, sl]) * smask
        state_ref[n, g] = new_state
        new_states.append(new_state)
        o_parts.append(acc)
    o = jnp.concatenate(o_parts, axis=1)
    on = _group_stats_norm(o, ones_ref[...], rng_ref[...])
    ret_out = (_silu(proj(xn, OFF_G)) * on).astype(BF16)

    mix = jnp.concatenate([conv_out, ret_out], axis=1)
    x1_ref[0] = x + _dot(mix, wout_ref[...])

    @pl.when(j == n_tiles - 1)
    def _final_states():
        convst_ref[n] = new_tail
        for g in range(N_GROUPS):
            s = new_states[g]
            for hh in range(HEADS_PER_GROUP):
                h = g * HEADS_PER_GROUP + hh
                cols = slice(hh * HEAD_DIM, (hh + 1) * HEAD_DIM)
                retst_ref[n, h, 0:HALF_DIM, :] = s[hh * HALF_DIM:(hh + 1) * HALF_DIM, cols]
                retst_ref[n, h, HALF_DIM:HEAD_DIM, :] = (
                    s[LANES + hh * HALF_DIM:LANES + (hh + 1) * HALF_DIM, cols])


def _prompt_mixer(x_prompt, meta, g1, w_in_b, w_qk_b, w_conv, rng, w_out_b, lgq, lgv, inv_lane):
    n_seq, seq, _ = x_prompt.shape
    n_tiles = seq // TILE
    const2 = lambda j, n: (0, 0)
    full = lambda a: pl.BlockSpec(a.shape, const2, pipeline_mode=pl.Buffered(1))
    return pl.pallas_call(
        functools.partial(_prompt_mixer_kernel, n_tiles=n_tiles),
        grid=(n_tiles, n_seq),
        in_specs=[pl.BlockSpec((1, TILE, D_MODEL), lambda j, n: (n, j, 0)),
                  full(meta), full(g1), full(w_in_b), full(w_qk_b), full(w_conv), full(rng),
                  full(w_out_b), full(lgq), full(lgv), full(inv_lane)],
        out_specs=[pl.BlockSpec((1, TILE, D_MODEL), lambda j, n: (n, j, 0)),
                   pl.BlockSpec((n_seq, 2, CONV_DIM), lambda j, n: (0, 0, 0)),
                   pl.BlockSpec((n_seq, RET_HEADS, HEAD_DIM, HEAD_DIM), lambda j, n: (0, 0, 0, 0))],
        out_shape=[jax.ShapeDtypeStruct((n_seq, seq, D_MODEL), F32),
                   jax.ShapeDtypeStruct((n_seq, 2, CONV_DIM), F32),
                   jax.ShapeDtypeStruct((n_seq, RET_HEADS, HEAD_DIM, HEAD_DIM), F32)],
        scratch_shapes=[
            pltpu.VMEM((TILE, LANES), F32),
            pltpu.VMEM((TILE, LANES), F32),
            pltpu.VMEM((TILE, RET_DIM), F32),
            pltpu.VMEM((TILE, RET_DIM), F32),
            pltpu.VMEM((RET_HEADS, TILE, TILE), F32),
            pltpu.VMEM((GROUP, GROUP), BF16),
            pltpu.VMEM((GROUP, GROUP), F32),
            pltpu.VMEM((1, RET_DIM), F32),
            pltpu.VMEM((n_seq, 2, CONV_DIM), F32),
            pltpu.VMEM((n_seq, N_GROUPS, GROUP, GROUP), F32),
            pltpu.VMEM((TILE + 8, CONV_DIM), F32),
        ],
        compiler_params=pltpu.CompilerParams(
            dimension_semantics=("arbitrary", "arbitrary"), vmem_limit_bytes=VMEM_LIMIT),
        name="prompt_mixer",
    )(x_prompt, meta, g1, w_in_b, w_qk_b, w_conv, rng, w_out_b, lgq, lgv, inv_lane)


def _sample_mixer_kernel(x_ref, p1_ref, p2_ref, s_ref, g1_ref, win_ref, wconv_ref, rng_ref,
                         wout_ref, lgv_ref, inv_ref,
                         x1_ref, u_ref, snew_ref,
                         qd_ref, kd_ref, v_ref, o_ref, zg_ref, cvo_ref, ones_ref,
                         *, n_steps, dec_seq):
    s = pl.program_id(0)
    L = dec_seq
    rows = x_ref.shape[0]

    @pl.when(s == 0)
    def _dense_front():
        x = x_ref[...]
        xn = _rmsnorm(x, g1_ref[...]).astype(BF16)

        def proj(off):
            return _dot(xn, win_ref[:, off:off + 512])

        t_i = _iota((rows, 1), 0) & (L - 1)
        t_f = t_i.astype(F32)
        u = proj(OFF_C) * proj(OFF_H)
        u_ref[...] = u
        um1 = jnp.where(t_i == 0, p1_ref[...], pltpu.roll(u, 1, axis=0))
        um2 = jnp.where(t_i < 2, p2_ref[...], pltpu.roll(u, 2, axis=0))
        wc = wconv_ref[...]
        conv = wc[0:1] * um2 + wc[1:2] * um1 + wc[2:3] * u
        cvo_ref[...] = (proj(OFF_B) * conv).astype(BF16)
        zg_ref[...] = proj(OFF_G)

        ang = (t_f + float(PAST_LEN)) * inv_ref[...]
        cos1 = jnp.cos(ang)
        sin1 = jnp.sin(ang)
        cos = jnp.concatenate([cos1] * 4, axis=1)
        sin = jnp.concatenate([sin1] * 4, axis=1)
        first_half = (_iota((1, RET_DIM), 1) & (HEAD_DIM - 1)) < HALF_DIM
        sgn_sin = jnp.where(first_half, -sin, sin)

        def rope(t):
            rot = jnp.where(first_half, pltpu.roll(t, RET_DIM - HALF_DIM, axis=1),
                            pltpu.roll(t, HALF_DIM, axis=1))
            return t * cos + rot * sgn_sin

        lgv = lgv_ref[...]
        qr = rope(proj(OFF_Q))
        kr = rope(proj(OFF_K))
        v = proj(OFF_V)
        qd_ref[...] = qr * jnp.exp((t_f + 1.0) * lgv)
        kd_ref[...] = kr * jnp.exp((L - 1.0 - t_f) * lgv)
        v_ref[...] = v
        ones = _ones_block()
        ones_ref[...] = ones
        intra = jnp.zeros((rows, RET_DIM), F32)
        for d in range(L):
            k_d = kr if d == 0 else pltpu.roll(kr, d, axis=0)
            v_d = v if d == 0 else pltpu.roll(v, d, axis=0)
            p = _segment_sum(qr * k_d, ones)
            term = p * jnp.exp(float(d) * lgv) * v_d
            intra = intra + jnp.where(t_i >= d, term, 0.0)
        o_ref[...] = intra

    first_seq = _iota((2 * L, 1), 0) < L
    for pair in range(SEQ_BLOCK // 2):
        row0 = pl.multiple_of((s * SEQ_BLOCK + 2 * pair) * L, 2 * L)
        q8 = qd_ref[pl.ds(row0, 2 * L), :].astype(BF16)
        k8 = kd_ref[pl.ds(row0, 2 * L), :]
        v8 = v_ref[pl.ds(row0, 2 * L), :].astype(BF16)
        k_of = (jnp.where(first_seq, k8, 0.0).astype(BF16), jnp.where(first_seq, 0.0, k8).astype(BF16))
        cross = []
        for h in range(RET_HEADS):
            cols = slice(h * HEAD_DIM, (h + 1) * HEAD_DIM)
            per_seq = []
            for e in range(2):
                state = s_ref[2 * pair + e, h]
                per_seq.append(_dot(q8[:, cols], state.astype(BF16)))
                snew_ref[2 * pair + e, h] = (state * float(np.exp(np.float32(L) * _LOG_GAMMA[h]))
                                             + _dot_tn(k_of[e][:, cols], v8[:, cols]))
            cross.append(jnp.where(first_seq, per_seq[0], per_seq[1]))
        o_ref[pl.ds(row0, 2 * L), :] = o_ref[pl.ds(row0, 2 * L), :] + jnp.concatenate(cross, axis=1)

    @pl.when(s == n_steps - 1)
    def _dense_back():
        on = _group_stats_norm(o_ref[...], ones_ref[...], rng_ref[...])
        ret_out = (_silu(zg_ref[...]) * on).astype(BF16)
        mix = jnp.concatenate([cvo_ref[...], ret_out], axis=1)
        x1_ref[...] = x_ref[...] + _dot(mix, wout_ref[...])


def _sample_mixer(xs, p1, p2, state_ret, g1, w_in_b, w_conv, rng, w_out_b, lgv, inv_lane, dec_seq):
    rows = xs.shape[0]
    n_seq = state_ret.shape[0]
    n_steps = n_seq // SEQ_BLOCK
    full = lambda a: pl.BlockSpec(a.shape, lambda s: (0,) * a.ndim, pipeline_mode=pl.Buffered(1))
    state_spec = pl.BlockSpec((SEQ_BLOCK, RET_HEADS, HEAD_DIM, HEAD_DIM), lambda s: (s, 0, 0, 0))
    return pl.pallas_call(
        functools.partial(_sample_mixer_kernel, n_steps=n_steps, dec_seq=dec_seq),
        grid=(n_steps,),
        in_specs=[full(xs), full(p1), full(p2), state_spec, full(g1), full(w_in_b), full(w_conv),
                  full(rng), full(w_out_b), full(lgv), full(inv_lane)],
        out_specs=[pl.BlockSpec((rows, D_MODEL), lambda s: (0, 0)),
                   pl.BlockSpec((rows, CONV_DIM), lambda s: (0, 0)),
                   state_spec],
        out_shape=[jax.ShapeDtypeStruct((rows, D_MODEL), F32),
                   jax.ShapeDtypeStruct((rows, CONV_DIM), F32),
                   jax.ShapeDtypeStruct(state_ret.shape, F32)],
        scratch_shapes=[
            pltpu.VMEM((rows, RET_DIM), F32),
            pltpu.VMEM((rows, RET_DIM), F32),
            pltpu.VMEM((rows, RET_DIM), F32),
            pltpu.VMEM((rows, RET_DIM), F32),
            pltpu.VMEM((rows, RET_DIM), F32),
            pltpu.VMEM((rows, CONV_DIM), BF16),
            pltpu.VMEM((GROUP, GROUP), BF16),
        ],
        compiler_params=pltpu.CompilerParams(
            dimension_semantics=("arbitrary",), vmem_limit_bytes=VMEM_LIMIT),
        name="sample_mixer",
    )(xs, p1, p2, state_ret, g1, w_in_b, w_conv, rng, w_out_b, lgv, inv_lane)


def _ffn_kernel(x_ref, g2_ref, wg_ref, wu_ref, wd_ref, gf_ref, y_ref):
    for c in range(FFN_TILE // FFN_CHUNK):
        rows = slice(c * FFN_CHUNK, (c + 1) * FFN_CHUNK)
        x = x_ref[rows, :]
        xn = _rmsnorm(x, g2_ref[...]).astype(BF16)
        hidden = (_silu(_dot(xn, wg_ref[...])) * _dot(xn, wu_ref[...])).astype(BF16)
        y_ref[rows, :] = _rmsnorm(x + _dot(hidden, wd_ref[...]), gf_ref[...])


def _ffn(x, g2, wg_b, wu_b, wd_b, gf):
    rows = x.shape[0]
    full = lambda a: pl.BlockSpec(a.shape, lambda i: (0, 0), pipeline_mode=pl.Buffered(1))
    tile = pl.BlockSpec((FFN_TILE, D_MODEL), lambda i: (i, 0))
    return pl.pallas_call(
        _ffn_kernel,
        grid=(rows // FFN_TILE,),
        in_specs=[tile, full(g2), full(wg_b), full(wu_b), full(wd_b), full(gf)],
        out_specs=tile,
        out_shape=jax.ShapeDtypeStruct((rows, D_MODEL), F32),
        compiler_params=pltpu.CompilerParams(
            dimension_semantics=("arbitrary",), vmem_limit_bytes=VMEM_LIMIT),
        name="ffn",
    )(x, g2, wg_b, wu_b, wd_b, gf)


def _grouped_perm():
    idx = np.arange(RET_DIM).reshape(N_GROUPS, HEADS_PER_GROUP, 2, HALF_DIM)
    return idx.transpose(0, 2, 1, 3).reshape(-1)


def kernel(x_prompt, x_sample, state_conv, state_ret, meta_tokens, norm1_g, w_in, w_conv, ret_norm_g,
           w_out, norm2_g, w_gate, w_up, w_down, final_norm_g):
    n_p, seq, _ = x_prompt.shape
    n_s, dec_seq, _ = x_sample.shape
    assert norm1_g.shape[0] == 1 and seq % TILE == 0 and n_s % SEQ_BLOCK == 0
    assert dec_seq == 4 and (n_p * seq) % FFN_TILE == 0 and (n_s * dec_seq) % FFN_TILE == 0

    col_scale = np.ones((1, w_in.shape[-1]), np.float32)
    col_scale[:, OFF_K:OFF_V] = HEAD_DIM ** -0.5
    w_in_b = (w_in[0] * col_scale).astype(BF16)
    perm = _grouped_perm()
    w_qk_b = jnp.concatenate([w_in_b[:, OFF_Q:OFF_K][:, perm], w_in_b[:, OFF_K:OFF_V][:, perm]], axis=1)
    w_out_b = w_out[0].astype(BF16)
    wg_b, wu_b, wd_b = w_gate[0].astype(BF16), w_up[0].astype(BF16), w_down[0].astype(BF16)
    g1, g2, gf = norm1_g[0][None], norm2_g[0][None], final_norm_g[None]
    rng = ret_norm_g[0][None]

    head_of_qlane = (np.arange(RET_DIM) // GROUP) * HEADS_PER_GROUP + (np.arange(RET_DIM) % LANES) // HALF_DIM
    lgq = jnp.asarray(_LOG_GAMMA[head_of_qlane][None])
    lgv = jnp.asarray(_LOG_GAMMA[np.arange(RET_DIM) // HEAD_DIM][None])
    inv = ROPE_BASE ** (-jnp.arange(HALF_DIM, dtype=F32) / HALF_DIM)
    inv_lane = jnp.tile(inv, LANES // HALF_DIM)[None]

    x1_p, conv_p, ret_p = _prompt_mixer(x_prompt, meta_tokens, g1, w_in_b, w_qk_b, w_conv[0], rng,
                                        w_out_b, lgq, lgv, inv_lane)
    y_prompt = _ffn(x1_p.reshape(n_p * seq, D_MODEL), g2, wg_b, wu_b, wd_b, gf).reshape(n_p, seq, D_MODEL)

    sc = state_conv[0]
    zeros = jnp.zeros_like(sc[:, 0])
    p1 = jnp.stack([sc[:, 1], zeros, zeros, zeros], axis=1).reshape(n_s * dec_seq, CONV_DIM)
    p2 = jnp.stack([sc[:, 0], sc[:, 1], zeros, zeros], axis=1).reshape(n_s * dec_seq, CONV_DIM)
    x1_s, u_s, ret_s = _sample_mixer(x_sample.reshape(n_s * dec_seq, D_MODEL), p1, p2, state_ret[0], g1,
                                     w_in_b, w_conv[0], rng, w_out_b, lgv, inv_lane, dec_seq)
    y_sample = _ffn(x1_s, g2, wg_b, wu_b, wd_b, gf).reshape(n_s, dec_seq, D_MODEL)
    conv_s = u_s.reshape(n_s, dec_seq, CONV_DIM)[:, dec_seq - 2:]

    return (y_prompt, y_sample, conv_p[None], ret_p[None], conv_s[None], ret_s[None])
```

```python
import functools

import numpy as np
import jax
import jax.numpy as jnp
from jax import lax
from jax.experimental import pallas as pl
from jax.experimental.pallas import tpu as pltpu

D_MODEL = 1024
N_META = 16
CONV_DIM = 512
RET_HEADS = 8
HEAD_DIM = 64
HALF_DIM = HEAD_DIM // 2
RET_DIM = RET_HEADS * HEAD_DIM
D_FF = 2816
PAST_LEN = 16384
ROPE_BASE = 10000.0
EPS = 1e-6
GN_EPS = 1e-5

OFF_B, OFF_C, OFF_H, OFF_Q, OFF_K, OFF_V, OFF_G = (i * 512 for i in range(7))

LANES = 128
GROUP = 256
HEADS_PER_GROUP = GROUP // HEAD_DIM
N_GROUPS = RET_DIM // GROUP
TILE = 256
FFN_TILE = 512
FFN_CHUNK = 256
E_CHUNK = 32
VMEM_LIMIT = 56 * 1024 * 1024

F32 = jnp.float32
BF16 = jnp.bfloat16

_LOG_GAMMA = np.log1p(-(2.0 ** (-5.0 - np.arange(RET_HEADS)))).astype(np.float32)


def _dot(a, b):
    return jnp.dot(a, b, preferred_element_type=F32)


def _dot_nt(a, b):
    return lax.dot_general(a, b, (((1,), (1,)), ((), ())), preferred_element_type=F32)


def _dot_tn(a, b):
    return lax.dot_general(a, b, (((0,), (0,)), ((), ())), preferred_element_type=F32)


def _rmsnorm(x, g):
    ms = jnp.mean(x * x, axis=-1, keepdims=True)
    return x * lax.rsqrt(ms + EPS) * g


def _silu(x):
    return x * jax.nn.sigmoid(x)


def _iota(shape, dim):
    return lax.broadcasted_iota(jnp.int32, shape, dim)


def _rope_grouped(t, cos, sin):
    t0, t1, t2, t3 = (t[:, i * LANES:(i + 1) * LANES] for i in range(4))
    return jnp.concatenate(
        [t0 * cos - t1 * sin, t1 * cos + t0 * sin, t2 * cos - t3 * sin, t3 * cos + t2 * sin], axis=1)


def _segment_sum(t, ones_blk):
    return jnp.concatenate(
        [_dot(t[:, g * GROUP:(g + 1) * GROUP].astype(BF16), ones_blk) for g in range(N_GROUPS)], axis=1)


def _group_stats_norm(o, ones_blk, gain):
    mu = _segment_sum(o, ones_blk) * (1.0 / HEAD_DIM)
    d = o - mu
    var = _segment_sum(d * d, ones_blk) * (1.0 / HEAD_DIM)
    return d * lax.rsqrt(var + GN_EPS) * gain


def _ones_block():
    r = _iota((GROUP, GROUP), 0) >> 6
    c = _iota((GROUP, GROUP), 1) >> 6
    return jnp.where(r == c, 1.0, 0.0).astype(BF16)


def _prompt_mixer_kernel(x_ref, meta_ref, g1_ref, win_ref, wqk_ref, wconv_ref, rng_ref, wout_ref,
                         lgq_ref, lgv_ref, inv_ref,
                         x1_ref, convst_ref, retst_ref,
                         cos_ref, sin_ref, qdec_ref, kdec_ref, dmask_ref, ones_ref, smask_ref,
                         sdec_ref, tail_ref, state_ref, ubuf_ref, *, n_tiles):
    j = pl.program_id(0)
    n = pl.program_id(1)
    C = TILE

    def proj(xn, off):
        return _dot(xn, win_ref[:, off:off + 512])

    @pl.when((j == 0) & (n == 0))
    def _init():
        i_f = _iota((C, 1), 0).astype(F32)
        lgq = lgq_ref[...]
        qdec_ref[...] = jnp.exp((i_f + 1.0) * lgq)
        kdec_ref[...] = jnp.exp((C - 1.0 - i_f) * lgq)
        sdec_ref[...] = jnp.exp(C * lgv_ref[...])
        diff = (_iota((C, C), 0) - _iota((C, C), 1)).astype(F32)
        for h in range(RET_HEADS):
            dmask_ref[h] = jnp.where(
                diff >= 0, jnp.exp(jnp.maximum(diff, 0.0) * float(_LOG_GAMMA[h])), 0.0)
        ones_ref[...] = _ones_block()
        r = (_iota((GROUP, GROUP), 0) & (LANES - 1)) >> 5
        c = _iota((GROUP, GROUP), 1) >> 6
        smask = jnp.where(r == c, 1.0, 0.0)
        smask_ref[...] = smask

        xm = _rmsnorm(meta_ref[...], g1_ref[...]).astype(BF16)
        um = proj(xm, OFF_C) * proj(xm, OFF_H)
        km = _dot(xm, wqk_ref[:, RET_DIM:])
        vm = proj(xm, OFF_V).astype(BF16)
        m_f = _iota((N_META, 1), 0).astype(F32)
        ang = m_f * inv_ref[...]
        kmr = _rope_grouped(km, jnp.cos(ang), jnp.sin(ang))
        kmd = (kmr * jnp.exp((N_META - 1.0 - m_f) * lgq)).astype(BF16)
        for b in range(tail_ref.shape[0]):
            tail_ref[b] = um[N_META - 2:N_META, :]
            for g in range(N_GROUPS):
                sl = slice(g * GROUP, (g + 1) * GROUP)
                state_ref[b, g] = _dot_tn(kmd[:, sl], vm[:, sl]) * smask

    @pl.when(n == 0)
    def _rope_tables():
        pos = (N_META + j * C + _iota((C, LANES), 0)).astype(F32)
        ang = pos * inv_ref[...]
        cos_ref[...] = jnp.cos(ang)
        sin_ref[...] = jnp.sin(ang)

    x = x_ref[0]
    xn = _rmsnorm(x, g1_ref[...]).astype(BF16)

    zb = proj(xn, OFF_B)
    u = proj(xn, OFF_C) * proj(xn, OFF_H)
    ubuf_ref[6:8, :] = tail_ref[n]
    ubuf_ref[8:8 + C, :] = u
    wc = wconv_ref[...]
    conv = wc[0:1] * ubuf_ref[6:6 + C, :] + wc[1:2] * ubuf_ref[7:7 + C, :] + wc[2:3] * u
    conv_out = (zb * conv).astype(BF16)
    new_tail = u[C - 2:C, :]
    tail_ref[n] = new_tail

    cos = cos_ref[...]
    sin = sin_ref[...]
    qr = _rope_grouped(_dot(xn, wqk_ref[:, :RET_DIM]), cos, sin)
    kr = _rope_grouped(_dot(xn, wqk_ref[:, RET_DIM:]), cos, sin)
    vb = proj(xn, OFF_V).astype(BF16)
    qb = qr.astype(BF16)
    kb = kr.astype(BF16)
    qd = (qr * qdec_ref[...]).astype(BF16)
    kd = (kr * kdec_ref[...]).astype(BF16)
    head_of_qlane = (_iota((1, GROUP), 1) & (LANES - 1)) >> 5
    head_of_vlane = _iota((1, GROUP), 1) >> 6
    smask = smask_ref[...]
    o_parts = []
    new_states = []
    for g in range(N_GROUPS):
        sl = slice(g * GROUP, (g + 1) * GROUP)
        state = state_ref[n, g]
        acc = _dot(qd[:, sl], state.astype(BF16))
        for hh in range(HEADS_PER_GROUP):
            k_h = jnp.where(head_of_qlane == hh, kb[:, sl], jnp.zeros_like(kb[:, sl]))
            v_h = jnp.where(head_of_vlane == hh, vb[:, sl], jnp.zeros_like(vb[:, sl]))
            scores = _dot_nt(qb[:, sl], k_h) * dmask_ref[g * HEADS_PER_GROUP + hh]
            acc = acc + _dot(scores.astype(BF16), v_h)
        new_state = state * sdec_ref[:, sl] + _dot_tn(kd[:, sl], vb[:, sl]) * smask
        state_ref[n, g] = new_state
        new_states.append(new_state)
        o_parts.append(acc)
    o = jnp.concatenate(o_parts, axis=1)
    on = _group_stats_norm(o, ones_ref[...], rng_ref[...])
    ret_out = (_silu(proj(xn, OFF_G)) * on).astype(BF16)

    mix = jnp.concatenate([conv_out, ret_out], axis=1)
    x1_ref[0] = x + _dot(mix, wout_ref[...])

    @pl.when(j == n_tiles - 1)
    def _final_states():
        convst_ref[n] = new_tail
        for g in range(N_GROUPS):
            s = new_states[g]
            for hh in range(HEADS_PER_GROUP):
                h = g * HEADS_PER_GROUP + hh
                cols = slice(hh * HEAD_DIM, (hh + 1) * HEAD_DIM)
                retst_ref[n, h, 0:HALF_DIM, :] = s[hh * HALF_DIM:(hh + 1) * HALF_DIM, cols]
                retst_ref[n, h, HALF_DIM:HEAD_DIM, :] = (
                    s[LANES + hh * HALF_DIM:LANES + (hh + 1) * HALF_DIM, cols])


def _prompt_mixer(x_prompt, meta, g1, w_in_b, w_qk_b, w_conv, rng, w_out_b, lgq, lgv, inv_lane):
    n_seq, seq, _ = x_prompt.shape
    n_tiles = seq // TILE
    const2 = lambda j, n: (0, 0)
    full = lambda a: pl.BlockSpec(a.shape, const2, pipeline_mode=pl.Buffered(1))
    return pl.pallas_call(
        functools.partial(_prompt_mixer_kernel, n_tiles=n_tiles),
        grid=(n_tiles, n_seq),
        in_specs=[pl.BlockSpec((1, TILE, D_MODEL), lambda j, n: (n, j, 0)),
                  full(meta), full(g1), full(w_in_b), full(w_qk_b), full(w_conv), full(rng),
                  full(w_out_b), full(lgq), full(lgv), full(inv_lane)],
        out_specs=[pl.BlockSpec((1, TILE, D_MODEL), lambda j, n: (n, j, 0)),
                   pl.BlockSpec((n_seq, 2, CONV_DIM), lambda j, n: (0, 0, 0)),
                   pl.BlockSpec((n_seq, RET_HEADS, HEAD_DIM, HEAD_DIM), lambda j, n: (0, 0, 0, 0))],
        out_shape=[jax.ShapeDtypeStruct((n_seq, seq, D_MODEL), F32),
                   jax.ShapeDtypeStruct((n_seq, 2, CONV_DIM), F32),
                   jax.ShapeDtypeStruct((n_seq, RET_HEADS, HEAD_DIM, HEAD_DIM), F32)],
        scratch_shapes=[
            pltpu.VMEM((TILE, LANES), F32),
            pltpu.VMEM((TILE, LANES), F32),
            pltpu.VMEM((TILE, RET_DIM), F32),
            pltpu.VMEM((TILE, RET_DIM), F32),
            pltpu.VMEM((RET_HEADS, TILE, TILE), F32),
            pltpu.VMEM((GROUP, GROUP), BF16),
            pltpu.VMEM((GROUP, GROUP), F32),
            pltpu.VMEM((1, RET_DIM), F32),
            pltpu.VMEM((n_seq, 2, CONV_DIM), F32),
            pltpu.VMEM((n_seq, N_GROUPS, GROUP, GROUP), F32),
            pltpu.VMEM((TILE + 8, CONV_DIM), F32),
        ],
        compiler_params=pltpu.CompilerParams(
            dimension_semantics=("arbitrary", "arbitrary"), vmem_limit_bytes=VMEM_LIMIT),
        name="prompt_mixer",
    )(x_prompt, meta, g1, w_in_b, w_qk_b, w_conv, rng, w_out_b, lgq, lgv, inv_lane)


def _ffn_rows(x, g2, wg_ref, wu_ref, wd_ref, gf):
    xn = _rmsnorm(x, g2).astype(BF16)
    hidden = (_silu(_dot(xn, wg_ref[...])) * _dot(xn, wu_ref[...])).astype(BF16)
    return _rmsnorm(x + _dot(hidden, wd_ref[...]), gf)


def _ffn_kernel(x_ref, g2_ref, wg_ref, wu_ref, wd_ref, gf_ref, y_ref):
    for c in range(FFN_TILE // FFN_CHUNK):
        rows = slice(c * FFN_CHUNK, (c + 1) * FFN_CHUNK)
        y_ref[rows, :] = _ffn_rows(x_ref[rows, :], g2_ref[...], wg_ref, wu_ref, wd_ref, gf_ref[...])


def _ffn(x, g2, wg_b, wu_b, wd_b, gf):
    rows = x.shape[0]
    full = lambda a: pl.BlockSpec(a.shape, lambda i: (0, 0), pipeline_mode=pl.Buffered(1))
    tile = pl.BlockSpec((FFN_TILE, D_MODEL), lambda i: (i, 0))
    return pl.pallas_call(
        _ffn_kernel,
        grid=(rows // FFN_TILE,),
        in_specs=[tile, full(g2), full(wg_b), full(wu_b), full(wd_b), full(gf)],
        out_specs=tile,
        out_shape=jax.ShapeDtypeStruct((rows, D_MODEL), F32),
        compiler_params=pltpu.CompilerParams(
            dimension_semantics=("arbitrary",), vmem_limit_bytes=VMEM_LIMIT),
        name="ffn",
    )(x, g2, wg_b, wu_b, wd_b, gf)


def _gamma_pow(head, power):
    return float(np.exp(np.float32(power) * _LOG_GAMMA[head]))


def _sample_kernel(x_ref, st_ref, s_ref, g1_ref, win_ref, wconv_ref, gain_ref, wout_ref, inv_ref,
                   sdec_ref, g2_ref, wg_ref, wu_ref, wd_ref, gf_ref,
                   y_ref, convst_ref, snew_ref,
                   qd_ref, kd_ref, v_ref, o_ref, zg_ref, cvo_ref, *, n_seq, dec_seq):
    h = pl.program_id(0)
    L, B = dec_seq, n_seq

    @pl.when(h == 0)
    def _dense_front():
        xn = _rmsnorm(x_ref[...], g1_ref[...]).astype(BF16)

        def proj(off):
            return _dot(xn, win_ref[:, off:off + 512])

        zb = proj(OFF_B)
        u = proj(OFF_C) * proj(OFF_H)
        ext = [st_ref[:, :CONV_DIM], st_ref[:, CONV_DIM:]] + [u[i * B:(i + 1) * B] for i in range(L)]
        wc = wconv_ref[...]
        for i in range(L):
            conv = wc[0:1] * ext[i] + wc[1:2] * ext[i + 1] + wc[2:3] * ext[i + 2]
            cvo_ref[i * B:(i + 1) * B, :] = (zb[i * B:(i + 1) * B] * conv).astype(BF16)
        convst_ref[...] = jnp.concatenate([ext[L], ext[L + 1]], axis=1)
        zg_ref[...] = proj(OFF_G)

        q_t = proj(OFF_Q).T
        k_t = proj(OFF_K).T
        v_t = proj(OFF_V).T
        v_ref[...] = v_t
        inv = inv_ref[...]
        cos_sin = [(jnp.cos(float(PAST_LEN + i) * inv), jnp.sin(float(PAST_LEN + i) * inv))
                   for i in range(L)]

        def rope(t, r0, i):
            cos, sin = cos_sin[i]
            t1 = t[r0:r0 + HALF_DIM, i * B:(i + 1) * B]
            t2 = t[r0 + HALF_DIM:r0 + HEAD_DIM, i * B:(i + 1) * B]
            return jnp.concatenate([t1 * cos - t2 * sin, t2 * cos + t1 * sin], axis=0)

        for hd in range(RET_HEADS):
            r0 = hd * HEAD_DIM
            qr = [rope(q_t, r0, i) for i in range(L)]
            kr = [rope(k_t, r0, i) for i in range(L)]
            for i in range(L):
                lanes = slice(i * B, (i + 1) * B)
                qd_ref[r0:r0 + HEAD_DIM, lanes] = qr[i] * _gamma_pow(hd, i + 1)
                kd_ref[r0:r0 + HEAD_DIM, lanes] = kr[i] * _gamma_pow(hd, L - 1 - i)
                intra = jnp.zeros((HEAD_DIM, B), F32)
                for j in range(i + 1):
                    score = jnp.sum(qr[i] * kr[j], axis=0, keepdims=True) * _gamma_pow(hd, i - j)
                    intra = intra + score * v_t[r0:r0 + HEAD_DIM, j * B:(j + 1) * B]
                o_ref[r0:r0 + HEAD_DIM, lanes] = intra

    base = pl.multiple_of(h * HEAD_DIM, HEAD_DIM)
    qd_h = qd_ref[pl.ds(base, HEAD_DIM), :]
    kd_h = kd_ref[pl.ds(base, HEAD_DIM), :]
    state_decay = sdec_ref[h]
    for e0 in range(0, HEAD_DIM, E_CHUNK):
        rows = pl.ds(pl.multiple_of(base + e0, E_CHUNK), E_CHUNK)
        v_blk = [v_ref[rows, j * B:(j + 1) * B] for j in range(L)]
        cross = [jnp.zeros((E_CHUNK, B), F32) for _ in range(L)]
        for d in range(HEAD_DIM):
            s_de = s_ref[0, d, e0:e0 + E_CHUNK, :]
            new = s_de * state_decay
            for t in range(L):
                lanes = slice(t * B, (t + 1) * B)
                cross[t] = cross[t] + qd_h[d:d + 1, lanes] * s_de
                new = new + kd_h[d:d + 1, lanes] * v_blk[t]
            snew_ref[0, d, e0:e0 + E_CHUNK, :] = new
        for i in range(L):
            o_ref[rows, i * B:(i + 1) * B] = o_ref[rows, i * B:(i + 1) * B] + cross[i]

    @pl.when(h == RET_HEADS - 1)
    def _dense_back():
        for hd in range(RET_HEADS):
            r0 = hd * HEAD_DIM
            o_h = o_ref[r0:r0 + HEAD_DIM, :]
            mu = jnp.sum(o_h, axis=0, keepdims=True) * (1.0 / HEAD_DIM)
            dlt = o_h - mu
            var = jnp.sum(dlt * dlt, axis=0, keepdims=True) * (1.0 / HEAD_DIM)
            o_ref[r0:r0 + HEAD_DIM, :] = dlt * lax.rsqrt(var + GN_EPS) * gain_ref[r0:r0 + HEAD_DIM, :]
        ret_out = (_silu(zg_ref[...]) * o_ref[...].T).astype(BF16)
        x1 = (x_ref[...] + _dot(cvo_ref[...], wout_ref[:CONV_DIM, :])
              + _dot(ret_out, wout_ref[CONV_DIM:, :]))
        for c in range(L * B // FFN_CHUNK):
            rows = slice(c * FFN_CHUNK, (c + 1) * FFN_CHUNK)
            y_ref[rows, :] = _ffn_rows(x1[rows], g2_ref[...], wg_ref, wu_ref, wd_ref, gf_ref[...])


def _sample_step(xs, st, s4, g1, w_in_b, w_conv, gain, w_out_b, inv_col, sdec, g2, wg_b, wu_b, wd_b, gf,
                 dec_seq):
    rows = xs.shape[0]
    n_seq = rows // dec_seq
    full = lambda a: pl.BlockSpec(a.shape, lambda h: (0,) * a.ndim, pipeline_mode=pl.Buffered(1))
    state_spec = pl.BlockSpec((1, HEAD_DIM, HEAD_DIM, n_seq), lambda h: (h, 0, 0, 0))
    consts = (g1, w_in_b, w_conv, gain, w_out_b, inv_col, sdec, g2, wg_b, wu_b, wd_b, gf)
    return pl.pallas_call(
        functools.partial(_sample_kernel, n_seq=n_seq, dec_seq=dec_seq),
        grid=(RET_HEADS,),
        in_specs=[full(xs), full(st), state_spec] + [full(a) for a in consts],
        out_specs=[pl.BlockSpec((rows, D_MODEL), lambda h: (0, 0)),
                   pl.BlockSpec(st.shape, lambda h: (0, 0)),
                   state_spec],
        out_shape=[jax.ShapeDtypeStruct((rows, D_MODEL), F32),
                   jax.ShapeDtypeStruct(st.shape, F32),
                   jax.ShapeDtypeStruct(s4.shape, F32)],
        scratch_shapes=[
            pltpu.VMEM((RET_DIM, rows), F32),
            pltpu.VMEM((RET_DIM, rows), F32),
            pltpu.VMEM((RET_DIM, rows), F32),
            pltpu.VMEM((RET_DIM, rows), F32),
            pltpu.VMEM((rows, RET_DIM), F32),
            pltpu.VMEM((rows, CONV_DIM), BF16),
        ],
        compiler_params=pltpu.CompilerParams(
            dimension_semantics=("arbitrary",), vmem_limit_bytes=VMEM_LIMIT),
        name="sample_step",
    )(xs, st, s4, *consts)


def _grouped_perm():
    idx = np.arange(RET_DIM).reshape(N_GROUPS, HEADS_PER_GROUP, 2, HALF_DIM)
    return idx.transpose(0, 2, 1, 3).reshape(-1)


def kernel(x_prompt, x_sample, state_conv, state_ret, meta_tokens, norm1_g, w_in, w_conv, ret_norm_g,
           w_out, norm2_g, w_gate, w_up, w_down, final_norm_g):
    n_p, seq, _ = x_prompt.shape
    n_s, dec_seq, _ = x_sample.shape
    rows_s = n_s * dec_seq
    assert norm1_g.shape[0] == 1 and seq % TILE == 0 and (n_p * seq) % FFN_TILE == 0
    assert n_s % LANES == 0 and FFN_CHUNK % n_s == 0 and rows_s % FFN_CHUNK == 0

    col_scale = np.ones((1, w_in.shape[-1]), np.float32)
    col_scale[:, OFF_K:OFF_V] = HEAD_DIM ** -0.5
    w_in_b = (w_in[0] * col_scale).astype(BF16)
    perm = _grouped_perm()
    w_qk_b = jnp.concatenate([w_in_b[:, OFF_Q:OFF_K][:, perm], w_in_b[:, OFF_K:OFF_V][:, perm]], axis=1)
    w_out_b = w_out[0].astype(BF16)
    wg_b, wu_b, wd_b = w_gate[0].astype(BF16), w_up[0].astype(BF16), w_down[0].astype(BF16)
    g1, g2, gf = norm1_g[0][None], norm2_g[0][None], final_norm_g[None]
    rng = ret_norm_g[0][None]

    head_of_qlane = (np.arange(RET_DIM) // GROUP) * HEADS_PER_GROUP + (np.arange(RET_DIM) % LANES) // HALF_DIM
    lgq = jnp.asarray(_LOG_GAMMA[head_of_qlane][None])
    lgv = jnp.asarray(_LOG_GAMMA[np.arange(RET_DIM) // HEAD_DIM][None])
    inv = ROPE_BASE ** (-jnp.arange(HALF_DIM, dtype=F32) / HALF_DIM)
    inv_lane = jnp.tile(inv, LANES // HALF_DIM)[None]

    x1_p, conv_p, ret_p = _prompt_mixer(x_prompt, meta_tokens, g1, w_in_b, w_qk_b, w_conv[0], rng,
                                        w_out_b, lgq, lgv, inv_lane)
    y_prompt = _ffn(x1_p.reshape(n_p * seq, D_MODEL), g2, wg_b, wu_b, wd_b, gf).reshape(n_p, seq, D_MODEL)

    s4 = jnp.transpose(state_ret[0], (1, 2, 3, 0))
    st = state_conv[0].reshape(n_s, 2 * CONV_DIM)
    gain_col = jnp.broadcast_to(ret_norm_g[0][:, None], (RET_DIM, rows_s))
    inv_col = jnp.broadcast_to(inv[:, None], (HALF_DIM, n_s))
    sdec = jnp.asarray(np.broadcast_to(
        np.exp(np.float32(dec_seq) * _LOG_GAMMA)[:, None, None], (RET_HEADS, E_CHUNK, n_s)))
    xs = jnp.transpose(x_sample, (1, 0, 2)).reshape(rows_s, D_MODEL)
    y_s, conv_s, s4_new = _sample_step(xs, st, s4, g1, w_in_b, w_conv[0], gain_col, w_out_b, inv_col, sdec,
                                       g2, wg_b, wu_b, wd_b, gf, dec_seq)
    y_sample = jnp.transpose(y_s.reshape(dec_seq, n_s, D_MODEL), (1, 0, 2))
    ret_s = jnp.transpose(s4_new, (3, 0, 1, 2))

    return (y_prompt, y_sample, conv_p[None], ret_p[None], conv_s.reshape(1, n_s, 2, CONV_DIM), ret_s[None])
```

```python
import functools

import numpy as np
import jax
import jax.numpy as jnp
from jax import lax
from jax.experimental import pallas as pl
from jax.experimental.pallas import tpu as pltpu

D_MODEL = 1024
N_META = 16
CONV_DIM = 512
RET_HEADS = 8
HEAD_DIM = 64
HALF_DIM = HEAD_DIM // 2
RET_DIM = RET_HEADS * HEAD_DIM
D_FF = 2816
PAST_LEN = 16384
ROPE_BASE = 10000.0
EPS = 1e-6
GN_EPS = 1e-5

OFF_B, OFF_C, OFF_H, OFF_Q, OFF_K, OFF_V, OFF_G = (i * 512 for i in range(7))

LANES = 128
GROUP = 256
HEADS_PER_GROUP = GROUP // HEAD_DIM
N_GROUPS = RET_DIM // GROUP
TILE = 256
FFN_TILE = 512
FFN_CHUNK = 256
E_CHUNK = 32
CAST_ROWS = 128
VMEM_LIMIT = 56 * 1024 * 1024

F32 = jnp.float32
BF16 = jnp.bfloat16

_LOG_GAMMA = np.log1p(-(2.0 ** (-5.0 - np.arange(RET_HEADS)))).astype(np.float32)


def _dot(a, b):
    return jnp.dot(a, b, preferred_element_type=F32)


def _dot_nt(a, b):
    return lax.dot_general(a, b, (((1,), (1,)), ((), ())), preferred_element_type=F32)


def _dot_tn(a, b):
    return lax.dot_general(a, b, (((0,), (0,)), ((), ())), preferred_element_type=F32)


def _rmsnorm(x, g):
    ms = jnp.mean(x * x, axis=-1, keepdims=True)
    return x * lax.rsqrt(ms + EPS) * g


def _silu(x):
    return x * jax.nn.sigmoid(x)


def _iota(shape, dim):
    return lax.broadcasted_iota(jnp.int32, shape, dim)


def _first_half_lanes():
    return (_iota((1, LANES), 1) & (HEAD_DIM - 1)) < HALF_DIM


def _rope(t, cos, signed_sin):
    first_half = _first_half_lanes()
    out = []
    for b in range(t.shape[1] // LANES):
        blk = t[:, b * LANES:(b + 1) * LANES]
        partner = jnp.where(first_half, pltpu.roll(blk, LANES - HALF_DIM, axis=1),
                            pltpu.roll(blk, HALF_DIM, axis=1))
        out.append(blk * cos + partner * signed_sin)
    return jnp.concatenate(out, axis=1)


def _segment_sum(t, ones_blk):
    return jnp.concatenate(
        [_dot(t[:, g * GROUP:(g + 1) * GROUP].astype(BF16), ones_blk) for g in range(N_GROUPS)], axis=1)


def _group_stats_norm(o, ones_blk, gain):
    mu = _segment_sum(o, ones_blk) * (1.0 / HEAD_DIM)
    d = o - mu
    var = _segment_sum(d * d, ones_blk) * (1.0 / HEAD_DIM)
    return d * lax.rsqrt(var + GN_EPS) * gain


def _ones_block():
    r = _iota((GROUP, GROUP), 0) >> 6
    c = _iota((GROUP, GROUP), 1) >> 6
    return jnp.where(r == c, 1.0, 0.0).astype(BF16)


def _prompt_mixer_kernel(x_ref, meta_ref, g1_ref, win32_ref, wconv_ref, rng_ref, wout32_ref,
                         lg_ref, inv_ref,
                         x1_ref, convst_ref, retst_ref,
                         win_ref, wout_ref, cos_ref, sin_ref, qdec_ref, kdec_ref, dmask_ref, ones_ref,
                         smask_ref, sdec_ref, tail_ref, state_ref, ubuf_ref, *, n_tiles):
    j = pl.program_id(0)
    n = pl.program_id(1)
    C = TILE
    k_scale = HEAD_DIM ** -0.5

    def proj(xn, off):
        return _dot(xn, win_ref[:, off:off + 512])

    @pl.when((j == 0) & (n == 0))
    def _init():
        for r in range(0, D_MODEL, CAST_ROWS):
            win_ref[r:r + CAST_ROWS, :] = win32_ref[r:r + CAST_ROWS, :].astype(BF16)
            wout_ref[r:r + CAST_ROWS, :] = wout32_ref[r:r + CAST_ROWS, :].astype(BF16)
        i_f = _iota((C, 1), 0).astype(F32)
        lg = lg_ref[...]
        qdec_ref[...] = jnp.exp((i_f + 1.0) * lg)
        kdec_ref[...] = jnp.exp((C - 1.0 - i_f) * lg) * k_scale
        sdec_ref[...] = jnp.exp(C * lg)
        diff = (_iota((C, C), 0) - _iota((C, C), 1)).astype(F32)
        for h in range(RET_HEADS):
            dmask_ref[h] = jnp.where(
                diff >= 0, jnp.exp(jnp.maximum(diff, 0.0) * float(_LOG_GAMMA[h])) * k_scale, 0.0)
        ones = _ones_block()
        ones_ref[...] = ones
        smask = ones.astype(F32)
        smask_ref[...] = smask

        xm = _rmsnorm(meta_ref[...], g1_ref[...]).astype(BF16)
        um = proj(xm, OFF_C) * proj(xm, OFF_H)
        vm = proj(xm, OFF_V).astype(BF16)
        m_f = _iota((N_META, 1), 0).astype(F32)
        ang = m_f * inv_ref[...]
        sin = jnp.sin(ang)
        kmr = _rope(proj(xm, OFF_K), jnp.cos(ang), jnp.where(_first_half_lanes(), -sin, sin))
        kmd = (kmr * (jnp.exp((N_META - 1.0 - m_f) * lg) * k_scale)).astype(BF16)
        for b in range(tail_ref.shape[0]):
            tail_ref[b] = um[N_META - 2:N_META, :]
            for g in range(N_GROUPS):
                sl = slice(g * GROUP, (g + 1) * GROUP)
                state_ref[b, g] = _dot_tn(kmd[:, sl], vm[:, sl]) * smask

    @pl.when(n == 0)
    def _rope_tables():
        pos = (N_META + j * C + _iota((C, LANES), 0)).astype(F32)
        ang = pos * inv_ref[...]
        sin = jnp.sin(ang)
        cos_ref[...] = jnp.cos(ang)
        sin_ref[...] = jnp.where(_first_half_lanes(), -sin, sin)

    x = x_ref[0]
    xn = _rmsnorm(x, g1_ref[...]).astype(BF16)

    zb = proj(xn, OFF_B)
    u = proj(xn, OFF_C) * proj(xn, OFF_H)
    ubuf_ref[6:8, :] = tail_ref[n]
    ubuf_ref[8:8 + C, :] = u
    wc = wconv_ref[...]
    conv = wc[0:1] * ubuf_ref[6:6 + C, :] + wc[1:2] * ubuf_ref[7:7 + C, :] + wc[2:3] * u
    conv_out = (zb * conv).astype(BF16)
    new_tail = u[C - 2:C, :]
    tail_ref[n] = new_tail

    cos = cos_ref[...]
    sin = sin_ref[...]
    qr = _rope(proj(xn, OFF_Q), cos, sin)
    kr = _rope(proj(xn, OFF_K), cos, sin)
    vb = proj(xn, OFF_V).astype(BF16)
    qb = qr.astype(BF16)
    kb = kr.astype(BF16)
    qd = (qr * qdec_ref[...]).astype(BF16)
    kd = (kr * kdec_ref[...]).astype(BF16)
    head_of_lane = _iota((1, GROUP), 1) >> 6
    smask = smask_ref[...]
    o_parts = []
    new_states = []
    for g in range(N_GROUPS):
        sl = slice(g * GROUP, (g + 1) * GROUP)
        state = state_ref[n, g]
        acc = _dot(qd[:, sl], state.astype(BF16))
        for hh in range(HEADS_PER_GROUP):
            k_h = jnp.where(head_of_lane == hh, kb[:, sl], jnp.zeros_like(kb[:, sl]))
            v_h = jnp.where(head_of_lane == hh, vb[:, sl], jnp.zeros_like(vb[:, sl]))
            scores = _dot_nt(qb[:, sl], k_h) * dmask_ref[g * HEADS_PER_GROUP + hh]
            acc = acc + _dot(scores.astype(BF16), v_h)
        new_state = state * sdec_ref[:, sl] + _dot_tn(kd[:, sl], vb[:, sl]) * smask
        state_ref[n, g] = new_state
        new_states.append(new_state)
        o_parts.append(acc)
    o = jnp.concatenate(o_parts, axis=1)
    on = _group_stats_norm(o, ones_ref[...], rng_ref[...])
    ret_out = (_silu(proj(xn, OFF_G)) * on).astype(BF16)

    mix = jnp.concatenate([conv_out, ret_out], axis=1)
    x1_ref[0] = x + _dot(mix, wout_ref[...])

    @pl.when(j == n_tiles - 1)
    def _final_states():
        convst_ref[n] = new_tail
        for g in range(N_GROUPS):
            s = new_states[g]
            for hh in range(HEADS_PER_GROUP):
                h = g * HEADS_PER_GROUP + hh
                blk = slice(hh * HEAD_DIM, (hh + 1) * HEAD_DIM)
                retst_ref[n, h] = s[blk, blk]


def _prompt_mixer(x_prompt, meta, g1, w_in, w_conv, rng, w_out, lg_lane, inv_lane):
    n_seq, seq, _ = x_prompt.shape
    n_tiles = seq // TILE
    const2 = lambda j, n: (0, 0)
    full = lambda a: pl.BlockSpec(a.shape, const2, pipeline_mode=pl.Buffered(1))
    return pl.pallas_call(
        functools.partial(_prompt_mixer_kernel, n_tiles=n_tiles),
        grid=(n_tiles, n_seq),
        in_specs=[pl.BlockSpec((1, TILE, D_MODEL), lambda j, n: (n, j, 0)),
                  full(meta), full(g1), full(w_in), full(w_conv), full(rng), full(w_out),
                  full(lg_lane), full(inv_lane)],
        out_specs=[pl.BlockSpec((1, TILE, D_MODEL), lambda j, n: (n, j, 0)),
                   pl.BlockSpec((n_seq, 2, CONV_DIM), lambda j, n: (0, 0, 0)),
                   pl.BlockSpec((n_seq, RET_HEADS, HEAD_DIM, HEAD_DIM), lambda j, n: (0, 0, 0, 0))],
        out_shape=[jax.ShapeDtypeStruct((n_seq, seq, D_MODEL), F32),
                   jax.ShapeDtypeStruct((n_seq, 2, CONV_DIM), F32),
                   jax.ShapeDtypeStruct((n_seq, RET_HEADS, HEAD_DIM, HEAD_DIM), F32)],
        scratch_shapes=[
            pltpu.VMEM(w_in.shape, BF16),
            pltpu.VMEM(w_out.shape, BF16),
            pltpu.VMEM((TILE, LANES), F32),
            pltpu.VMEM((TILE, LANES), F32),
            pltpu.VMEM((TILE, RET_DIM), F32),
            pltpu.VMEM((TILE, RET_DIM), F32),
            pltpu.VMEM((RET_HEADS, TILE, TILE), F32),
            pltpu.VMEM((GROUP, GROUP), BF16),
            pltpu.VMEM((GROUP, GROUP), F32),
            pltpu.VMEM((1, RET_DIM), F32),
            pltpu.VMEM((n_seq, 2, CONV_DIM), F32),
            pltpu.VMEM((n_seq, N_GROUPS, GROUP, GROUP), F32),
            pltpu.VMEM((TILE + 8, CONV_DIM), F32),
        ],
        compiler_params=pltpu.CompilerParams(
            dimension_semantics=("arbitrary", "arbitrary"), vmem_limit_bytes=VMEM_LIMIT),
        name="prompt_mixer",
    )(x_prompt, meta, g1, w_in, w_conv, rng, w_out, lg_lane, inv_lane)


def _ffn_rows(x, g2, wg_ref, wu_ref, wd_ref, gf):
    xn = _rmsnorm(x, g2).astype(BF16)
    hidden = (_silu(_dot(xn, wg_ref[...].astype(BF16))) * _dot(xn, wu_ref[...].astype(BF16))).astype(BF16)
    return _rmsnorm(x + _dot(hidden, wd_ref[...].astype(BF16)), gf)


def _ffn_kernel(xp_ref, xs_ref, g2_ref, wg_ref, wu_ref, wd_ref, gf_ref, yp_ref, ys_ref, *, prompt_steps):
    i = pl.program_id(0)

    def run(x_ref, y_ref):
        for c in range(FFN_TILE // FFN_CHUNK):
            rows = slice(c * FFN_CHUNK, (c + 1) * FFN_CHUNK)
            y_ref[rows, :] = _ffn_rows(x_ref[rows, :], g2_ref[...], wg_ref, wu_ref, wd_ref, gf_ref[...])

    @pl.when(i < prompt_steps)
    def _prompt_rows():
        run(xp_ref, yp_ref)

    @pl.when(i == prompt_steps)
    def _sample_rows():
        run(xs_ref, ys_ref)


def _ffn(x_p, x_s, g2, w_gate, w_up, w_down, gf):
    prompt_steps = x_p.shape[0] // FFN_TILE
    full = lambda a: pl.BlockSpec(a.shape, lambda i: (0, 0), pipeline_mode=pl.Buffered(1))
    prompt_tile = pl.BlockSpec((FFN_TILE, D_MODEL), lambda i: (jnp.minimum(i, prompt_steps - 1), 0))
    sample_tile = pl.BlockSpec((FFN_TILE, D_MODEL), lambda i: (0, 0))
    return pl.pallas_call(
        functools.partial(_ffn_kernel, prompt_steps=prompt_steps),
        grid=(prompt_steps + 1,),
        in_specs=[prompt_tile, sample_tile, full(g2), full(w_gate), full(w_up), full(w_down), full(gf)],
        out_specs=[prompt_tile, sample_tile],
        out_shape=[jax.ShapeDtypeStruct(x_p.shape, F32), jax.ShapeDtypeStruct(x_s.shape, F32)],
        compiler_params=pltpu.CompilerParams(
            dimension_semantics=("arbitrary",), vmem_limit_bytes=VMEM_LIMIT),
        name="ffn",
    )(x_p, x_s, g2, w_gate, w_up, w_down, gf)


def _gamma_pow(head, power):
    return float(np.exp(np.float32(power) * _LOG_GAMMA[head]))


def _sample_kernel(x_ref, st_ref, s_ref, g1_ref, win_ref, wconv_ref, gain_ref, wout_ref, inv_ref,
                   sdec_ref,
                   x1_ref, convst_ref, snew_ref,
                   qd_ref, kd_ref, v_ref, o_ref, zg_ref, cvo_ref, *, n_seq, dec_seq):
    h = pl.program_id(0)
    L, B = dec_seq, n_seq
    k_scale = HEAD_DIM ** -0.5

    @pl.when(h == 0)
    def _dense_front():
        xn = _rmsnorm(x_ref[...], g1_ref[...]).astype(BF16)

        def proj(off):
            return _dot(xn, win_ref[:, off:off + 512].astype(BF16))

        zb = proj(OFF_B)
        u = proj(OFF_C) * proj(OFF_H)
        ext = [st_ref[:, :CONV_DIM], st_ref[:, CONV_DIM:]] + [u[i * B:(i + 1) * B] for i in range(L)]
        wc = wconv_ref[...]
        for i in range(L):
            conv = wc[0:1] * ext[i] + wc[1:2] * ext[i + 1] + wc[2:3] * ext[i + 2]
            cvo_ref[i * B:(i + 1) * B, :] = (zb[i * B:(i + 1) * B] * conv).astype(BF16)
        convst_ref[...] = jnp.concatenate([ext[L], ext[L + 1]], axis=1)
        zg_ref[...] = proj(OFF_G)

        q_t = proj(OFF_Q).T
        k_t = proj(OFF_K).T
        v_t = proj(OFF_V).T
        v_ref[...] = v_t
        inv = inv_ref[...]
        cos_sin = [(jnp.cos(float(PAST_LEN + i) * inv), jnp.sin(float(PAST_LEN + i) * inv))
                   for i in range(L)]

        def rope(t, r0, i):
            cos, sin = cos_sin[i]
            t1 = t[r0:r0 + HALF_DIM, i * B:(i + 1) * B]
            t2 = t[r0 + HALF_DIM:r0 + HEAD_DIM, i * B:(i + 1) * B]
            return jnp.concatenate([t1 * cos - t2 * sin, t2 * cos + t1 * sin], axis=0)

        for hd in range(RET_HEADS):
            r0 = hd * HEAD_DIM
            qr = [rope(q_t, r0, i) for i in range(L)]
            kr = [rope(k_t, r0, i) for i in range(L)]
            for i in range(L):
                lanes = slice(i * B, (i + 1) * B)
                qd_ref[r0:r0 + HEAD_DIM, lanes] = qr[i] * _gamma_pow(hd, i + 1)
                kd_ref[r0:r0 + HEAD_DIM, lanes] = kr[i] * (_gamma_pow(hd, L - 1 - i) * k_scale)
                intra = jnp.zeros((HEAD_DIM, B), F32)
                for j in range(i + 1):
                    score = jnp.sum(qr[i] * kr[j], axis=0, keepdims=True) * (_gamma_pow(hd, i - j) * k_scale)
                    intra = intra + score * v_t[r0:r0 + HEAD_DIM, j * B:(j + 1) * B]
                o_ref[r0:r0 + HEAD_DIM, lanes] = intra

    base = pl.multiple_of(h * HEAD_DIM, HEAD_DIM)
    qd_h = qd_ref[pl.ds(base, HEAD_DIM), :]
    kd_h = kd_ref[pl.ds(base, HEAD_DIM), :]
    state_decay = sdec_ref[h]
    for e0 in range(0, HEAD_DIM, E_CHUNK):
        rows = pl.ds(pl.multiple_of(base + e0, E_CHUNK), E_CHUNK)
        v_blk = [v_ref[rows, j * B:(j + 1) * B] for j in range(L)]
        cross = [jnp.zeros((E_CHUNK, B), F32) for _ in range(L)]
        for d in range(HEAD_DIM):
            s_de = s_ref[0, d, e0:e0 + E_CHUNK, :]
            new = s_de * state_decay
            for t in range(L):
                lanes = slice(t * B, (t + 1) * B)
                cross[t] = cross[t] + qd_h[d:d + 1, lanes] * s_de
                new = new + kd_h[d:d + 1, lanes] * v_blk[t]
            snew_ref[0, d, e0:e0 + E_CHUNK, :] = new
        for i in range(L):
            o_ref[rows, i * B:(i + 1) * B] = o_ref[rows, i * B:(i + 1) * B] + cross[i]

    @pl.when(h == RET_HEADS - 1)
    def _dense_back():
        for hd in range(RET_HEADS):
            r0 = hd * HEAD_DIM
            o_h = o_ref[r0:r0 + HEAD_DIM, :]
            mu = jnp.sum(o_h, axis=0, keepdims=True) * (1.0 / HEAD_DIM)
            dlt = o_h - mu
            var = jnp.sum(dlt * dlt, axis=0, keepdims=True) * (1.0 / HEAD_DIM)
            o_ref[r0:r0 + HEAD_DIM, :] = dlt * lax.rsqrt(var + GN_EPS) * gain_ref[r0:r0 + HEAD_DIM, :]
        ret_out = (_silu(zg_ref[...]) * o_ref[...].T).astype(BF16)
        x1_ref[...] = (x_ref[...] + _dot(cvo_ref[...], wout_ref[:CONV_DIM, :].astype(BF16))
                       + _dot(ret_out, wout_ref[CONV_DIM:, :].astype(BF16)))


def _sample_mixer(xs, st, s4, g1, w_in, w_conv, gain, w_out, inv_col, sdec, dec_seq):
    rows = xs.shape[0]
    n_seq = rows // dec_seq
    full = lambda a: pl.BlockSpec(a.shape, lambda h: (0,) * a.ndim, pipeline_mode=pl.Buffered(1))
    state_spec = pl.BlockSpec((1, HEAD_DIM, HEAD_DIM, n_seq), lambda h: (h, 0, 0, 0))
    consts = (g1, w_in, w_conv, gain, w_out, inv_col, sdec)
    return pl.pallas_call(
        functools.partial(_sample_kernel, n_seq=n_seq, dec_seq=dec_seq),
        grid=(RET_HEADS,),
        in_specs=[full(xs), full(st), state_spec] + [full(a) for a in consts],
        out_specs=[pl.BlockSpec((rows, D_MODEL), lambda h: (0, 0)),
                   pl.BlockSpec(st.shape, lambda h: (0, 0)),
                   state_spec],
        out_shape=[jax.ShapeDtypeStruct((rows, D_MODEL), F32),
                   jax.ShapeDtypeStruct(st.shape, F32),
                   jax.ShapeDtypeStruct(s4.shape, F32)],
        scratch_shapes=[
            pltpu.VMEM((RET_DIM, rows), F32),
            pltpu.VMEM((RET_DIM, rows), F32),
            pltpu.VMEM((RET_DIM, rows), F32),
            pltpu.VMEM((RET_DIM, rows), F32),
            pltpu.VMEM((rows, RET_DIM), F32),
            pltpu.VMEM((rows, CONV_DIM), BF16),
        ],
        compiler_params=pltpu.CompilerParams(
            dimension_semantics=("arbitrary",), vmem_limit_bytes=VMEM_LIMIT),
        name="sample_mixer",
    )(xs, st, s4, *consts)


def kernel(x_prompt, x_sample, state_conv, state_ret, meta_tokens, norm1_g, w_in, w_conv, ret_norm_g,
           w_out, norm2_g, w_gate, w_up, w_down, final_norm_g):
    n_p, seq, _ = x_prompt.shape
    n_s, dec_seq, _ = x_sample.shape
    rows_s = n_s * dec_seq
    assert norm1_g.shape[0] == 1 and seq % TILE == 0 and (n_p * seq) % FFN_TILE == 0
    assert n_s % LANES == 0 and rows_s == FFN_TILE

    g1, g2, gf = norm1_g[0][None], norm2_g[0][None], final_norm_g[None]
    rng = ret_norm_g[0][None]
    lg_lane = jnp.asarray(_LOG_GAMMA[np.arange(RET_DIM) // HEAD_DIM][None])
    inv = ROPE_BASE ** (-jnp.arange(HALF_DIM, dtype=F32) / HALF_DIM)
    inv_lane = jnp.tile(inv, LANES // HALF_DIM)[None]

    x1_p, conv_p, ret_p = _prompt_mixer(x_prompt, meta_tokens, g1, w_in[0], w_conv[0], rng, w_out[0],
                                        lg_lane, inv_lane)

    s4 = jnp.transpose(state_ret[0], (1, 2, 3, 0))
    st = state_conv[0].reshape(n_s, 2 * CONV_DIM)
    gain_col = jnp.broadcast_to(ret_norm_g[0][:, None], (RET_DIM, rows_s))
    inv_col = jnp.broadcast_to(inv[:, None], (HALF_DIM, n_s))
    sdec = jnp.asarray(np.broadcast_to(
        np.exp(np.float32(dec_seq) * _LOG_GAMMA)[:, None, None], (RET_HEADS, E_CHUNK, n_s)))
    xs = jnp.transpose(x_sample, (1, 0, 2)).reshape(rows_s, D_MODEL)
    x1_s, conv_s, s4_new = _sample_mixer(xs, st, s4, g1, w_in[0], w_conv[0], gain_col, w_out[0], inv_col,
                                         sdec, dec_seq)

    y_p, y_s = _ffn(x1_p.reshape(n_p * seq, D_MODEL), x1_s, g2, w_gate[0], w_up[0], w_down[0], gf)
    y_prompt = y_p.reshape(n_p, seq, D_MODEL)
    y_sample = jnp.transpose(y_s.reshape(dec_seq, n_s, D_MODEL), (1, 0, 2))
    ret_s = jnp.transpose(s4_new, (3, 0, 1, 2))

    return (y_prompt, y_sample, conv_p[None], ret_p[None], conv_s.reshape(1, n_s, 2, CONV_DIM), ret_s[None])
```

```python
import functools

import numpy as np
import jax
import jax.numpy as jnp
from jax import lax
from jax.experimental import pallas as pl
from jax.experimental.pallas import tpu as pltpu

D_MODEL = 1024
N_META = 16
CONV_DIM = 512
RET_HEADS = 8
HEAD_DIM = 64
HALF_DIM = HEAD_DIM // 2
RET_DIM = RET_HEADS * HEAD_DIM
D_FF = 2816
PAST_LEN = 16384
ROPE_BASE = 10000.0
EPS = 1e-6
GN_EPS = 1e-5

OFF_B, OFF_C, OFF_H, OFF_Q, OFF_K, OFF_V, OFF_G = (i * 512 for i in range(7))

LANES = 128
GROUP = 256
HEADS_PER_GROUP = GROUP // HEAD_DIM
N_GROUPS = RET_DIM // GROUP
TILE = 256
SEQS_PER_STEP = 2
FFN_TILE = 512
FFN_CHUNK = 256
E_CHUNK = 32
CAST_ROWS = 128
VMEM_LIMIT = 56 * 1024 * 1024

F32 = jnp.float32
BF16 = jnp.bfloat16

_LOG_GAMMA = np.log1p(-(2.0 ** (-5.0 - np.arange(RET_HEADS)))).astype(np.float32)


def _dot(a, b):
    return jnp.dot(a, b, preferred_element_type=F32)


def _dot_nt(a, b):
    return lax.dot_general(a, b, (((1,), (1,)), ((), ())), preferred_element_type=F32)


def _dot_tn(a, b):
    return lax.dot_general(a, b, (((0,), (0,)), ((), ())), preferred_element_type=F32)


def _rmsnorm(x, g):
    ms = jnp.mean(x * x, axis=-1, keepdims=True)
    return x * lax.rsqrt(ms + EPS) * g


def _silu(x):
    return x * jax.nn.sigmoid(x)


def _iota(shape, dim):
    return lax.broadcasted_iota(jnp.int32, shape, dim)


def _first_half_lanes():
    return (_iota((1, LANES), 1) & (HEAD_DIM - 1)) < HALF_DIM


def _rope(t, cos, signed_sin):
    first_half = _first_half_lanes()
    out = []
    for b in range(t.shape[1] // LANES):
        blk = t[:, b * LANES:(b + 1) * LANES]
        partner = jnp.where(first_half, pltpu.roll(blk, LANES - HALF_DIM, axis=1),
                            pltpu.roll(blk, HALF_DIM, axis=1))
        out.append(blk * cos + partner * signed_sin)
    return jnp.concatenate(out, axis=1)


def _segment_sum(t, ones_blk):
    return jnp.concatenate(
        [_dot(t[:, g * GROUP:(g + 1) * GROUP].astype(BF16), ones_blk) for g in range(N_GROUPS)], axis=1)


def _group_stats_norm(o, ones_blk, gain):
    mu = _segment_sum(o, ones_blk) * (1.0 / HEAD_DIM)
    d = o - mu
    var = _segment_sum(d * d, ones_blk) * (1.0 / HEAD_DIM)
    return d * lax.rsqrt(var + GN_EPS) * gain


def _ones_block():
    r = _iota((GROUP, GROUP), 0) >> 6
    c = _iota((GROUP, GROUP), 1) >> 6
    return jnp.where(r == c, 1.0, 0.0).astype(BF16)


def _prompt_mixer_kernel(x_ref, meta_ref, g1_ref, win32_ref, wconv_ref, rng_ref, wout32_ref,
                         lg_ref, inv_ref,
                         x1_ref, convst_ref, retst_ref,
                         win_ref, wout_ref, cos_ref, sin_ref, qdec_ref, kdec_ref, dmask_ref, ones_ref,
                         smask_ref, sdec_ref, tail_ref, state_ref, ubuf_ref, *, n_tiles):
    j = pl.program_id(0)
    p = pl.program_id(1)
    C = TILE
    k_scale = HEAD_DIM ** -0.5

    def proj(xn, off):
        return _dot(xn, win_ref[:, off:off + 512])

    @pl.when((j == 0) & (p == 0))
    def _init():
        for r in range(0, D_MODEL, CAST_ROWS):
            win_ref[r:r + CAST_ROWS, :] = win32_ref[r:r + CAST_ROWS, :].astype(BF16)
            wout_ref[r:r + CAST_ROWS, :] = wout32_ref[r:r + CAST_ROWS, :].astype(BF16)
        i_f = _iota((C, 1), 0).astype(F32)
        lg = lg_ref[...]
        qdec_ref[...] = jnp.exp((i_f + 1.0) * lg)
        kdec_ref[...] = jnp.exp((C - 1.0 - i_f) * lg) * k_scale
        sdec_ref[...] = jnp.exp(C * lg)
        diff = (_iota((C, C), 0) - _iota((C, C), 1)).astype(F32)
        for h in range(RET_HEADS):
            dmask_ref[h] = jnp.where(
                diff >= 0, jnp.exp(jnp.maximum(diff, 0.0) * float(_LOG_GAMMA[h])) * k_scale, 0.0)
        ones = _ones_block()
        ones_ref[...] = ones
        smask = ones.astype(F32)
        smask_ref[...] = smask

        xm = _rmsnorm(meta_ref[...], g1_ref[...]).astype(BF16)
        um = proj(xm, OFF_C) * proj(xm, OFF_H)
        vm = proj(xm, OFF_V).astype(BF16)
        m_f = _iota((N_META, 1), 0).astype(F32)
        ang = m_f * inv_ref[...]
        sin = jnp.sin(ang)
        kmr = _rope(proj(xm, OFF_K), jnp.cos(ang), jnp.where(_first_half_lanes(), -sin, sin))
        kmd = (kmr * (jnp.exp((N_META - 1.0 - m_f) * lg) * k_scale)).astype(BF16)
        for b in range(tail_ref.shape[0]):
            tail_ref[b] = um[N_META - 2:N_META, :]
            for g in range(N_GROUPS):
                sl = slice(g * GROUP, (g + 1) * GROUP)
                state_ref[b, g] = _dot_tn(kmd[:, sl], vm[:, sl]) * smask

    @pl.when(p == 0)
    def _rope_tables():
        pos = (N_META + j * C + _iota((C, LANES), 0)).astype(F32)
        ang = pos * inv_ref[...]
        sin = jnp.sin(ang)
        cos_ref[...] = jnp.cos(ang)
        sin_ref[...] = jnp.where(_first_half_lanes(), -sin, sin)

    cos = cos_ref[...]
    sin = sin_ref[...]
    head_of_lane = _iota((1, GROUP), 1) >> 6
    smask = smask_ref[...]
    wc = wconv_ref[...]

    def mix_sequence(slot):
        n = p * SEQS_PER_STEP + slot
        x = x_ref[slot]
        xn = _rmsnorm(x, g1_ref[...]).astype(BF16)

        zb = proj(xn, OFF_B)
        u = proj(xn, OFF_C) * proj(xn, OFF_H)
        ubuf_ref[slot, 6:8, :] = tail_ref[n]
        ubuf_ref[slot, 8:8 + C, :] = u
        conv = wc[0:1] * ubuf_ref[slot, 6:6 + C, :] + wc[1:2] * ubuf_ref[slot, 7:7 + C, :] + wc[2:3] * u
        conv_out = (zb * conv).astype(BF16)
        new_tail = u[C - 2:C, :]
        tail_ref[n] = new_tail

        qr = _rope(proj(xn, OFF_Q), cos, sin)
        kr = _rope(proj(xn, OFF_K), cos, sin)
        vb = proj(xn, OFF_V).astype(BF16)
        qb = qr.astype(BF16)
        kb = kr.astype(BF16)
        qd = (qr * qdec_ref[...]).astype(BF16)
        kd = (kr * kdec_ref[...]).astype(BF16)
        o_parts = []
        new_states = []
        for g in range(N_GROUPS):
            sl = slice(g * GROUP, (g + 1) * GROUP)
            state = state_ref[n, g]
            acc = _dot(qd[:, sl], state.astype(BF16))
            for hh in range(HEADS_PER_GROUP):
                k_h = jnp.where(head_of_lane == hh, kb[:, sl], jnp.zeros_like(kb[:, sl]))
                v_h = jnp.where(head_of_lane == hh, vb[:, sl], jnp.zeros_like(vb[:, sl]))
                scores = _dot_nt(qb[:, sl], k_h) * dmask_ref[g * HEADS_PER_GROUP + hh]
                acc = acc + _dot(scores.astype(BF16), v_h)
            new_state = state * sdec_ref[:, sl] + _dot_tn(kd[:, sl], vb[:, sl]) * smask
            state_ref[n, g] = new_state
            new_states.append(new_state)
            o_parts.append(acc)
        o = jnp.concatenate(o_parts, axis=1)
        on = _group_stats_norm(o, ones_ref[...], rng_ref[...])
        ret_out = (_silu(proj(xn, OFF_G)) * on).astype(BF16)

        mix = jnp.concatenate([conv_out, ret_out], axis=1)
        x1_ref[slot] = x + _dot(mix, wout_ref[...])
        return new_tail, new_states

    finals = [mix_sequence(slot) for slot in range(SEQS_PER_STEP)]

    @pl.when(j == n_tiles - 1)
    def _final_states():
        for slot, (new_tail, new_states) in enumerate(finals):
            n = p * SEQS_PER_STEP + slot
            convst_ref[n] = new_tail
            for g in range(N_GROUPS):
                for hh in range(HEADS_PER_GROUP):
                    blk = slice(hh * HEAD_DIM, (hh + 1) * HEAD_DIM)
                    retst_ref[n, g * HEADS_PER_GROUP + hh] = new_states[g][blk, blk]


def _prompt_mixer(x_prompt, meta, g1, w_in, w_conv, rng, w_out, lg_lane, inv_lane):
    n_seq, seq, _ = x_prompt.shape
    n_tiles = seq // TILE
    const2 = lambda j, p: (0, 0)
    full = lambda a: pl.BlockSpec(a.shape, const2, pipeline_mode=pl.Buffered(1))
    return pl.pallas_call(
        functools.partial(_prompt_mixer_kernel, n_tiles=n_tiles),
        grid=(n_tiles, n_seq // SEQS_PER_STEP),
        in_specs=[pl.BlockSpec((SEQS_PER_STEP, TILE, D_MODEL), lambda j, p: (p, j, 0)),
                  full(meta), full(g1), full(w_in), full(w_conv), full(rng), full(w_out),
                  full(lg_lane), full(inv_lane)],
        out_specs=[pl.BlockSpec((SEQS_PER_STEP, TILE, D_MODEL), lambda j, p: (p, j, 0)),
                   pl.BlockSpec((n_seq, 2, CONV_DIM), lambda j, p: (0, 0, 0)),
                   pl.BlockSpec((n_seq, RET_HEADS, HEAD_DIM, HEAD_DIM), lambda j, p: (0, 0, 0, 0))],
        out_shape=[jax.ShapeDtypeStruct((n_seq, seq, D_MODEL), F32),
                   jax.ShapeDtypeStruct((n_seq, 2, CONV_DIM), F32),
                   jax.ShapeDtypeStruct((n_seq, RET_HEADS, HEAD_DIM, HEAD_DIM), F32)],
        scratch_shapes=[
            pltpu.VMEM(w_in.shape, BF16),
            pltpu.VMEM(w_out.shape, BF16),
            pltpu.VMEM((TILE, LANES), F32),
            pltpu.VMEM((TILE, LANES), F32),
            pltpu.VMEM((TILE, RET_DIM), F32),
            pltpu.VMEM((TILE, RET_DIM), F32),
            pltpu.VMEM((RET_HEADS, TILE, TILE), F32),
            pltpu.VMEM((GROUP, GROUP), BF16),
            pltpu.VMEM((GROUP, GROUP), F32),
            pltpu.VMEM((1, RET_DIM), F32),
            pltpu.VMEM((n_seq, 2, CONV_DIM), F32),
            pltpu.VMEM((n_seq, N_GROUPS, GROUP, GROUP), F32),
            pltpu.VMEM((SEQS_PER_STEP, TILE + 8, CONV_DIM), F32),
        ],
        compiler_params=pltpu.CompilerParams(
            dimension_semantics=("arbitrary", "arbitrary"), vmem_limit_bytes=VMEM_LIMIT),
        name="prompt_mixer",
    )(x_prompt, meta, g1, w_in, w_conv, rng, w_out, lg_lane, inv_lane)


def _ffn_rows(x, g2, wg_ref, wu_ref, wd_ref, gf):
    xn = _rmsnorm(x, g2).astype(BF16)
    hidden = (_silu(_dot(xn, wg_ref[...].astype(BF16))) * _dot(xn, wu_ref[...].astype(BF16))).astype(BF16)
    return _rmsnorm(x + _dot(hidden, wd_ref[...].astype(BF16)), gf)


def _ffn_kernel(xp_ref, xs_ref, g2_ref, wg_ref, wu_ref, wd_ref, gf_ref, yp_ref, ys_ref, *, prompt_steps):
    i = pl.program_id(0)

    def run(x_ref, y_ref):
        for c in range(FFN_TILE // FFN_CHUNK):
            rows = slice(c * FFN_CHUNK, (c + 1) * FFN_CHUNK)
            y_ref[rows, :] = _ffn_rows(x_ref[rows, :], g2_ref[...], wg_ref, wu_ref, wd_ref, gf_ref[...])

    @pl.when(i < prompt_steps)
    def _prompt_rows():
        run(xp_ref, yp_ref)

    @pl.when(i == prompt_steps)
    def _sample_rows():
        run(xs_ref, ys_ref)


def _ffn(x_p, x_s, g2, w_gate, w_up, w_down, gf):
    prompt_steps = x_p.shape[0] // FFN_TILE
    full = lambda a: pl.BlockSpec(a.shape, lambda i: (0, 0), pipeline_mode=pl.Buffered(1))
    prompt_tile = pl.BlockSpec((FFN_TILE, D_MODEL), lambda i: (jnp.minimum(i, prompt_steps - 1), 0))
    sample_tile = pl.BlockSpec((FFN_TILE, D_MODEL), lambda i: (0, 0))
    return pl.pallas_call(
        functools.partial(_ffn_kernel, prompt_steps=prompt_steps),
        grid=(prompt_steps + 1,),
        in_specs=[prompt_tile, sample_tile, full(g2), full(w_gate), full(w_up), full(w_down), full(gf)],
        out_specs=[prompt_tile, sample_tile],
        out_shape=[jax.ShapeDtypeStruct(x_p.shape, F32), jax.ShapeDtypeStruct(x_s.shape, F32)],
        compiler_params=pltpu.CompilerParams(
            dimension_semantics=("arbitrary",), vmem_limit_bytes=VMEM_LIMIT),
        name="ffn",
    )(x_p, x_s, g2, w_gate, w_up, w_down, gf)


def _gamma_pow(head, power):
    return float(np.exp(np.float32(power) * _LOG_GAMMA[head]))


def _sample_kernel(x_ref, st_ref, s_ref, g1_ref, win_ref, wconv_ref, gain_ref, wout_ref, inv_ref,
                   sdec_ref,
                   x1_ref, convst_ref, snew_ref,
                   qd_ref, kd_ref, v_ref, o_ref, zg_ref, cvo_ref, *, n_seq, dec_seq):
    h = pl.program_id(0)
    L, B = dec_seq, n_seq
    k_scale = HEAD_DIM ** -0.5

    @pl.when(h == 0)
    def _dense_front():
        xn = _rmsnorm(x_ref[...], g1_ref[...]).astype(BF16)

        def proj(off):
            return _dot(xn, win_ref[:, off:off + 512].astype(BF16))

        zb = proj(OFF_B)
        u = proj(OFF_C) * proj(OFF_H)
        ext = [st_ref[:, :CONV_DIM], st_ref[:, CONV_DIM:]] + [u[i * B:(i + 1) * B] for i in range(L)]
        wc = wconv_ref[...]
        for i in range(L):
            conv = wc[0:1] * ext[i] + wc[1:2] * ext[i + 1] + wc[2:3] * ext[i + 2]
            cvo_ref[i * B:(i + 1) * B, :] = (zb[i * B:(i + 1) * B] * conv).astype(BF16)
        convst_ref[...] = jnp.concatenate([ext[L], ext[L + 1]], axis=1)
        zg_ref[...] = proj(OFF_G)

        q_t = proj(OFF_Q).T
        k_t = proj(OFF_K).T
        v_t = proj(OFF_V).T
        v_ref[...] = v_t
        inv = inv_ref[...]
        cos_sin = [(jnp.cos(float(PAST_LEN + i) * inv), jnp.sin(float(PAST_LEN + i) * inv))
                   for i in range(L)]

        def rope(t, r0, i):
            cos, sin = cos_sin[i]
            t1 = t[r0:r0 + HALF_DIM, i * B:(i + 1) * B]
            t2 = t[r0 + HALF_DIM:r0 + HEAD_DIM, i * B:(i + 1) * B]
            return jnp.concatenate([t1 * cos - t2 * sin, t2 * cos + t1 * sin], axis=0)

        for hd in range(RET_HEADS):
            r0 = hd * HEAD_DIM
            qr = [rope(q_t, r0, i) for i in range(L)]
            kr = [rope(k_t, r0, i) for i in range(L)]
            for i in range(L):
                lanes = slice(i * B, (i + 1) * B)
                qd_ref[r0:r0 + HEAD_DIM, lanes] = qr[i] * _gamma_pow(hd, i + 1)
                kd_ref[r0:r0 + HEAD_DIM, lanes] = kr[i] * (_gamma_pow(hd, L - 1 - i) * k_scale)
                intra = jnp.zeros((HEAD_DIM, B), F32)
                for j in range(i + 1):
                    score = jnp.sum(qr[i] * kr[j], axis=0, keepdims=True) * (_gamma_pow(hd, i - j) * k_scale)
                    intra = intra + score * v_t[r0:r0 + HEAD_DIM, j * B:(j + 1) * B]
                o_ref[r0:r0 + HEAD_DIM, lanes] = intra

    base = pl.multiple_of(h * HEAD_DIM, HEAD_DIM)
    qd_h = qd_ref[pl.ds(base, HEAD_DIM), :]
    kd_h = kd_ref[pl.ds(base, HEAD_DIM), :]
    state_decay = sdec_ref[h]
    for e0 in range(0, HEAD_DIM, E_CHUNK):
        rows = pl.ds(pl.multiple_of(base + e0, E_CHUNK), E_CHUNK)
        v_blk = [v_ref[rows, j * B:(j + 1) * B] for j in range(L)]
        cross = [jnp.zeros((E_CHUNK, B), F32) for _ in range(L)]
        for d in range(HEAD_DIM):
            s_de = s_ref[0, d, e0:e0 + E_CHUNK, :]
            new = s_de * state_decay
            for t in range(L):
                lanes = slice(t * B, (t + 1) * B)
                cross[t] = cross[t] + qd_h[d:d + 1, lanes] * s_de
                new = new + kd_h[d:d + 1, lanes] * v_blk[t]
            snew_ref[0, d, e0:e0 + E_CHUNK, :] = new
        for i in range(L):
            o_ref[rows, i * B:(i + 1) * B] = o_ref[rows, i * B:(i + 1) * B] + cross[i]

    @pl.when(h == RET_HEADS - 1)
    def _dense_back():
        for hd in range(RET_HEADS):
            r0 = hd * HEAD_DIM
            o_h = o_ref[r0:r0 + HEAD_DIM, :]
            mu = jnp.sum(o_h, axis=0, keepdims=True) * (1.0 / HEAD_DIM)
            dlt = o_h - mu
            var = jnp.sum(dlt * dlt, axis=0, keepdims=True) * (1.0 / HEAD_DIM)
            o_ref[r0:r0 + HEAD_DIM, :] = dlt * lax.rsqrt(var + GN_EPS) * gain_ref[r0:r0 + HEAD_DIM, :]
        ret_out = (_silu(zg_ref[...]) * o_ref[...].T).astype(BF16)
        x1_ref[...] = (x_ref[...] + _dot(cvo_ref[...], wout_ref[:CONV_DIM, :].astype(BF16))
                       + _dot(ret_out, wout_ref[CONV_DIM:, :].astype(BF16)))


def _sample_mixer(xs, st, s4, g1, w_in, w_conv, gain, w_out, inv_col, sdec, dec_seq):
    rows = xs.shape[0]
    n_seq = rows // dec_seq
    full = lambda a: pl.BlockSpec(a.shape, lambda h: (0,) * a.ndim, pipeline_mode=pl.Buffered(1))
    state_spec = pl.BlockSpec((1, HEAD_DIM, HEAD_DIM, n_seq), lambda h: (h, 0, 0, 0))
    consts = (g1, w_in, w_conv, gain, w_out, inv_col, sdec)
    return pl.pallas_call(
        functools.partial(_sample_kernel, n_seq=n_seq, dec_seq=dec_seq),
        grid=(RET_HEADS,),
        in_specs=[full(xs), full(st), state_spec] + [full(a) for a in consts],
        out_specs=[pl.BlockSpec((rows, D_MODEL), lambda h: (0, 0)),
                   pl.BlockSpec(st.shape, lambda h: (0, 0)),
                   state_spec],
        out_shape=[jax.ShapeDtypeStruct((rows, D_MODEL), F32),
                   jax.ShapeDtypeStruct(st.shape, F32),
                   jax.ShapeDtypeStruct(s4.shape, F32)],
        scratch_shapes=[
            pltpu.VMEM((RET_DIM, rows), F32),
            pltpu.VMEM((RET_DIM, rows), F32),
            pltpu.VMEM((RET_DIM, rows), F32),
            pltpu.VMEM((RET_DIM, rows), F32),
            pltpu.VMEM((rows, RET_DIM), F32),
            pltpu.VMEM((rows, CONV_DIM), BF16),
        ],
        compiler_params=pltpu.CompilerParams(
            dimension_semantics=("arbitrary",), vmem_limit_bytes=VMEM_LIMIT),
        name="sample_mixer",
    )(xs, st, s4, *consts)


def kernel(x_prompt, x_sample, state_conv, state_ret, meta_tokens, norm1_g, w_in, w_conv, ret_norm_g,
           w_out, norm2_g, w_gate, w_up, w_down, final_norm_g):
    n_p, seq, _ = x_prompt.shape
    n_s, dec_seq, _ = x_sample.shape
    rows_s = n_s * dec_seq
    assert norm1_g.shape[0] == 1 and seq % TILE == 0 and n_p % SEQS_PER_STEP == 0
    assert (n_p * seq) % FFN_TILE == 0
    assert n_s % LANES == 0 and rows_s == FFN_TILE

    g1, g2, gf = norm1_g[0][None], norm2_g[0][None], final_norm_g[None]
    rng = ret_norm_g[0][None]
    lg_lane = jnp.asarray(_LOG_GAMMA[np.arange(RET_DIM) // HEAD_DIM][None])
    inv = ROPE_BASE ** (-jnp.arange(HALF_DIM, dtype=F32) / HALF_DIM)
    inv_lane = jnp.tile(inv, LANES // HALF_DIM)[None]

    x1_p, conv_p, ret_p = _prompt_mixer(x_prompt, meta_tokens, g1, w_in[0], w_conv[0], rng, w_out[0],
                                        lg_lane, inv_lane)

    s4 = jnp.transpose(state_ret[0], (1, 2, 3, 0))
    st = state_conv[0].reshape(n_s, 2 * CONV_DIM)
    gain_col = jnp.broadcast_to(ret_norm_g[0][:, None], (RET_DIM, rows_s))
    inv_col = jnp.broadcast_to(inv[:, None], (HALF_DIM, n_s))
    sdec = jnp.asarray(np.broadcast_to(
        np.exp(np.float32(dec_seq) * _LOG_GAMMA)[:, None, None], (RET_HEADS, E_CHUNK, n_s)))
    xs = jnp.transpose(x_sample, (1, 0, 2)).reshape(rows_s, D_MODEL)
    x1_s, conv_s, s4_new = _sample_mixer(xs, st, s4, g1, w_in[0], w_conv[0], gain_col, w_out[0], inv_col,
                                         sdec, dec_seq)

    y_p, y_s = _ffn(x1_p.reshape(n_p * seq, D_MODEL), x1_s, g2, w_gate[0], w_up[0], w_down[0], gf)
    y_prompt = y_p.reshape(n_p, seq, D_MODEL)
    y_sample = jnp.transpose(y_s.reshape(dec_seq, n_s, D_MODEL), (1, 0, 2))
    ret_s = jnp.transpose(s4_new, (3, 0, 1, 2))

    return (y_prompt, y_sample, conv_p[None], ret_p[None], conv_s.reshape(1, n_s, 2, CONV_DIM), ret_s[None])
```

```python
import functools

import numpy as np
import jax
import jax.numpy as jnp
from jax import lax
from jax.experimental import pallas as pl
from jax.experimental.pallas import tpu as pltpu

D_MODEL = 1024
N_META = 16
CONV_DIM = 512
RET_HEADS = 8
HEAD_DIM = 64
HALF_DIM = HEAD_DIM // 2
RET_DIM = RET_HEADS * HEAD_DIM
D_FF = 2816
PAST_LEN = 16384
ROPE_BASE = 10000.0
EPS = 1e-6
GN_EPS = 1e-5

OFF_B, OFF_C, OFF_H, OFF_Q, OFF_K, OFF_V, OFF_G = (i * 512 for i in range(7))

LANES = 128
GROUP = 256
HEADS_PER_GROUP = GROUP // HEAD_DIM
N_GROUPS = RET_DIM // GROUP
TILE = 256
SEQS_PER_STEP = 2
FFN_TILE = 512
FFN_CHUNK = 256
E_CHUNK = 32
CAST_ROWS = 128
BF16_ROWS = 16
VMEM_LIMIT = 56 * 1024 * 1024

F32 = jnp.float32
BF16 = jnp.bfloat16

_LOG_GAMMA = np.log1p(-(2.0 ** (-5.0 - np.arange(RET_HEADS)))).astype(np.float32)


def _dot(a, b):
    return jnp.dot(a, b, preferred_element_type=F32)


def _dot_nt(a, b):
    return lax.dot_general(a, b, (((1,), (1,)), ((), ())), preferred_element_type=F32)


def _dot_tn(a, b):
    return lax.dot_general(a, b, (((0,), (0,)), ((), ())), preferred_element_type=F32)


def _rmsnorm(x, g):
    ms = jnp.mean(x * x, axis=-1, keepdims=True)
    return x * lax.rsqrt(ms + EPS) * g


def _silu(x):
    return x * jax.nn.sigmoid(x)


def _iota(shape, dim):
    return lax.broadcasted_iota(jnp.int32, shape, dim)


def _first_half_lanes():
    return (_iota((1, LANES), 1) & (HEAD_DIM - 1)) < HALF_DIM


def _rope(t, cos, signed_sin):
    first_half = _first_half_lanes()
    out = []
    for b in range(t.shape[1] // LANES):
        blk = t[:, b * LANES:(b + 1) * LANES]
        partner = jnp.where(first_half, pltpu.roll(blk, LANES - HALF_DIM, axis=1),
                            pltpu.roll(blk, HALF_DIM, axis=1))
        out.append(blk * cos + partner * signed_sin)
    return jnp.concatenate(out, axis=1)


def _segment_sum(t, ones_blk):
    return jnp.concatenate(
        [_dot(t[:, g * GROUP:(g + 1) * GROUP].astype(BF16), ones_blk) for g in range(N_GROUPS)], axis=1)


def _group_stats_norm(o, ones_blk, gain):
    mu = _segment_sum(o, ones_blk) * (1.0 / HEAD_DIM)
    d = o - mu
    var = _segment_sum(d * d, ones_blk) * (1.0 / HEAD_DIM)
    return d * lax.rsqrt(var + GN_EPS) * gain


def _ones_block():
    r = _iota((GROUP, GROUP), 0) >> 6
    c = _iota((GROUP, GROUP), 1) >> 6
    return jnp.where(r == c, 1.0, 0.0).astype(BF16)


def _prompt_mixer_kernel(x_ref, meta_ref, g1_ref, win32_ref, wconv_ref, rng_ref, wout32_ref,
                         lg_ref, inv_ref, wg32_ref, wu32_ref, wd32_ref,
                         x1_ref, convst_ref, retst_ref, win_ref, wout_ref, wg_ref, wu_ref, wd_ref,
                         cos_ref, sin_ref, qdec_ref, kdec_ref, dmask_ref, ones_ref,
                         smask_ref, sdec_ref, tail_ref, state_ref, ubuf_ref, *, n_tiles):
    j = pl.program_id(0)
    p = pl.program_id(1)
    C = TILE
    k_scale = HEAD_DIM ** -0.5

    def proj(xn, off):
        return _dot(xn, win_ref[:, off:off + 512])

    @pl.when((j == 0) & (p == 0))
    def _init():
        for r in range(0, D_MODEL, CAST_ROWS):
            win_ref[r:r + CAST_ROWS, :] = win32_ref[r:r + CAST_ROWS, :].astype(BF16)
            wout_ref[r:r + CAST_ROWS, :] = wout32_ref[r:r + CAST_ROWS, :].astype(BF16)
        i_f = _iota((C, 1), 0).astype(F32)
        lg = lg_ref[...]
        qdec_ref[...] = jnp.exp((i_f + 1.0) * lg)
        kdec_ref[...] = jnp.exp((C - 1.0 - i_f) * lg) * k_scale
        sdec_ref[...] = jnp.exp(C * lg)
        diff = (_iota((C, C), 0) - _iota((C, C), 1)).astype(F32)
        for h in range(RET_HEADS):
            dmask_ref[h] = jnp.where(
                diff >= 0, jnp.exp(jnp.maximum(diff, 0.0) * float(_LOG_GAMMA[h])) * k_scale, 0.0)
        ones = _ones_block()
        ones_ref[...] = ones
        smask = ones.astype(F32)
        smask_ref[...] = smask

        xm = _rmsnorm(meta_ref[...], g1_ref[...]).astype(BF16)
        um = proj(xm, OFF_C) * proj(xm, OFF_H)
        vm = proj(xm, OFF_V).astype(BF16)
        m_f = _iota((N_META, 1), 0).astype(F32)
        ang = m_f * inv_ref[...]
        sin = jnp.sin(ang)
        kmr = _rope(proj(xm, OFF_K), jnp.cos(ang), jnp.where(_first_half_lanes(), -sin, sin))
        kmd = (kmr * (jnp.exp((N_META - 1.0 - m_f) * lg) * k_scale)).astype(BF16)
        for b in range(tail_ref.shape[0]):
            tail_ref[b] = um[N_META - 2:N_META, :]
            for g in range(N_GROUPS):
                sl = slice(g * GROUP, (g + 1) * GROUP)
                state_ref[b, g] = _dot_tn(kmd[:, sl], vm[:, sl]) * smask

    @pl.when(p == 0)
    def _rope_tables():
        pos = (N_META + j * C + _iota((C, LANES), 0)).astype(F32)
        ang = pos * inv_ref[...]
        sin = jnp.sin(ang)
        cos_ref[...] = jnp.cos(ang)
        sin_ref[...] = jnp.where(_first_half_lanes(), -sin, sin)

    cos = cos_ref[...]
    sin = sin_ref[...]
    head_of_lane = _iota((1, GROUP), 1) >> 6
    smask = smask_ref[...]
    wc = wconv_ref[...]

    def mix_sequence(slot):
        n = p * SEQS_PER_STEP + slot
        x = x_ref[slot]
        xn = _rmsnorm(x, g1_ref[...]).astype(BF16)

        zb = proj(xn, OFF_B)
        u = proj(xn, OFF_C) * proj(xn, OFF_H)
        ubuf_ref[slot, 6:8, :] = tail_ref[n]
        ubuf_ref[slot, 8:8 + C, :] = u
        conv = wc[0:1] * ubuf_ref[slot, 6:6 + C, :] + wc[1:2] * ubuf_ref[slot, 7:7 + C, :] + wc[2:3] * u
        conv_out = (zb * conv).astype(BF16)
        new_tail = u[C - 2:C, :]
        tail_ref[n] = new_tail

        qr = _rope(proj(xn, OFF_Q), cos, sin)
        kr = _rope(proj(xn, OFF_K), cos, sin)
        vb = proj(xn, OFF_V).astype(BF16)
        qb = qr.astype(BF16)
        kb = kr.astype(BF16)
        qd = (qr * qdec_ref[...]).astype(BF16)
        kd = (kr * kdec_ref[...]).astype(BF16)
        o_parts = []
        new_states = []
        for g in range(N_GROUPS):
            sl = slice(g * GROUP, (g + 1) * GROUP)
            state = state_ref[n, g]
            acc = _dot(qd[:, sl], state.astype(BF16))
            for hh in range(HEADS_PER_GROUP):
                k_h = jnp.where(head_of_lane == hh, kb[:, sl], jnp.zeros_like(kb[:, sl]))
                v_h = jnp.where(head_of_lane == hh, vb[:, sl], jnp.zeros_like(vb[:, sl]))
                scores = _dot_nt(qb[:, sl], k_h) * dmask_ref[g * HEADS_PER_GROUP + hh]
                acc = acc + _dot(scores.astype(BF16), v_h)
            new_state = state * sdec_ref[:, sl] + _dot_tn(kd[:, sl], vb[:, sl]) * smask
            state_ref[n, g] = new_state
            new_states.append(new_state)
            o_parts.append(acc)
        o = jnp.concatenate(o_parts, axis=1)
        on = _group_stats_norm(o, ones_ref[...], rng_ref[...])
        ret_out = (_silu(proj(xn, OFF_G)) * on).astype(BF16)

        mix = jnp.concatenate([conv_out, ret_out], axis=1)
        x1_ref[slot] = x + _dot(mix, wout_ref[...])
        return new_tail, new_states

    finals = [mix_sequence(slot) for slot in range(SEQS_PER_STEP)]

    wg_ref[...] = wg32_ref[...].astype(BF16)
    wu_ref[...] = wu32_ref[...].astype(BF16)
    wd_ref[...] = wd32_ref[...].astype(BF16)

    @pl.when(j == n_tiles - 1)
    def _final_states():
        for slot, (new_tail, new_states) in enumerate(finals):
            n = p * SEQS_PER_STEP + slot
            convst_ref[n] = new_tail
            for g in range(N_GROUPS):
                for hh in range(HEADS_PER_GROUP):
                    blk = slice(hh * HEAD_DIM, (hh + 1) * HEAD_DIM)
                    retst_ref[n, g * HEADS_PER_GROUP + hh] = new_states[g][blk, blk]


def _prompt_mixer(x_prompt, meta, g1, w_in, w_conv, rng, w_out, lg_lane, inv_lane, w_gate, w_up, w_down):
    n_seq, seq, _ = x_prompt.shape
    n_tiles = seq // TILE
    pairs = n_seq // SEQS_PER_STEP
    n_steps = n_tiles * pairs
    const2 = lambda j, p: (0, 0)
    full = lambda a: pl.BlockSpec(a.shape, const2, pipeline_mode=pl.Buffered(1))
    gu_rows = w_gate.shape[0] // n_steps
    d_rows = w_down.shape[0] // (n_steps // 2)
    assert gu_rows * n_steps == w_gate.shape[0] and d_rows * (n_steps // 2) == w_down.shape[0]
    assert gu_rows % BF16_ROWS == 0 and d_rows % BF16_ROWS == 0
    gu_slab = pl.BlockSpec((gu_rows, w_gate.shape[1]), lambda j, p: (j * pairs + p, 0))
    d_slab = pl.BlockSpec((d_rows, w_down.shape[1]), lambda j, p: ((j * pairs + p) // 2, 0))
    return pl.pallas_call(
        functools.partial(_prompt_mixer_kernel, n_tiles=n_tiles),
        grid=(n_tiles, pairs),
        in_specs=[pl.BlockSpec((SEQS_PER_STEP, TILE, D_MODEL), lambda j, p: (p, j, 0)),
                  full(meta), full(g1), full(w_in), full(w_conv), full(rng), full(w_out),
                  full(lg_lane), full(inv_lane), gu_slab, gu_slab, d_slab],
        out_specs=[pl.BlockSpec((SEQS_PER_STEP, TILE, D_MODEL), lambda j, p: (p, j, 0)),
                   pl.BlockSpec((n_seq, 2, CONV_DIM), lambda j, p: (0, 0, 0)),
                   pl.BlockSpec((n_seq, RET_HEADS, HEAD_DIM, HEAD_DIM), lambda j, p: (0, 0, 0, 0)),
                   full(w_in), full(w_out), gu_slab, gu_slab, d_slab],
        out_shape=[jax.ShapeDtypeStruct((n_seq, seq, D_MODEL), F32),
                   jax.ShapeDtypeStruct((n_seq, 2, CONV_DIM), F32),
                   jax.ShapeDtypeStruct((n_seq, RET_HEADS, HEAD_DIM, HEAD_DIM), F32),
                   jax.ShapeDtypeStruct(w_in.shape, BF16), jax.ShapeDtypeStruct(w_out.shape, BF16),
                   jax.ShapeDtypeStruct(w_gate.shape, BF16), jax.ShapeDtypeStruct(w_up.shape, BF16),
                   jax.ShapeDtypeStruct(w_down.shape, BF16)],
        scratch_shapes=[
            pltpu.VMEM((TILE, LANES), F32),
            pltpu.VMEM((TILE, LANES), F32),
            pltpu.VMEM((TILE, RET_DIM), F32),
            pltpu.VMEM((TILE, RET_DIM), F32),
            pltpu.VMEM((RET_HEADS, TILE, TILE), F32),
            pltpu.VMEM((GROUP, GROUP), BF16),
            pltpu.VMEM((GROUP, GROUP), F32),
            pltpu.VMEM((1, RET_DIM), F32),
            pltpu.VMEM((n_seq, 2, CONV_DIM), F32),
            pltpu.VMEM((n_seq, N_GROUPS, GROUP, GROUP), F32),
            pltpu.VMEM((SEQS_PER_STEP, TILE + 8, CONV_DIM), F32),
        ],
        compiler_params=pltpu.CompilerParams(
            dimension_semantics=("arbitrary", "arbitrary"), vmem_limit_bytes=VMEM_LIMIT),
        name="prompt_mixer",
    )(x_prompt, meta, g1, w_in, w_conv, rng, w_out, lg_lane, inv_lane, w_gate, w_up, w_down)


def _ffn_rows(x, g2, wg_ref, wu_ref, wd_ref, gf):
    xn = _rmsnorm(x, g2).astype(BF16)
    hidden = (_silu(_dot(xn, wg_ref[...])) * _dot(xn, wu_ref[...])).astype(BF16)
    return _rmsnorm(x + _dot(hidden, wd_ref[...]), gf)


def _ffn_kernel(xp_ref, xs_ref, g2_ref, wg_ref, wu_ref, wd_ref, gf_ref, yp_ref, ys_ref, *, prompt_steps):
    i = pl.program_id(0)

    def run(x_ref, y_ref):
        for c in range(FFN_TILE // FFN_CHUNK):
            rows = slice(c * FFN_CHUNK, (c + 1) * FFN_CHUNK)
            y_ref[rows, :] = _ffn_rows(x_ref[rows, :], g2_ref[...], wg_ref, wu_ref, wd_ref, gf_ref[...])

    @pl.when(i < prompt_steps)
    def _prompt_rows():
        run(xp_ref, yp_ref)

    @pl.when(i == prompt_steps)
    def _sample_rows():
        run(xs_ref, ys_ref)


def _ffn(x_p, x_s, g2, w_gate, w_up, w_down, gf):
    prompt_steps = x_p.shape[0] // FFN_TILE
    full = lambda a: pl.BlockSpec(a.shape, lambda i: (0, 0), pipeline_mode=pl.Buffered(1))
    prompt_tile = pl.BlockSpec((FFN_TILE, D_MODEL), lambda i: (jnp.minimum(i, prompt_steps - 1), 0))
    sample_tile = pl.BlockSpec((FFN_TILE, D_MODEL), lambda i: (0, 0))
    return pl.pallas_call(
        functools.partial(_ffn_kernel, prompt_steps=prompt_steps),
        grid=(prompt_steps + 1,),
        in_specs=[prompt_tile, sample_tile, full(g2), full(w_gate), full(w_up), full(w_down), full(gf)],
        out_specs=[prompt_tile, sample_tile],
        out_shape=[jax.ShapeDtypeStruct(x_p.shape, F32), jax.ShapeDtypeStruct(x_s.shape, F32)],
        compiler_params=pltpu.CompilerParams(
            dimension_semantics=("arbitrary",), vmem_limit_bytes=VMEM_LIMIT),
        name="ffn",
    )(x_p, x_s, g2, w_gate, w_up, w_down, gf)


def _gamma_pow(head, power):
    return float(np.exp(np.float32(power) * _LOG_GAMMA[head]))


def _sample_kernel(x_ref, st_ref, s_ref, g1_ref, win_ref, wconv_ref, gain_ref, wout_ref, inv_ref,
                   sdec_ref,
                   x1_ref, convst_ref, snew_ref,
                   qd_ref, kd_ref, v_ref, o_ref, zg_ref, cvo_ref, *, n_seq, dec_seq):
    h = pl.program_id(0)
    L, B = dec_seq, n_seq
    k_scale = HEAD_DIM ** -0.5

    @pl.when(h == 0)
    def _dense_front():
        xn = _rmsnorm(x_ref[...], g1_ref[...]).astype(BF16)

        def proj(off):
            return _dot(xn, win_ref[:, off:off + 512])

        zb = proj(OFF_B)
        u = proj(OFF_C) * proj(OFF_H)
        ext = [st_ref[:, :CONV_DIM], st_ref[:, CONV_DIM:]] + [u[i * B:(i + 1) * B] for i in range(L)]
        wc = wconv_ref[...]
        for i in range(L):
            conv = wc[0:1] * ext[i] + wc[1:2] * ext[i + 1] + wc[2:3] * ext[i + 2]
            cvo_ref[i * B:(i + 1) * B, :] = (zb[i * B:(i + 1) * B] * conv).astype(BF16)
        convst_ref[...] = jnp.concatenate([ext[L], ext[L + 1]], axis=1)
        zg_ref[...] = proj(OFF_G)

        q_t = proj(OFF_Q).T
        k_t = proj(OFF_K).T
        v_t = proj(OFF_V).T
        v_ref[...] = v_t
        inv = inv_ref[...]
        cos_sin = [(jnp.cos(float(PAST_LEN + i) * inv), jnp.sin(float(PAST_LEN + i) * inv))
                   for i in range(L)]

        def rope(t, r0, i):
            cos, sin = cos_sin[i]
            t1 = t[r0:r0 + HALF_DIM, i * B:(i + 1) * B]
            t2 = t[r0 + HALF_DIM:r0 + HEAD_DIM, i * B:(i + 1) * B]
            return jnp.concatenate([t1 * cos - t2 * sin, t2 * cos + t1 * sin], axis=0)

        for hd in range(RET_HEADS):
            r0 = hd * HEAD_DIM
            qr = [rope(q_t, r0, i) for i in range(L)]
            kr = [rope(k_t, r0, i) for i in range(L)]
            for i in range(L):
                lanes = slice(i * B, (i + 1) * B)
                qd_ref[r0:r0 + HEAD_DIM, lanes] = qr[i] * _gamma_pow(hd, i + 1)
                kd_ref[r0:r0 + HEAD_DIM, lanes] = kr[i] * (_gamma_pow(hd, L - 1 - i) * k_scale)
                intra = jnp.zeros((HEAD_DIM, B), F32)
                for j in range(i + 1):
                    score = jnp.sum(qr[i] * kr[j], axis=0, keepdims=True) * (_gamma_pow(hd, i - j) * k_scale)
                    intra = intra + score * v_t[r0:r0 + HEAD_DIM, j * B:(j + 1) * B]
                o_ref[r0:r0 + HEAD_DIM, lanes] = intra

    base = pl.multiple_of(h * HEAD_DIM, HEAD_DIM)
    qd_h = qd_ref[pl.ds(base, HEAD_DIM), :]
    kd_h = kd_ref[pl.ds(base, HEAD_DIM), :]
    state_decay = sdec_ref[h]
    for e0 in range(0, HEAD_DIM, E_CHUNK):
        rows = pl.ds(pl.multiple_of(base + e0, E_CHUNK), E_CHUNK)
        v_blk = [v_ref[rows, j * B:(j + 1) * B] for j in range(L)]
        cross = [jnp.zeros((E_CHUNK, B), F32) for _ in range(L)]
        for d in range(HEAD_DIM):
            s_de = s_ref[0, d, e0:e0 + E_CHUNK, :]
            new = s_de * state_decay
            for t in range(L):
                lanes = slice(t * B, (t + 1) * B)
                cross[t] = cross[t] + qd_h[d:d + 1, lanes] * s_de
                new = new + kd_h[d:d + 1, lanes] * v_blk[t]
            snew_ref[0, d, e0:e0 + E_CHUNK, :] = new
        for i in range(L):
            o_ref[rows, i * B:(i + 1) * B] = o_ref[rows, i * B:(i + 1) * B] + cross[i]

    @pl.when(h == RET_HEADS - 1)
    def _dense_back():
        for hd in range(RET_HEADS):
            r0 = hd * HEAD_DIM
            o_h = o_ref[r0:r0 + HEAD_DIM, :]
            mu = jnp.sum(o_h, axis=0, keepdims=True) * (1.0 / HEAD_DIM)
            dlt = o_h - mu
            var = jnp.sum(dlt * dlt, axis=0, keepdims=True) * (1.0 / HEAD_DIM)
            o_ref[r0:r0 + HEAD_DIM, :] = dlt * lax.rsqrt(var + GN_EPS) * gain_ref[r0:r0 + HEAD_DIM, :]
        ret_out = (_silu(zg_ref[...]) * o_ref[...].T).astype(BF16)
        x1_ref[...] = (x_ref[...] + _dot(cvo_ref[...], wout_ref[:CONV_DIM, :])
                       + _dot(ret_out, wout_ref[CONV_DIM:, :]))


def _sample_mixer(xs, st, s4, g1, w_in, w_conv, gain, w_out, inv_col, sdec, dec_seq):
    rows = xs.shape[0]
    n_seq = rows // dec_seq
    full = lambda a: pl.BlockSpec(a.shape, lambda h: (0,) * a.ndim, pipeline_mode=pl.Buffered(1))
    state_spec = pl.BlockSpec((1, HEAD_DIM, HEAD_DIM, n_seq), lambda h: (h, 0, 0, 0))
    consts = (g1, w_in, w_conv, gain, w_out, inv_col, sdec)
    return pl.pallas_call(
        functools.partial(_sample_kernel, n_seq=n_seq, dec_seq=dec_seq),
        grid=(RET_HEADS,),
        in_specs=[full(xs), full(st), state_spec] + [full(a) for a in consts],
        out_specs=[pl.BlockSpec((rows, D_MODEL), lambda h: (0, 0)),
                   pl.BlockSpec(st.shape, lambda h: (0, 0)),
                   state_spec],
        out_shape=[jax.ShapeDtypeStruct((rows, D_MODEL), F32),
                   jax.ShapeDtypeStruct(st.shape, F32),
                   jax.ShapeDtypeStruct(s4.shape, F32)],
        scratch_shapes=[
            pltpu.VMEM((RET_DIM, rows), F32),
            pltpu.VMEM((RET_DIM, rows), F32),
            pltpu.VMEM((RET_DIM, rows), F32),
            pltpu.VMEM((RET_DIM, rows), F32),
            pltpu.VMEM((rows, RET_DIM), F32),
            pltpu.VMEM((rows, CONV_DIM), BF16),
        ],
        compiler_params=pltpu.CompilerParams(
            dimension_semantics=("arbitrary",), vmem_limit_bytes=VMEM_LIMIT),
        name="sample_mixer",
    )(xs, st, s4, *consts)


def kernel(x_prompt, x_sample, state_conv, state_ret, meta_tokens, norm1_g, w_in, w_conv, ret_norm_g,
           w_out, norm2_g, w_gate, w_up, w_down, final_norm_g):
    n_p, seq, _ = x_prompt.shape
    n_s, dec_seq, _ = x_sample.shape
    rows_s = n_s * dec_seq
    assert norm1_g.shape[0] == 1 and seq % TILE == 0 and n_p % SEQS_PER_STEP == 0
    assert (n_p * seq) % FFN_TILE == 0
    assert n_s % LANES == 0 and rows_s == FFN_TILE

    g1, g2, gf = norm1_g[0][None], norm2_g[0][None], final_norm_g[None]
    rng = ret_norm_g[0][None]
    lg_lane = jnp.asarray(_LOG_GAMMA[np.arange(RET_DIM) // HEAD_DIM][None])
    inv = ROPE_BASE ** (-jnp.arange(HALF_DIM, dtype=F32) / HALF_DIM)
    inv_lane = jnp.tile(inv, LANES // HALF_DIM)[None]

    x1_p, conv_p, ret_p, w_in_b, w_out_b, wg_b, wu_b, wd_b = _prompt_mixer(
        x_prompt, meta_tokens, g1, w_in[0], w_conv[0], rng, w_out[0], lg_lane, inv_lane,
        w_gate[0], w_up[0], w_down[0])

    s4 = jnp.transpose(state_ret[0], (1, 2, 3, 0))
    st = state_conv[0].reshape(n_s, 2 * CONV_DIM)
    gain_col = jnp.broadcast_to(ret_norm_g[0][:, None], (RET_DIM, rows_s))
    inv_col = jnp.broadcast_to(inv[:, None], (HALF_DIM, n_s))
    sdec = jnp.asarray(np.broadcast_to(
        np.exp(np.float32(dec_seq) * _LOG_GAMMA)[:, None, None], (RET_HEADS, E_CHUNK, n_s)))
    xs = jnp.transpose(x_sample, (1, 0, 2)).reshape(rows_s, D_MODEL)
    x1_s, conv_s, s4_new = _sample_mixer(xs, st, s4, g1, w_in_b, w_conv[0], gain_col, w_out_b, inv_col,
                                         sdec, dec_seq)

    y_p, y_s = _ffn(x1_p.reshape(n_p * seq, D_MODEL), x1_s, g2, wg_b, wu_b, wd_b, gf)
    y_prompt = y_p.reshape(n_p, seq, D_MODEL)
    y_sample = jnp.transpose(y_s.reshape(dec_seq, n_s, D_MODEL), (1, 0, 2))
    ret_s = jnp.transpose(s4_new, (3, 0, 1, 2))

    return (y_prompt, y_sample, conv_p[None], ret_p[None], conv_s.reshape(1, n_s, 2, CONV_DIM), ret_s[None])
```

```python
import functools

import numpy as np
import jax
import jax.numpy as jnp
from jax import lax
from jax.experimental import pallas as pl
from jax.experimental.pallas import tpu as pltpu

D_MODEL = 1024
N_META = 16
CONV_DIM = 512
RET_HEADS = 8
HEAD_DIM = 64
HALF_DIM = HEAD_DIM // 2
RET_DIM = RET_HEADS * HEAD_DIM
D_FF = 2816
PAST_LEN = 16384
ROPE_BASE = 10000.0
EPS = 1e-6
GN_EPS = 1e-5

OFF_B, OFF_C, OFF_H, OFF_Q, OFF_K, OFF_V, OFF_G = (i * 512 for i in range(7))

LANES = 128
GROUP = 256
HEADS_PER_GROUP = GROUP // HEAD_DIM
N_GROUPS = RET_DIM // GROUP
TILE = 256
SEQS_PER_STEP = 4
FFN_TILE = 1024
FFN_CHUNK = 256
E_CHUNK = 32
CAST_ROWS = 128
BF16_ROWS = 16
VMEM_LIMIT = 56 * 1024 * 1024

F32 = jnp.float32
BF16 = jnp.bfloat16

_LOG_GAMMA = np.log1p(-(2.0 ** (-5.0 - np.arange(RET_HEADS)))).astype(np.float32)


def _dot(a, b):
    return jnp.dot(a, b, preferred_element_type=F32)


def _dot_nt(a, b):
    return lax.dot_general(a, b, (((1,), (1,)), ((), ())), preferred_element_type=F32)


def _dot_tn(a, b):
    return lax.dot_general(a, b, (((0,), (0,)), ((), ())), preferred_element_type=F32)


def _rmsnorm(x, g):
    ms = jnp.mean(x * x, axis=-1, keepdims=True)
    return x * lax.rsqrt(ms + EPS) * g


def _silu(x):
    return x * jax.nn.sigmoid(x)


def _iota(shape, dim):
    return lax.broadcasted_iota(jnp.int32, shape, dim)


def _first_half_lanes():
    return (_iota((1, LANES), 1) & (HEAD_DIM - 1)) < HALF_DIM


def _rope(t, cos, signed_sin):
    first_half = _first_half_lanes()
    out = []
    for b in range(t.shape[1] // LANES):
        blk = t[:, b * LANES:(b + 1) * LANES]
        partner = jnp.where(first_half, pltpu.roll(blk, LANES - HALF_DIM, axis=1),
                            pltpu.roll(blk, HALF_DIM, axis=1))
        out.append(blk * cos + partner * signed_sin)
    return jnp.concatenate(out, axis=1)


def _segment_sum(t, ones_blk):
    return jnp.concatenate(
        [_dot(t[:, g * GROUP:(g + 1) * GROUP].astype(BF16), ones_blk) for g in range(N_GROUPS)], axis=1)


def _group_stats_norm(o, ones_blk, gain):
    mu = _segment_sum(o, ones_blk) * (1.0 / HEAD_DIM)
    d = o - mu
    var = _segment_sum(d * d, ones_blk) * (1.0 / HEAD_DIM)
    return d * lax.rsqrt(var + GN_EPS) * gain


def _ones_block():
    r = _iota((GROUP, GROUP), 0) >> 6
    c = _iota((GROUP, GROUP), 1) >> 6
    return jnp.where(r == c, 1.0, 0.0).astype(BF16)


def _prompt_mixer_kernel(x_ref, meta_ref, g1_ref, win_ref, wconv_ref, rng_ref, wout_ref,
                         lg_ref, inv_ref, wg32_ref, wu32_ref, wd32_ref,
                         x1_ref, convst_ref, retst_ref, wg_ref, wu_ref, wd_ref,
                         cos_ref, sin_ref, qdec_ref, kdec_ref, dmask_ref, ones_ref,
                         smask_ref, sdec_ref, tail_ref, state_ref, ubuf_ref, *, n_tiles):
    j = pl.program_id(0)
    p = pl.program_id(1)
    C = TILE
    k_scale = HEAD_DIM ** -0.5

    def proj(xn, off):
        return _dot(xn, win_ref[:, off:off + 512])

    @pl.when((j == 0) & (p == 0))
    def _init():
        i_f = _iota((C, 1), 0).astype(F32)
        lg = lg_ref[...]
        qdec_ref[...] = jnp.exp((i_f + 1.0) * lg)
        kdec_ref[...] = jnp.exp((C - 1.0 - i_f) * lg) * k_scale
        sdec_ref[...] = jnp.exp(C * lg)
        diff = (_iota((C, C), 0) - _iota((C, C), 1)).astype(F32)
        for h in range(RET_HEADS):
            dmask_ref[h] = jnp.where(
                diff >= 0, jnp.exp(jnp.maximum(diff, 0.0) * float(_LOG_GAMMA[h])) * k_scale, 0.0)
        ones = _ones_block()
        ones_ref[...] = ones
        smask = ones.astype(F32)
        smask_ref[...] = smask

        xm = _rmsnorm(meta_ref[...], g1_ref[...]).astype(BF16)
        um = proj(xm, OFF_C) * proj(xm, OFF_H)
        vm = proj(xm, OFF_V).astype(BF16)
        m_f = _iota((N_META, 1), 0).astype(F32)
        ang = m_f * inv_ref[...]
        sin = jnp.sin(ang)
        kmr = _rope(proj(xm, OFF_K), jnp.cos(ang), jnp.where(_first_half_lanes(), -sin, sin))
        kmd = (kmr * (jnp.exp((N_META - 1.0 - m_f) * lg) * k_scale)).astype(BF16)
        for b in range(tail_ref.shape[0]):
            tail_ref[b] = um[N_META - 2:N_META, :]
            for g in range(N_GROUPS):
                sl = slice(g * GROUP, (g + 1) * GROUP)
                state_ref[b, g] = _dot_tn(kmd[:, sl], vm[:, sl]) * smask

    @pl.when(p == 0)
    def _rope_tables():
        pos = (N_META + j * C + _iota((C, LANES), 0)).astype(F32)
        ang = pos * inv_ref[...]
        sin = jnp.sin(ang)
        cos_ref[...] = jnp.cos(ang)
        sin_ref[...] = jnp.where(_first_half_lanes(), -sin, sin)

    cos = cos_ref[...]
    sin = sin_ref[...]
    head_of_lane = _iota((1, GROUP), 1) >> 6
    smask = smask_ref[...]
    wc = wconv_ref[...]

    def mix_sequence(slot):
        n = p * SEQS_PER_STEP + slot
        x = x_ref[slot]
        xn = _rmsnorm(x, g1_ref[...]).astype(BF16)

        zb = proj(xn, OFF_B)
        u = proj(xn, OFF_C) * proj(xn, OFF_H)
        ubuf_ref[slot, 6:8, :] = tail_ref[n]
        ubuf_ref[slot, 8:8 + C, :] = u
        conv = wc[0:1] * ubuf_ref[slot, 6:6 + C, :] + wc[1:2] * ubuf_ref[slot, 7:7 + C, :] + wc[2:3] * u
        conv_out = (zb * conv).astype(BF16)
        new_tail = u[C - 2:C, :]
        tail_ref[n] = new_tail

        qr = _rope(proj(xn, OFF_Q), cos, sin)
        kr = _rope(proj(xn, OFF_K), cos, sin)
        vb = proj(xn, OFF_V).astype(BF16)
        qb = qr.astype(BF16)
        kb = kr.astype(BF16)
        qd = (qr * qdec_ref[...]).astype(BF16)
        kd = (kr * kdec_ref[...]).astype(BF16)
        o_parts = []
        new_states = []
        for g in range(N_GROUPS):
            sl = slice(g * GROUP, (g + 1) * GROUP)
            state = state_ref[n, g]
            acc = _dot(qd[:, sl], state.astype(BF16))
            for hh in range(HEADS_PER_GROUP):
                k_h = jnp.where(head_of_lane == hh, kb[:, sl], jnp.zeros_like(kb[:, sl]))
                v_h = jnp.where(head_of_lane == hh, vb[:, sl], jnp.zeros_like(vb[:, sl]))
                scores = _dot_nt(qb[:, sl], k_h) * dmask_ref[g * HEADS_PER_GROUP + hh]
                acc = acc + _dot(scores.astype(BF16), v_h)
            new_state = state * sdec_ref[:, sl] + _dot_tn(kd[:, sl], vb[:, sl]) * smask
            state_ref[n, g] = new_state
            new_states.append(new_state)
            o_parts.append(acc)
        o = jnp.concatenate(o_parts, axis=1)
        on = _group_stats_norm(o, ones_ref[...], rng_ref[...])
        ret_out = (_silu(proj(xn, OFF_G)) * on).astype(BF16)

        mix = jnp.concatenate([conv_out, ret_out], axis=1)
        x1_ref[slot] = x + _dot(mix, wout_ref[...])
        return new_tail, new_states

    finals = [mix_sequence(slot) for slot in range(SEQS_PER_STEP)]

    wg_ref[...] = wg32_ref[...].astype(BF16)
    wu_ref[...] = wu32_ref[...].astype(BF16)
    wd_ref[...] = wd32_ref[...].astype(BF16)

    @pl.when(j == n_tiles - 1)
    def _final_states():
        for slot, (new_tail, new_states) in enumerate(finals):
            n = p * SEQS_PER_STEP + slot
            convst_ref[n] = new_tail
            for g in range(N_GROUPS):
                for hh in range(HEADS_PER_GROUP):
                    blk = slice(hh * HEAD_DIM, (hh + 1) * HEAD_DIM)
                    retst_ref[n, g * HEADS_PER_GROUP + hh] = new_states[g][blk, blk]


def _prompt_mixer(x_prompt, meta, g1, w_in, w_conv, rng, w_out, lg_lane, inv_lane, w_gate, w_up, w_down):
    n_seq, seq, _ = x_prompt.shape
    n_tiles = seq // TILE
    pairs = n_seq // SEQS_PER_STEP
    n_steps = n_tiles * pairs
    const2 = lambda j, p: (0, 0)
    full = lambda a: pl.BlockSpec(a.shape, const2, pipeline_mode=pl.Buffered(1))
    gu_rows = w_gate.shape[0] // n_steps
    d_rows = w_down.shape[0] // (n_steps // 2)
    assert gu_rows * n_steps == w_gate.shape[0] and d_rows * (n_steps // 2) == w_down.shape[0]
    assert gu_rows % BF16_ROWS == 0 and d_rows % BF16_ROWS == 0
    gu_slab = pl.BlockSpec((gu_rows, w_gate.shape[1]), lambda j, p: (j * pairs + p, 0))
    d_slab = pl.BlockSpec((d_rows, w_down.shape[1]), lambda j, p: ((j * pairs + p) // 2, 0))
    return pl.pallas_call(
        functools.partial(_prompt_mixer_kernel, n_tiles=n_tiles),
        grid=(n_tiles, pairs),
        in_specs=[pl.BlockSpec((SEQS_PER_STEP, TILE, D_MODEL), lambda j, p: (p, j, 0)),
                  full(meta), full(g1), full(w_in), full(w_conv), full(rng), full(w_out),
                  full(lg_lane), full(inv_lane), gu_slab, gu_slab, d_slab],
        out_specs=[pl.BlockSpec((SEQS_PER_STEP, TILE, D_MODEL), lambda j, p: (p, j, 0)),
                   pl.BlockSpec((n_seq, 2, CONV_DIM), lambda j, p: (0, 0, 0)),
                   pl.BlockSpec((n_seq, RET_HEADS, HEAD_DIM, HEAD_DIM), lambda j, p: (0, 0, 0, 0)),
                   gu_slab, gu_slab, d_slab],
        out_shape=[jax.ShapeDtypeStruct((n_seq, seq, D_MODEL), F32),
                   jax.ShapeDtypeStruct((n_seq, 2, CONV_DIM), F32),
                   jax.ShapeDtypeStruct((n_seq, RET_HEADS, HEAD_DIM, HEAD_DIM), F32),
                   jax.ShapeDtypeStruct(w_gate.shape, BF16), jax.ShapeDtypeStruct(w_up.shape, BF16),
                   jax.ShapeDtypeStruct(w_down.shape, BF16)],
        scratch_shapes=[
            pltpu.VMEM((TILE, LANES), F32),
            pltpu.VMEM((TILE, LANES), F32),
            pltpu.VMEM((TILE, RET_DIM), F32),
            pltpu.VMEM((TILE, RET_DIM), F32),
            pltpu.VMEM((RET_HEADS, TILE, TILE), F32),
            pltpu.VMEM((GROUP, GROUP), BF16),
            pltpu.VMEM((GROUP, GROUP), F32),
            pltpu.VMEM((1, RET_DIM), F32),
            pltpu.VMEM((n_seq, 2, CONV_DIM), F32),
            pltpu.VMEM((n_seq, N_GROUPS, GROUP, GROUP), F32),
            pltpu.VMEM((SEQS_PER_STEP, TILE + 8, CONV_DIM), F32),
        ],
        compiler_params=pltpu.CompilerParams(
            dimension_semantics=("arbitrary", "arbitrary"), vmem_limit_bytes=VMEM_LIMIT),
        name="prompt_mixer",
    )(x_prompt, meta, g1, w_in, w_conv, rng, w_out, lg_lane, inv_lane, w_gate, w_up, w_down)


def _ffn_rows(x, g2, wg_ref, wu_ref, wd_ref, gf):
    xn = _rmsnorm(x, g2).astype(BF16)
    hidden = (_silu(_dot(xn, wg_ref[...])) * _dot(xn, wu_ref[...])).astype(BF16)
    return _rmsnorm(x + _dot(hidden, wd_ref[...]), gf)


def _ffn_kernel(xp_ref, xs_ref, g2_ref, wg_ref, wu_ref, wd_ref, gf_ref, yp_ref, ys_ref, *, prompt_steps):
    i = pl.program_id(0)

    def run(x_ref, y_ref):
        for c in range(x_ref.shape[0] // FFN_CHUNK):
            rows = slice(c * FFN_CHUNK, (c + 1) * FFN_CHUNK)
            y_ref[rows, :] = _ffn_rows(x_ref[rows, :], g2_ref[...], wg_ref, wu_ref, wd_ref, gf_ref[...])

    @pl.when(i < prompt_steps)
    def _prompt_rows():
        run(xp_ref, yp_ref)

    @pl.when(i == prompt_steps)
    def _sample_rows():
        run(xs_ref, ys_ref)


def _ffn(x_p, x_s, g2, w_gate, w_up, w_down, gf):
    prompt_steps = x_p.shape[0] // FFN_TILE
    full = lambda a: pl.BlockSpec(a.shape, lambda i: (0, 0), pipeline_mode=pl.Buffered(1))
    prompt_tile = pl.BlockSpec((FFN_TILE, D_MODEL), lambda i: (jnp.minimum(i, prompt_steps - 1), 0))
    sample_tile = pl.BlockSpec(x_s.shape, lambda i: (0, 0))
    return pl.pallas_call(
        functools.partial(_ffn_kernel, prompt_steps=prompt_steps),
        grid=(prompt_steps + 1,),
        in_specs=[prompt_tile, sample_tile, full(g2), full(w_gate), full(w_up), full(w_down), full(gf)],
        out_specs=[prompt_tile, sample_tile],
        out_shape=[jax.ShapeDtypeStruct(x_p.shape, F32), jax.ShapeDtypeStruct(x_s.shape, F32)],
        compiler_params=pltpu.CompilerParams(
            dimension_semantics=("arbitrary",), vmem_limit_bytes=VMEM_LIMIT),
        name="ffn",
    )(x_p, x_s, g2, w_gate, w_up, w_down, gf)


def _gamma_pow(head, power):
    return float(np.exp(np.float32(power) * _LOG_GAMMA[head]))


def _sample_kernel(x_ref, st_ref, s_ref, g1_ref, win32_ref, wconv_ref, gain_ref, wout32_ref, inv_ref,
                   sdec_ref,
                   x1_ref, convst_ref, snew_ref, win_ref, wout_ref,
                   qd_ref, kd_ref, v_ref, o_ref, zg_ref, cvo_ref, *, n_seq, dec_seq):
    h = pl.program_id(0)
    L, B = dec_seq, n_seq
    k_scale = HEAD_DIM ** -0.5

    @pl.when(h == 0)
    def _dense_front():
        for r in range(0, D_MODEL, CAST_ROWS):
            win_ref[r:r + CAST_ROWS, :] = win32_ref[r:r + CAST_ROWS, :].astype(BF16)
            wout_ref[r:r + CAST_ROWS, :] = wout32_ref[r:r + CAST_ROWS, :].astype(BF16)
        xn = _rmsnorm(x_ref[...], g1_ref[...]).astype(BF16)

        def proj(off):
            return _dot(xn, win_ref[:, off:off + 512])

        zb = proj(OFF_B)
        u = proj(OFF_C) * proj(OFF_H)
        ext = [st_ref[:, :CONV_DIM], st_ref[:, CONV_DIM:]] + [u[i * B:(i + 1) * B] for i in range(L)]
        wc = wconv_ref[...]
        for i in range(L):
            conv = wc[0:1] * ext[i] + wc[1:2] * ext[i + 1] + wc[2:3] * ext[i + 2]
            cvo_ref[i * B:(i + 1) * B, :] = (zb[i * B:(i + 1) * B] * conv).astype(BF16)
        convst_ref[...] = jnp.concatenate([ext[L], ext[L + 1]], axis=1)
        zg_ref[...] = proj(OFF_G)

        q_t = proj(OFF_Q).T
        k_t = proj(OFF_K).T
        v_t = proj(OFF_V).T
        v_ref[...] = v_t
        inv = inv_ref[...]
        cos_sin = [(jnp.cos(float(PAST_LEN + i) * inv), jnp.sin(float(PAST_LEN + i) * inv))
                   for i in range(L)]

        def rope(t, r0, i):
            cos, sin = cos_sin[i]
            t1 = t[r0:r0 + HALF_DIM, i * B:(i + 1) * B]
            t2 = t[r0 + HALF_DIM:r0 + HEAD_DIM, i * B:(i + 1) * B]
            return jnp.concatenate([t1 * cos - t2 * sin, t2 * cos + t1 * sin], axis=0)

        for hd in range(RET_HEADS):
            r0 = hd * HEAD_DIM
            qr = [rope(q_t, r0, i) for i in range(L)]
            kr = [rope(k_t, r0, i) for i in range(L)]
            for i in range(L):
                lanes = slice(i * B, (i + 1) * B)
                qd_ref[r0:r0 + HEAD_DIM, lanes] = qr[i] * _gamma_pow(hd, i + 1)
                kd_ref[r0:r0 + HEAD_DIM, lanes] = kr[i] * (_gamma_pow(hd, L - 1 - i) * k_scale)
                intra = jnp.zeros((HEAD_DIM, B), F32)
                for j in range(i + 1):
                    score = jnp.sum(qr[i] * kr[j], axis=0, keepdims=True) * (_gamma_pow(hd, i - j) * k_scale)
                    intra = intra + score * v_t[r0:r0 + HEAD_DIM, j * B:(j + 1) * B]
                o_ref[r0:r0 + HEAD_DIM, lanes] = intra

    base = pl.multiple_of(h * HEAD_DIM, HEAD_DIM)
    qd_h = qd_ref[pl.ds(base, HEAD_DIM), :]
    kd_h = kd_ref[pl.ds(base, HEAD_DIM), :]
    state_decay = sdec_ref[h]
    for e0 in range(0, HEAD_DIM, E_CHUNK):
        rows = pl.ds(pl.multiple_of(base + e0, E_CHUNK), E_CHUNK)
        v_blk = [v_ref[rows, j * B:(j + 1) * B] for j in range(L)]
        cross = [jnp.zeros((E_CHUNK, B), F32) for _ in range(L)]
        for d in range(HEAD_DIM):
            s_de = s_ref[0, d, e0:e0 + E_CHUNK, :]
            new = s_de * state_decay
            for t in range(L):
                lanes = slice(t * B, (t + 1) * B)
                cross[t] = cross[t] + qd_h[d:d + 1, lanes] * s_de
                new = new + kd_h[d:d + 1, lanes] * v_blk[t]
            snew_ref[0, d, e0:e0 + E_CHUNK, :] = new
        for i in range(L):
            o_ref[rows, i * B:(i + 1) * B] = o_ref[rows, i * B:(i + 1) * B] + cross[i]

    @pl.when(h == RET_HEADS - 1)
    def _dense_back():
        for hd in range(RET_HEADS):
            r0 = hd * HEAD_DIM
            o_h = o_ref[r0:r0 + HEAD_DIM, :]
            mu = jnp.sum(o_h, axis=0, keepdims=True) * (1.0 / HEAD_DIM)
            dlt = o_h - mu
            var = jnp.sum(dlt * dlt, axis=0, keepdims=True) * (1.0 / HEAD_DIM)
            o_ref[r0:r0 + HEAD_DIM, :] = dlt * lax.rsqrt(var + GN_EPS) * gain_ref[r0:r0 + HEAD_DIM, :]
        ret_out = (_silu(zg_ref[...]) * o_ref[...].T).astype(BF16)
        x1_ref[...] = (x_ref[...] + _dot(cvo_ref[...], wout_ref[:CONV_DIM, :])
                       + _dot(ret_out, wout_ref[CONV_DIM:, :]))


def _sample_mixer(xs, st, s4, g1, w_in, w_conv, gain, w_out, inv_col, sdec, dec_seq):
    rows = xs.shape[0]
    n_seq = rows // dec_seq
    full = lambda a: pl.BlockSpec(a.shape, lambda h: (0,) * a.ndim, pipeline_mode=pl.Buffered(1))
    state_spec = pl.BlockSpec((1, HEAD_DIM, HEAD_DIM, n_seq), lambda h: (h, 0, 0, 0))
    consts = (g1, w_in, w_conv, gain, w_out, inv_col, sdec)
    return pl.pallas_call(
        functools.partial(_sample_kernel, n_seq=n_seq, dec_seq=dec_seq),
        grid=(RET_HEADS,),
        in_specs=[full(xs), full(st), state_spec] + [full(a) for a in consts],
        out_specs=[pl.BlockSpec((rows, D_MODEL), lambda h: (0, 0)),
                   pl.BlockSpec(st.shape, lambda h: (0, 0)),
                   state_spec, full(w_in), full(w_out)],
        out_shape=[jax.ShapeDtypeStruct((rows, D_MODEL), F32),
                   jax.ShapeDtypeStruct(st.shape, F32),
                   jax.ShapeDtypeStruct(s4.shape, F32),
                   jax.ShapeDtypeStruct(w_in.shape, BF16), jax.ShapeDtypeStruct(w_out.shape, BF16)],
        scratch_shapes=[
            pltpu.VMEM((RET_DIM, rows), F32),
            pltpu.VMEM((RET_DIM, rows), F32),
            pltpu.VMEM((RET_DIM, rows), F32),
            pltpu.VMEM((RET_DIM, rows), F32),
            pltpu.VMEM((rows, RET_DIM), F32),
            pltpu.VMEM((rows, CONV_DIM), BF16),
        ],
        compiler_params=pltpu.CompilerParams(
            dimension_semantics=("arbitrary",), vmem_limit_bytes=VMEM_LIMIT),
        name="sample_mixer",
    )(xs, st, s4, *consts)


def kernel(x_prompt, x_sample, state_conv, state_ret, meta_tokens, norm1_g, w_in, w_conv, ret_norm_g,
           w_out, norm2_g, w_gate, w_up, w_down, final_norm_g):
    n_p, seq, _ = x_prompt.shape
    n_s, dec_seq, _ = x_sample.shape
    rows_s = n_s * dec_seq
    assert norm1_g.shape[0] == 1 and seq % TILE == 0 and n_p % SEQS_PER_STEP == 0
    assert (n_p * seq) % FFN_TILE == 0
    assert n_s % LANES == 0 and rows_s % FFN_CHUNK == 0

    g1, g2, gf = norm1_g[0][None], norm2_g[0][None], final_norm_g[None]
    rng = ret_norm_g[0][None]
    lg_lane = jnp.asarray(_LOG_GAMMA[np.arange(RET_DIM) // HEAD_DIM][None])
    inv = ROPE_BASE ** (-jnp.arange(HALF_DIM, dtype=F32) / HALF_DIM)
    inv_lane = jnp.tile(inv, LANES // HALF_DIM)[None]

    s4 = jnp.transpose(state_ret[0], (1, 2, 3, 0))
    st = state_conv[0].reshape(n_s, 2 * CONV_DIM)
    gain_col = jnp.broadcast_to(ret_norm_g[0][:, None], (RET_DIM, rows_s))
    inv_col = jnp.broadcast_to(inv[:, None], (HALF_DIM, n_s))
    sdec = jnp.asarray(np.broadcast_to(
        np.exp(np.float32(dec_seq) * _LOG_GAMMA)[:, None, None], (RET_HEADS, E_CHUNK, n_s)))
    xs = jnp.transpose(x_sample, (1, 0, 2)).reshape(rows_s, D_MODEL)
    x1_s, conv_s, s4_new, w_in_b, w_out_b = _sample_mixer(
        xs, st, s4, g1, w_in[0], w_conv[0], gain_col, w_out[0], inv_col, sdec, dec_seq)

    x1_p, conv_p, ret_p, wg_b, wu_b, wd_b = _prompt_mixer(
        x_prompt, meta_tokens, g1, w_in_b, w_conv[0], rng, w_out_b, lg_lane, inv_lane,
        w_gate[0], w_up[0], w_down[0])

    y_p, y_s = _ffn(x1_p.reshape(n_p * seq, D_MODEL), x1_s, g2, wg_b, wu_b, wd_b, gf)
    y_prompt = y_p.reshape(n_p, seq, D_MODEL)
    y_sample = jnp.transpose(y_s.reshape(dec_seq, n_s, D_MODEL), (1, 0, 2))
    ret_s = jnp.transpose(s4_new, (3, 0, 1, 2))

    return (y_prompt, y_sample, conv_p[None], ret_p[None], conv_s.reshape(1, n_s, 2, CONV_DIM), ret_s[None])
```

```python
import functools

import numpy as np
import jax
import jax.numpy as jnp
from jax import lax
from jax.experimental import pallas as pl
from jax.experimental.pallas import tpu as pltpu

D_MODEL = 1024
N_META = 16
CONV_DIM = 512
RET_HEADS = 8
HEAD_DIM = 64
HALF_DIM = HEAD_DIM // 2
RET_DIM = RET_HEADS * HEAD_DIM
D_FF = 2816
PAST_LEN = 16384
ROPE_BASE = 10000.0
EPS = 1e-6
GN_EPS = 1e-5

OFF_B, OFF_C, OFF_H, OFF_Q, OFF_K, OFF_V, OFF_G = (i * 512 for i in range(7))

LANES = 128
GROUP = 256
HEADS_PER_GROUP = GROUP // HEAD_DIM
N_GROUPS = RET_DIM // GROUP
TILE = 256
SEQS_PER_STEP = 4
FFN_TILE = 512
FFN_CHUNK = 256
E_CHUNK = 32
CAST_ROWS = 128
PHASE_LAG = 3
BF16_ROWS = 16
VMEM_LIMIT = 56 * 1024 * 1024

F32 = jnp.float32
BF16 = jnp.bfloat16

_LOG_GAMMA = np.log1p(-(2.0 ** (-5.0 - np.arange(RET_HEADS)))).astype(np.float32)


def _dot(a, b):
    return jnp.dot(a, b, preferred_element_type=F32)


def _dot_nt(a, b):
    return lax.dot_general(a, b, (((1,), (1,)), ((), ())), preferred_element_type=F32)


def _dot_tn(a, b):
    return lax.dot_general(a, b, (((0,), (0,)), ((), ())), preferred_element_type=F32)


def _rmsnorm(x, g):
    ms = jnp.mean(x * x, axis=-1, keepdims=True)
    return x * lax.rsqrt(ms + EPS) * g


def _silu(x):
    return x * jax.nn.sigmoid(x)


def _iota(shape, dim):
    return lax.broadcasted_iota(jnp.int32, shape, dim)


def _first_half_lanes():
    return (_iota((1, LANES), 1) & (HEAD_DIM - 1)) < HALF_DIM


def _rope(t, cos, signed_sin):
    first_half = _first_half_lanes()
    out = []
    for b in range(t.shape[1] // LANES):
        blk = t[:, b * LANES:(b + 1) * LANES]
        partner = jnp.where(first_half, pltpu.roll(blk, LANES - HALF_DIM, axis=1),
                            pltpu.roll(blk, HALF_DIM, axis=1))
        out.append(blk * cos + partner * signed_sin)
    return jnp.concatenate(out, axis=1)


def _ones_block():
    r = _iota((GROUP, GROUP), 0) >> 6
    c = _iota((GROUP, GROUP), 1) >> 6
    return jnp.where(r == c, 1.0, 0.0).astype(BF16)


def _prompt_mixer_kernel(x_ref, meta_ref, g1_ref, win_ref, wvgt_ref, wconv_ref, gain_ref, wout_ref,
                         lg_ref, inv_ref, wg32_ref, wu32_ref, wd32_ref,
                         x1_ref, convst_ref, retst_ref, wg_ref, wu_ref, wd_ref,
                         cos_ref, sin_ref, qdec_ref, kdec_ref, dmask_ref,
                         smask_ref, sdec_ref, tail_ref, state_ref, ubuf_ref, *, n_tiles):
    j = pl.program_id(0)
    p = pl.program_id(1)
    C = TILE
    k_scale = HEAD_DIM ** -0.5

    def proj(xn, off):
        return _dot(xn, win_ref[:, off:off + 512])

    @pl.when((j == 0) & (p == 0))
    def _init():
        i_f = _iota((C, 1), 0).astype(F32)
        lg = lg_ref[...]
        qdec_ref[...] = jnp.exp((i_f + 1.0) * lg)
        kdec_ref[...] = jnp.exp((C - 1.0 - i_f) * lg) * k_scale
        sdec_ref[...] = jnp.exp(C * lg)
        diff = (_iota((C, C), 1) - _iota((C, C), 0)).astype(F32)
        for h in range(RET_HEADS):
            dmask_ref[h] = jnp.where(
                diff >= 0, jnp.exp(jnp.maximum(diff, 0.0) * float(_LOG_GAMMA[h])) * k_scale, 0.0)
        smask = _ones_block().astype(F32)
        smask_ref[...] = smask

        xm = _rmsnorm(meta_ref[...], g1_ref[...]).astype(BF16)
        um = proj(xm, OFF_C) * proj(xm, OFF_H)
        vm = proj(xm, OFF_V).astype(BF16)
        m_f = _iota((N_META, 1), 0).astype(F32)
        ang = m_f * inv_ref[...]
        sin = jnp.sin(ang)
        kmr = _rope(proj(xm, OFF_K), jnp.cos(ang), jnp.where(_first_half_lanes(), -sin, sin))
        kmd = (kmr * (jnp.exp((N_META - 1.0 - m_f) * lg) * k_scale)).astype(BF16)
        for b in range(tail_ref.shape[0]):
            tail_ref[b] = um[N_META - 2:N_META, :]
            for g in range(N_GROUPS):
                sl = slice(g * GROUP, (g + 1) * GROUP)
                state_ref[b, g] = _dot_tn(vm[:, sl], kmd[:, sl]) * smask

    @pl.when(p == 0)
    def _rope_tables():
        pos = (N_META + j * C + _iota((C, LANES), 0)).astype(F32)
        ang = pos * inv_ref[...]
        sin = jnp.sin(ang)
        cos_ref[...] = jnp.cos(ang)
        sin_ref[...] = jnp.where(_first_half_lanes(), -sin, sin)

    cos = cos_ref[...]
    sin = sin_ref[...]
    head_of_lane = _iota((1, GROUP), 1) >> 6
    smask = smask_ref[...]
    wc = wconv_ref[...]

    def mix_sequence(slot):
        n = p * SEQS_PER_STEP + slot
        x = x_ref[slot]
        xn = _rmsnorm(x, g1_ref[...]).astype(BF16)

        zb = proj(xn, OFF_B)
        yield
        u = proj(xn, OFF_C) * proj(xn, OFF_H)
        ubuf_ref[slot, 6:8, :] = tail_ref[n]
        ubuf_ref[slot, 8:8 + C, :] = u
        conv = wc[0:1] * ubuf_ref[slot, 6:6 + C, :] + wc[1:2] * ubuf_ref[slot, 7:7 + C, :] + wc[2:3] * u
        conv_out = (zb * conv).astype(BF16)
        new_tail = u[C - 2:C, :]
        tail_ref[n] = new_tail

        yield

        qr = _rope(proj(xn, OFF_Q), cos, sin)
        yield
        kr = _rope(proj(xn, OFF_K), cos, sin)
        yield
        vg_t = _dot_nt(wvgt_ref[...], xn)
        v_t = vg_t[:RET_DIM].astype(BF16)
        yield
        qb = qr.astype(BF16)
        kb = kr.astype(BF16)
        qd = (qr * qdec_ref[...]).astype(BF16)
        kd = (kr * kdec_ref[...]).astype(BF16)
        o_parts = []
        new_states = []
        for g in range(N_GROUPS):
            sl = slice(g * GROUP, (g + 1) * GROUP)
            state_t = state_ref[n, g]
            cross_t = _dot_nt(state_t.astype(BF16), qd[:, sl])
            k_heads = jnp.concatenate(
                [jnp.where(head_of_lane == hh, kb[:, sl], jnp.zeros_like(kb[:, sl]))
                 for hh in range(HEADS_PER_GROUP)], axis=0)
            decay = dmask_ref[g * HEADS_PER_GROUP:(g + 1) * HEADS_PER_GROUP].reshape(HEADS_PER_GROUP * C, C)
            scores_t = (_dot_nt(k_heads, qb[:, sl]) * decay).astype(BF16)
            for hh in range(HEADS_PER_GROUP):
                h = g * HEADS_PER_GROUP + hh
                rows = slice(hh * HEAD_DIM, (hh + 1) * HEAD_DIM)
                o_parts.append(cross_t[rows] + _dot(v_t[h * HEAD_DIM:(h + 1) * HEAD_DIM],
                                                    scores_t[hh * C:(hh + 1) * C]))
            new_state = state_t * sdec_ref[:, sl] + _dot(v_t[sl], kd[:, sl]) * smask
            state_ref[n, g] = new_state
            new_states.append(new_state)
            yield

        normed = []
        for h, o_h in enumerate(o_parts):
            mu = jnp.sum(o_h, axis=0, keepdims=True) * (1.0 / HEAD_DIM)
            dlt = o_h - mu
            var = jnp.sum(dlt * dlt, axis=0, keepdims=True) * (1.0 / HEAD_DIM)
            normed.append(dlt * lax.rsqrt(var + GN_EPS) * gain_ref[h * HEAD_DIM:(h + 1) * HEAD_DIM, :C])
        ret_out_t = (_silu(vg_t[RET_DIM:]) * jnp.concatenate(normed, axis=0)).astype(BF16)
        yield

        x1_ref[slot] = (x + _dot(conv_out, wout_ref[:CONV_DIM, :])
                        + _dot_tn(ret_out_t, wout_ref[CONV_DIM:, :]))
        finals[slot] = (new_tail, new_states)

    finals = [None] * SEQS_PER_STEP
    phases = [mix_sequence(slot) for slot in range(SEQS_PER_STEP)]
    live = list(range(SEQS_PER_STEP))
    tick = 0
    while live:
        for slot in list(live):
            if tick >= PHASE_LAG * slot:
                try:
                    next(phases[slot])
                except StopIteration:
                    live.remove(slot)
        tick += 1

    wg_ref[...] = wg32_ref[...].astype(BF16)
    wu_ref[...] = wu32_ref[...].astype(BF16)
    wd_ref[...] = wd32_ref[...].astype(BF16)

    @pl.when(j == n_tiles - 1)
    def _final_states():
        for slot, (new_tail, new_states) in enumerate(finals):
            n = p * SEQS_PER_STEP + slot
            convst_ref[n] = new_tail
            for g in range(N_GROUPS):
                state = new_states[g].T
                for hh in range(HEADS_PER_GROUP):
                    blk = slice(hh * HEAD_DIM, (hh + 1) * HEAD_DIM)
                    retst_ref[n, g * HEADS_PER_GROUP + hh] = state[blk, blk]


def _prompt_mixer(x_prompt, meta, g1, w_in, w_vg_t, w_conv, gain_col, w_out, lg_lane, inv_lane,
                  w_gate, w_up, w_down):
    n_seq, seq, _ = x_prompt.shape
    n_tiles = seq // TILE
    pairs = n_seq // SEQS_PER_STEP
    n_steps = n_tiles * pairs
    const2 = lambda j, p: (0, 0)
    full = lambda a: pl.BlockSpec(a.shape, const2, pipeline_mode=pl.Buffered(1))
    gu_rows = w_gate.shape[0] // n_steps
    d_rows = w_down.shape[0] // (n_steps // 2)
    assert gu_rows * n_steps == w_gate.shape[0] and d_rows * (n_steps // 2) == w_down.shape[0]
    assert gu_rows % BF16_ROWS == 0 and d_rows % BF16_ROWS == 0
    gu_slab = pl.BlockSpec((gu_rows, w_gate.shape[1]), lambda j, p: (j * pairs + p, 0))
    d_slab = pl.BlockSpec((d_rows, w_down.shape[1]), lambda j, p: ((j * pairs + p) // 2, 0))
    return pl.pallas_call(
        functools.partial(_prompt_mixer_kernel, n_tiles=n_tiles),
        grid=(n_tiles, pairs),
        in_specs=[pl.BlockSpec((SEQS_PER_STEP, TILE, D_MODEL), lambda j, p: (p, j, 0)),
                  full(meta), full(g1), full(w_in), full(w_vg_t), full(w_conv), full(gain_col),
                  full(w_out), full(lg_lane), full(inv_lane), gu_slab, gu_slab, d_slab],
        out_specs=[pl.BlockSpec((SEQS_PER_STEP, TILE, D_MODEL), lambda j, p: (p, j, 0)),
                   pl.BlockSpec((n_seq, 2, CONV_DIM), lambda j, p: (0, 0, 0)),
                   pl.BlockSpec((n_seq, RET_HEADS, HEAD_DIM, HEAD_DIM), lambda j, p: (0, 0, 0, 0)),
                   gu_slab, gu_slab, d_slab],
        out_shape=[jax.ShapeDtypeStruct((n_seq, seq, D_MODEL), F32),
                   jax.ShapeDtypeStruct((n_seq, 2, CONV_DIM), F32),
                   jax.ShapeDtypeStruct((n_seq, RET_HEADS, HEAD_DIM, HEAD_DIM), F32),
                   jax.ShapeDtypeStruct(w_gate.shape, BF16), jax.ShapeDtypeStruct(w_up.shape, BF16),
                   jax.ShapeDtypeStruct(w_down.shape, BF16)],
        scratch_shapes=[
            pltpu.VMEM((TILE, LANES), F32),
            pltpu.VMEM((TILE, LANES), F32),
            pltpu.VMEM((TILE, RET_DIM), F32),
            pltpu.VMEM((TILE, RET_DIM), F32),
            pltpu.VMEM((RET_HEADS, TILE, TILE), F32),
            pltpu.VMEM((GROUP, GROUP), F32),
            pltpu.VMEM((1, RET_DIM), F32),
            pltpu.VMEM((n_seq, 2, CONV_DIM), F32),
            pltpu.VMEM((n_seq, N_GROUPS, GROUP, GROUP), F32),
            pltpu.VMEM((SEQS_PER_STEP, TILE + 8, CONV_DIM), F32),
        ],
        compiler_params=pltpu.CompilerParams(
            dimension_semantics=("arbitrary", "arbitrary"), vmem_limit_bytes=VMEM_LIMIT),
        name="prompt_mixer",
    )(x_prompt, meta, g1, w_in, w_vg_t, w_conv, gain_col, w_out, lg_lane, inv_lane, w_gate, w_up, w_down)


def _ffn_rows(x, g2, wg_ref, wu_ref, wd_ref, gf):
    xn = _rmsnorm(x, g2).astype(BF16)
    hidden = (_silu(_dot(xn, wg_ref[...])) * _dot(xn, wu_ref[...])).astype(BF16)
    return _rmsnorm(x + _dot(hidden, wd_ref[...]), gf)


def _ffn_kernel(xp_ref, xs_ref, g2_ref, wg_ref, wu_ref, wd_ref, gf_ref, yp_ref, ys_ref, *, prompt_steps):
    i = pl.program_id(0)

    def run(x_ref, y_ref):
        for c in range(x_ref.shape[0] // FFN_CHUNK):
            rows = slice(c * FFN_CHUNK, (c + 1) * FFN_CHUNK)
            y_ref[rows, :] = _ffn_rows(x_ref[rows, :], g2_ref[...], wg_ref, wu_ref, wd_ref, gf_ref[...])

    @pl.when(i < prompt_steps)
    def _prompt_rows():
        run(xp_ref, yp_ref)

    @pl.when(i == prompt_steps)
    def _sample_rows():
        run(xs_ref, ys_ref)


def _ffn(x_p, x_s, g2, w_gate, w_up, w_down, gf):
    prompt_steps = x_p.shape[0] // FFN_TILE
    full = lambda a: pl.BlockSpec(a.shape, lambda i: (0, 0), pipeline_mode=pl.Buffered(1))
    prompt_tile = pl.BlockSpec((FFN_TILE, D_MODEL), lambda i: (jnp.minimum(i, prompt_steps - 1), 0))
    sample_tile = pl.BlockSpec(x_s.shape, lambda i: (0, 0))
    return pl.pallas_call(
        functools.partial(_ffn_kernel, prompt_steps=prompt_steps),
        grid=(prompt_steps + 1,),
        in_specs=[prompt_tile, sample_tile, full(g2), full(w_gate), full(w_up), full(w_down), full(gf)],
        out_specs=[prompt_tile, sample_tile],
        out_shape=[jax.ShapeDtypeStruct(x_p.shape, F32), jax.ShapeDtypeStruct(x_s.shape, F32)],
        compiler_params=pltpu.CompilerParams(
            dimension_semantics=("arbitrary",), vmem_limit_bytes=VMEM_LIMIT),
        name="ffn",
    )(x_p, x_s, g2, w_gate, w_up, w_down, gf)


def _gamma_pow(head, power):
    return float(np.exp(np.float32(power) * _LOG_GAMMA[head]))


def _sample_kernel(x_ref, st_ref, s_ref, g1_ref, win32_ref, wconv_ref, gain_ref, wout32_ref, inv_ref,
                   sdec_ref,
                   x1_ref, convst_ref, snew_ref, win_ref, wout_ref, wvgt_ref,
                   qd_ref, kd_ref, v_ref, o_ref, zg_ref, cvo_ref, *, n_seq, dec_seq):
    h = pl.program_id(0)
    L, B = dec_seq, n_seq
    k_scale = HEAD_DIM ** -0.5

    @pl.when(h == 0)
    def _dense_front():
        for r in range(0, D_MODEL, CAST_ROWS):
            win_ref[r:r + CAST_ROWS, :] = win32_ref[r:r + CAST_ROWS, :].astype(BF16)
            wout_ref[r:r + CAST_ROWS, :] = wout32_ref[r:r + CAST_ROWS, :].astype(BF16)
        for c in range(OFF_V, OFF_V + 2 * RET_DIM, GROUP):
            wvgt_ref[c - OFF_V:c - OFF_V + GROUP, :] = win32_ref[:, c:c + GROUP].T.astype(BF16)
        xn = _rmsnorm(x_ref[...], g1_ref[...]).astype(BF16)

        def proj(off):
            return _dot(xn, win_ref[:, off:off + 512])

        zb = proj(OFF_B)
        u = proj(OFF_C) * proj(OFF_H)
        ext = [st_ref[:, :CONV_DIM], st_ref[:, CONV_DIM:]] + [u[i * B:(i + 1) * B] for i in range(L)]
        wc = wconv_ref[...]
        for i in range(L):
            conv = wc[0:1] * ext[i] + wc[1:2] * ext[i + 1] + wc[2:3] * ext[i + 2]
            cvo_ref[i * B:(i + 1) * B, :] = (zb[i * B:(i + 1) * B] * conv).astype(BF16)
        convst_ref[...] = jnp.concatenate([ext[L], ext[L + 1]], axis=1)
        zg_ref[...] = proj(OFF_G)

        q_t = proj(OFF_Q).T
        k_t = proj(OFF_K).T
        v_t = proj(OFF_V).T
        v_ref[...] = v_t
        inv = inv_ref[...]
        cos_sin = [(jnp.cos(float(PAST_LEN + i) * inv), jnp.sin(float(PAST_LEN + i) * inv))
                   for i in range(L)]

        def rope(t, r0, i):
            cos, sin = cos_sin[i]
            t1 = t[r0:r0 + HALF_DIM, i * B:(i + 1) * B]
            t2 = t[r0 + HALF_DIM:r0 + HEAD_DIM, i * B:(i + 1) * B]
            return jnp.concatenate([t1 * cos - t2 * sin, t2 * cos + t1 * sin], axis=0)

        for hd in range(RET_HEADS):
            r0 = hd * HEAD_DIM
            qr = [rope(q_t, r0, i) for i in range(L)]
            kr = [rope(k_t, r0, i) for i in range(L)]
            for i in range(L):
                lanes = slice(i * B, (i + 1) * B)
                qd_ref[r0:r0 + HEAD_DIM, lanes] = qr[i] * _gamma_pow(hd, i + 1)
                kd_ref[r0:r0 + HEAD_DIM, lanes] = kr[i] * (_gamma_pow(hd, L - 1 - i) * k_scale)
                intra = jnp.zeros((HEAD_DIM, B), F32)
                for j in range(i + 1):
                    score = jnp.sum(qr[i] * kr[j], axis=0, keepdims=True) * (_gamma_pow(hd, i - j) * k_scale)
                    intra = intra + score * v_t[r0:r0 + HEAD_DIM, j * B:(j + 1) * B]
                o_ref[r0:r0 + HEAD_DIM, lanes] = intra

    base = pl.multiple_of(h * HEAD_DIM, HEAD_DIM)
    qd_h = qd_ref[pl.ds(base, HEAD_DIM), :]
    kd_h = kd_ref[pl.ds(base, HEAD_DIM), :]
    state_decay = sdec_ref[h]
    for e0 in range(0, HEAD_DIM, E_CHUNK):
        rows = pl.ds(pl.multiple_of(base + e0, E_CHUNK), E_CHUNK)
        v_blk = [v_ref[rows, j * B:(j + 1) * B] for j in range(L)]
        cross = [jnp.zeros((E_CHUNK, B), F32) for _ in range(L)]
        for d in range(HEAD_DIM):
            s_de = s_ref[0, d, e0:e0 + E_CHUNK, :]
            new = s_de * state_decay
            for t in range(L):
                lanes = slice(t * B, (t + 1) * B)
                cross[t] = cross[t] + qd_h[d:d + 1, lanes] * s_de
                new = new + kd_h[d:d + 1, lanes] * v_blk[t]
            snew_ref[0, d, e0:e0 + E_CHUNK, :] = new
        for i in range(L):
            o_ref[rows, i * B:(i + 1) * B] = o_ref[rows, i * B:(i + 1) * B] + cross[i]

    @pl.when(h == RET_HEADS - 1)
    def _dense_back():
        for hd in range(RET_HEADS):
            r0 = hd * HEAD_DIM
            o_h = o_ref[r0:r0 + HEAD_DIM, :]
            mu = jnp.sum(o_h, axis=0, keepdims=True) * (1.0 / HEAD_DIM)
            dlt = o_h - mu
            var = jnp.sum(dlt * dlt, axis=0, keepdims=True) * (1.0 / HEAD_DIM)
            o_ref[r0:r0 + HEAD_DIM, :] = dlt * lax.rsqrt(var + GN_EPS) * gain_ref[r0:r0 + HEAD_DIM, :]
        ret_out = (_silu(zg_ref[...]) * o_ref[...].T).astype(BF16)
        x1_ref[...] = (x_ref[...] + _dot(cvo_ref[...], wout_ref[:CONV_DIM, :])
                       + _dot(ret_out, wout_ref[CONV_DIM:, :]))


def _sample_mixer(xs, st, s4, g1, w_in, w_conv, gain, w_out, inv_col, sdec, dec_seq):
    rows = xs.shape[0]
    n_seq = rows // dec_seq
    full = lambda a: pl.BlockSpec(a.shape, lambda h: (0,) * a.ndim, pipeline_mode=pl.Buffered(1))
    state_spec = pl.BlockSpec((1, HEAD_DIM, HEAD_DIM, n_seq), lambda h: (h, 0, 0, 0))
    consts = (g1, w_in, w_conv, gain, w_out, inv_col, sdec)
    return pl.pallas_call(
        functools.partial(_sample_kernel, n_seq=n_seq, dec_seq=dec_seq),
        grid=(RET_HEADS,),
        in_specs=[full(xs), full(st), state_spec] + [full(a) for a in consts],
        out_specs=[pl.BlockSpec((rows, D_MODEL), lambda h: (0, 0)),
                   pl.BlockSpec(st.shape, lambda h: (0, 0)),
                   state_spec, full(w_in), full(w_out),
                   pl.BlockSpec((2 * RET_DIM, D_MODEL), lambda h: (0, 0), pipeline_mode=pl.Buffered(1))],
        out_shape=[jax.ShapeDtypeStruct((rows, D_MODEL), F32),
                   jax.ShapeDtypeStruct(st.shape, F32),
                   jax.ShapeDtypeStruct(s4.shape, F32),
                   jax.ShapeDtypeStruct(w_in.shape, BF16), jax.ShapeDtypeStruct(w_out.shape, BF16),
                   jax.ShapeDtypeStruct((2 * RET_DIM, D_MODEL), BF16)],
        scratch_shapes=[
            pltpu.VMEM((RET_DIM, rows), F32),
            pltpu.VMEM((RET_DIM, rows), F32),
            pltpu.VMEM((RET_DIM, rows), F32),
            pltpu.VMEM((RET_DIM, rows), F32),
            pltpu.VMEM((rows, RET_DIM), F32),
            pltpu.VMEM((rows, CONV_DIM), BF16),
        ],
        compiler_params=pltpu.CompilerParams(
            dimension_semantics=("arbitrary",), vmem_limit_bytes=VMEM_LIMIT),
        name="sample_mixer",
    )(xs, st, s4, *consts)


def kernel(x_prompt, x_sample, state_conv, state_ret, meta_tokens, norm1_g, w_in, w_conv, ret_norm_g,
           w_out, norm2_g, w_gate, w_up, w_down, final_norm_g):
    n_p, seq, _ = x_prompt.shape
    n_s, dec_seq, _ = x_sample.shape
    rows_s = n_s * dec_seq
    assert norm1_g.shape[0] == 1 and seq % TILE == 0 and n_p % SEQS_PER_STEP == 0
    assert (n_p * seq) % FFN_TILE == 0
    assert n_s % LANES == 0 and rows_s % FFN_CHUNK == 0

    g1, g2, gf = norm1_g[0][None], norm2_g[0][None], final_norm_g[None]
    lg_lane = jnp.asarray(_LOG_GAMMA[np.arange(RET_DIM) // HEAD_DIM][None])
    inv = ROPE_BASE ** (-jnp.arange(HALF_DIM, dtype=F32) / HALF_DIM)
    inv_lane = jnp.tile(inv, LANES // HALF_DIM)[None]

    s4 = jnp.transpose(state_ret[0], (1, 2, 3, 0))
    st = state_conv[0].reshape(n_s, 2 * CONV_DIM)
    gain_col = jnp.broadcast_to(ret_norm_g[0][:, None], (RET_DIM, rows_s))
    inv_col = jnp.broadcast_to(inv[:, None], (HALF_DIM, n_s))
    sdec = jnp.asarray(np.broadcast_to(
        np.exp(np.float32(dec_seq) * _LOG_GAMMA)[:, None, None], (RET_HEADS, E_CHUNK, n_s)))
    xs = jnp.transpose(x_sample, (1, 0, 2)).reshape(rows_s, D_MODEL)
    x1_s, conv_s, s4_new, w_in_b, w_out_b, w_vg_t = _sample_mixer(
        xs, st, s4, g1, w_in[0], w_conv[0], gain_col, w_out[0], inv_col, sdec, dec_seq)

    x1_p, conv_p, ret_p, wg_b, wu_b, wd_b = _prompt_mixer(
        x_prompt, meta_tokens, g1, w_in_b, w_vg_t, w_conv[0], gain_col, w_out_b, lg_lane, inv_lane,
        w_gate[0], w_up[0], w_down[0])

    y_p, y_s = _ffn(x1_p.reshape(n_p * seq, D_MODEL), x1_s, g2, wg_b, wu_b, wd_b, gf)
    y_prompt = y_p.reshape(n_p, seq, D_MODEL)
    y_sample = jnp.transpose(y_s.reshape(dec_seq, n_s, D_MODEL), (1, 0, 2))
    ret_s = jnp.transpose(s4_new, (3, 0, 1, 2))

    return (y_prompt, y_sample, conv_p[None], ret_p[None], conv_s.reshape(1, n_s, 2, CONV_DIM), ret_s[None])
```

```python
import functools

import numpy as np
import jax
import jax.numpy as jnp
from jax import lax
from jax.experimental import pallas as pl
from jax.experimental.pallas import tpu as pltpu

D_MODEL = 1024
N_META = 16
CONV_DIM = 512
RET_HEADS = 8
HEAD_DIM = 64
HALF_DIM = HEAD_DIM // 2
RET_DIM = RET_HEADS * HEAD_DIM
D_FF = 2816
PAST_LEN = 16384
ROPE_BASE = 10000.0
EPS = 1e-6
GN_EPS = 1e-5

OFF_B, OFF_C, OFF_H, OFF_Q, OFF_K, OFF_V, OFF_G = (i * 512 for i in range(7))

LANES = 128
GROUP = 256
HEADS_PER_GROUP = GROUP // HEAD_DIM
N_GROUPS = RET_DIM // GROUP
TILE = 256
SEQS_PER_STEP = 4
FFN_TILE = 512
FFN_CHUNK = 256
CAST_ROWS = 128
PHASE_LAG = 3
FFN_PHASE_LAG = 1
BF16_ROWS = 16
VMEM_LIMIT = 56 * 1024 * 1024

F32 = jnp.float32
BF16 = jnp.bfloat16

_LOG_GAMMA = np.log1p(-(2.0 ** (-5.0 - np.arange(RET_HEADS)))).astype(np.float32)


def _dot(a, b):
    return jnp.dot(a, b, preferred_element_type=F32)


def _dot_nt(a, b):
    return lax.dot_general(a, b, (((1,), (1,)), ((), ())), preferred_element_type=F32)


def _dot_tn(a, b):
    return lax.dot_general(a, b, (((0,), (0,)), ((), ())), preferred_element_type=F32)


def _rmsnorm(x, g):
    ms = jnp.mean(x * x, axis=-1, keepdims=True)
    return x * lax.rsqrt(ms + EPS) * g


def _silu(x):
    return x * jax.nn.sigmoid(x)


def _iota(shape, dim):
    return lax.broadcasted_iota(jnp.int32, shape, dim)


def _first_half_lanes():
    return (_iota((1, LANES), 1) & (HEAD_DIM - 1)) < HALF_DIM


def _rope(t, cos, signed_sin):
    first_half = _first_half_lanes()
    out = []
    for b in range(t.shape[1] // LANES):
        blk = t[:, b * LANES:(b + 1) * LANES]
        partner = jnp.where(first_half, pltpu.roll(blk, LANES - HALF_DIM, axis=1),
                            pltpu.roll(blk, HALF_DIM, axis=1))
        out.append(blk * cos + partner * signed_sin)
    return jnp.concatenate(out, axis=1)


def _ones_block():
    r = _iota((GROUP, GROUP), 0) >> 6
    c = _iota((GROUP, GROUP), 1) >> 6
    return jnp.where(r == c, 1.0, 0.0).astype(BF16)


def _trace_staggered(phase_iters, lag):
    live = list(range(len(phase_iters)))
    tick = 0
    while live:
        for k in list(live):
            if tick >= lag * k:
                try:
                    next(phase_iters[k])
                except StopIteration:
                    live.remove(k)
        tick += 1


def _prompt_mixer_kernel(x_ref, meta_ref, g1_ref, win_ref, wvgt_ref, wconv_ref, gain_ref, wout_ref,
                         lg_ref, inv_ref, wg32_ref, wu32_ref, wd32_ref,
                         x1_ref, convst_ref, retst_ref, wg_ref, wu_ref, wd_ref,
                         cos_ref, sin_ref, qdec_ref, kdec_ref, dmask_ref,
                         smask_ref, sdec_ref, tail_ref, state_ref, ubuf_ref, *, n_tiles):
    j = pl.program_id(0)
    p = pl.program_id(1)
    C = TILE
    k_scale = HEAD_DIM ** -0.5

    def proj(xn, off):
        return _dot(xn, win_ref[:, off:off + 512])

    @pl.when((j == 0) & (p == 0))
    def _init():
        i_f = _iota((C, 1), 0).astype(F32)
        lg = lg_ref[...]
        qdec_ref[...] = jnp.exp((i_f + 1.0) * lg)
        kdec_ref[...] = jnp.exp((C - 1.0 - i_f) * lg) * k_scale
        sdec_ref[...] = jnp.exp(C * lg)
        diff = (_iota((C, C), 1) - _iota((C, C), 0)).astype(F32)
        for h in range(RET_HEADS):
            dmask_ref[h] = jnp.where(
                diff >= 0, jnp.exp(jnp.maximum(diff, 0.0) * float(_LOG_GAMMA[h])) * k_scale, 0.0)
        smask = _ones_block().astype(F32)
        smask_ref[...] = smask

        xm = _rmsnorm(meta_ref[...], g1_ref[...]).astype(BF16)
        um = proj(xm, OFF_C) * proj(xm, OFF_H)
        vm = proj(xm, OFF_V).astype(BF16)
        m_f = _iota((N_META, 1), 0).astype(F32)
        ang = m_f * inv_ref[...]
        sin = jnp.sin(ang)
        kmr = _rope(proj(xm, OFF_K), jnp.cos(ang), jnp.where(_first_half_lanes(), -sin, sin))
        kmd = (kmr * (jnp.exp((N_META - 1.0 - m_f) * lg) * k_scale)).astype(BF16)
        for b in range(tail_ref.shape[0]):
            tail_ref[b] = um[N_META - 2:N_META, :]
            for g in range(N_GROUPS):
                sl = slice(g * GROUP, (g + 1) * GROUP)
                state_ref[b, g] = _dot_tn(vm[:, sl], kmd[:, sl]) * smask

    @pl.when(p == 0)
    def _rope_tables():
        pos = (N_META + j * C + _iota((C, LANES), 0)).astype(F32)
        ang = pos * inv_ref[...]
        sin = jnp.sin(ang)
        cos_ref[...] = jnp.cos(ang)
        sin_ref[...] = jnp.where(_first_half_lanes(), -sin, sin)

    cos = cos_ref[...]
    sin = sin_ref[...]
    head_of_lane = _iota((1, GROUP), 1) >> 6
    smask = smask_ref[...]
    wc = wconv_ref[...]

    def mix_sequence(slot):
        n = p * SEQS_PER_STEP + slot
        x = x_ref[slot]
        xn = _rmsnorm(x, g1_ref[...]).astype(BF16)

        zb = proj(xn, OFF_B)
        yield
        u = proj(xn, OFF_C) * proj(xn, OFF_H)
        ubuf_ref[slot, 6:8, :] = tail_ref[n]
        ubuf_ref[slot, 8:8 + C, :] = u
        conv = wc[0:1] * ubuf_ref[slot, 6:6 + C, :] + wc[1:2] * ubuf_ref[slot, 7:7 + C, :] + wc[2:3] * u
        conv_out = (zb * conv).astype(BF16)
        new_tail = u[C - 2:C, :]
        tail_ref[n] = new_tail

        yield

        qr = _rope(proj(xn, OFF_Q), cos, sin)
        yield
        kr = _rope(proj(xn, OFF_K), cos, sin)
        yield
        vg_t = _dot_nt(wvgt_ref[...], xn)
        v_t = vg_t[:RET_DIM].astype(BF16)
        yield
        qb = qr.astype(BF16)
        kb = kr.astype(BF16)
        qd = (qr * qdec_ref[...]).astype(BF16)
        kd = (kr * kdec_ref[...]).astype(BF16)
        o_parts = []
        new_states = []
        for g in range(N_GROUPS):
            sl = slice(g * GROUP, (g + 1) * GROUP)
            state_t = state_ref[n, g]
            cross_t = _dot_nt(state_t.astype(BF16), qd[:, sl])
            k_heads = jnp.concatenate(
                [jnp.where(head_of_lane == hh, kb[:, sl], jnp.zeros_like(kb[:, sl]))
                 for hh in range(HEADS_PER_GROUP)], axis=0)
            decay = dmask_ref[g * HEADS_PER_GROUP:(g + 1) * HEADS_PER_GROUP].reshape(HEADS_PER_GROUP * C, C)
            scores_t = (_dot_nt(k_heads, qb[:, sl]) * decay).astype(BF16)
            for hh in range(HEADS_PER_GROUP):
                h = g * HEADS_PER_GROUP + hh
                rows = slice(hh * HEAD_DIM, (hh + 1) * HEAD_DIM)
                o_parts.append(cross_t[rows] + _dot(v_t[h * HEAD_DIM:(h + 1) * HEAD_DIM],
                                                    scores_t[hh * C:(hh + 1) * C]))
            new_state = state_t * sdec_ref[:, sl] + _dot(v_t[sl], kd[:, sl]) * smask
            state_ref[n, g] = new_state
            new_states.append(new_state)
            yield

        normed = []
        for h, o_h in enumerate(o_parts):
            mu = jnp.sum(o_h, axis=0, keepdims=True) * (1.0 / HEAD_DIM)
            dlt = o_h - mu
            var = jnp.sum(dlt * dlt, axis=0, keepdims=True) * (1.0 / HEAD_DIM)
            normed.append(dlt * lax.rsqrt(var + GN_EPS) * gain_ref[h * HEAD_DIM:(h + 1) * HEAD_DIM, :C])
        ret_out_t = (_silu(vg_t[RET_DIM:]) * jnp.concatenate(normed, axis=0)).astype(BF16)
        yield

        x1_ref[slot] = (x + _dot(conv_out, wout_ref[:CONV_DIM, :])
                        + _dot_tn(ret_out_t, wout_ref[CONV_DIM:, :]))
        finals[slot] = (new_tail, new_states)

    finals = [None] * SEQS_PER_STEP
    _trace_staggered([mix_sequence(slot) for slot in range(SEQS_PER_STEP)], PHASE_LAG)

    wg_ref[...] = wg32_ref[...].astype(BF16)
    wu_ref[...] = wu32_ref[...].astype(BF16)
    wd_ref[...] = wd32_ref[...].astype(BF16)

    @pl.when(j == n_tiles - 1)
    def _final_states():
        for slot, (new_tail, new_states) in enumerate(finals):
            n = p * SEQS_PER_STEP + slot
            convst_ref[n] = new_tail
            for g in range(N_GROUPS):
                state = new_states[g].T
                for hh in range(HEADS_PER_GROUP):
                    blk = slice(hh * HEAD_DIM, (hh + 1) * HEAD_DIM)
                    retst_ref[n, g * HEADS_PER_GROUP + hh] = state[blk, blk]


def _prompt_mixer(x_prompt, meta, g1, w_in, w_vg_t, w_conv, gain_col, w_out, lg_lane, inv_lane,
                  w_gate, w_up, w_down):
    n_seq, seq, _ = x_prompt.shape
    n_tiles = seq // TILE
    pairs = n_seq // SEQS_PER_STEP
    n_steps = n_tiles * pairs
    const2 = lambda j, p: (0, 0)
    full = lambda a: pl.BlockSpec(a.shape, const2, pipeline_mode=pl.Buffered(1))
    gu_rows = w_gate.shape[0] // n_steps
    d_rows = w_down.shape[0] // (n_steps // 2)
    assert gu_rows * n_steps == w_gate.shape[0] and d_rows * (n_steps // 2) == w_down.shape[0]
    assert gu_rows % BF16_ROWS == 0 and d_rows % BF16_ROWS == 0
    gu_slab = pl.BlockSpec((gu_rows, w_gate.shape[1]), lambda j, p: (j * pairs + p, 0))
    d_slab = pl.BlockSpec((d_rows, w_down.shape[1]), lambda j, p: ((j * pairs + p) // 2, 0))
    return pl.pallas_call(
        functools.partial(_prompt_mixer_kernel, n_tiles=n_tiles),
        grid=(n_tiles, pairs),
        in_specs=[pl.BlockSpec((SEQS_PER_STEP, TILE, D_MODEL), lambda j, p: (p, j, 0)),
                  full(meta), full(g1), full(w_in), full(w_vg_t), full(w_conv), full(gain_col),
                  full(w_out), full(lg_lane), full(inv_lane), gu_slab, gu_slab, d_slab],
        out_specs=[pl.BlockSpec((SEQS_PER_STEP, TILE, D_MODEL), lambda j, p: (p, j, 0)),
                   pl.BlockSpec((n_seq, 2, CONV_DIM), lambda j, p: (0, 0, 0)),
                   pl.BlockSpec((n_seq, RET_HEADS, HEAD_DIM, HEAD_DIM), lambda j, p: (0, 0, 0, 0)),
                   gu_slab, gu_slab, d_slab],
        out_shape=[jax.ShapeDtypeStruct((n_seq, seq, D_MODEL), F32),
                   jax.ShapeDtypeStruct((n_seq, 2, CONV_DIM), F32),
                   jax.ShapeDtypeStruct((n_seq, RET_HEADS, HEAD_DIM, HEAD_DIM), F32),
                   jax.ShapeDtypeStruct(w_gate.shape, BF16), jax.ShapeDtypeStruct(w_up.shape, BF16),
                   jax.ShapeDtypeStruct(w_down.shape, BF16)],
        scratch_shapes=[
            pltpu.VMEM((TILE, LANES), F32),
            pltpu.VMEM((TILE, LANES), F32),
            pltpu.VMEM((TILE, RET_DIM), F32),
            pltpu.VMEM((TILE, RET_DIM), F32),
            pltpu.VMEM((RET_HEADS, TILE, TILE), F32),
            pltpu.VMEM((GROUP, GROUP), F32),
            pltpu.VMEM((1, RET_DIM), F32),
            pltpu.VMEM((n_seq, 2, CONV_DIM), F32),
            pltpu.VMEM((n_seq, N_GROUPS, GROUP, GROUP), F32),
            pltpu.VMEM((SEQS_PER_STEP, TILE + 8, CONV_DIM), F32),
        ],
        compiler_params=pltpu.CompilerParams(
            dimension_semantics=("arbitrary", "arbitrary"), vmem_limit_bytes=VMEM_LIMIT),
        name="prompt_mixer",
    )(x_prompt, meta, g1, w_in, w_vg_t, w_conv, gain_col, w_out, lg_lane, inv_lane, w_gate, w_up, w_down)


def _ffn_kernel(xp_ref, xs_ref, qd_ref, kd_ref, v_ref, ointra_ref, zg_ref, cvo_ref, s_ref, gain_ref,
                sdec_ref, wout_ref, g2_ref, wg_ref, wu_ref, wd_ref, gf_ref,
                yp_ref, ys_ref, snew_ref, o_ref, *, prompt_steps, n_seq, dec_seq, e_piece):
    i = pl.program_id(0)
    L, B = dec_seq, n_seq
    pieces_per_head = HEAD_DIM // e_piece

    def chunk_phases(load, y_ref, rows):
        x = load(rows)
        xn = _rmsnorm(x, g2_ref[...]).astype(BF16)
        yield
        gate = _dot(xn, wg_ref[...])
        yield
        up = _dot(xn, wu_ref[...])
        yield
        hidden = (_silu(gate) * up).astype(BF16)
        yield
        y_ref[rows, :] = _rmsnorm(x + _dot(hidden, wd_ref[...]), gf_ref[...])

    def ffn_chunks(load, y_ref, extra=()):
        chunks = [slice(r, r + FFN_CHUNK) for r in range(0, y_ref.shape[0], FFN_CHUNK)]
        _trace_staggered([chunk_phases(load, y_ref, rows) for rows in chunks] + list(extra), FFN_PHASE_LAG)

    def state_piece():
        h = i // pieces_per_head
        base = pl.multiple_of(h * HEAD_DIM, HEAD_DIM)
        rows = pl.ds(pl.multiple_of(base + (i % pieces_per_head) * e_piece, e_piece), e_piece)
        qd_h = qd_ref[pl.ds(base, HEAD_DIM), :]
        kd_h = kd_ref[pl.ds(base, HEAD_DIM), :]
        state_decay = sdec_ref[h]
        v_blk = [v_ref[rows, j * B:(j + 1) * B] for j in range(L)]
        cross = [jnp.zeros((e_piece, B), F32) for _ in range(L)]
        for d in range(HEAD_DIM):
            s_de = s_ref[0, d]
            new = s_de * state_decay
            for t in range(L):
                lanes = slice(t * B, (t + 1) * B)
                cross[t] = cross[t] + qd_h[d:d + 1, lanes] * s_de
                new = new + kd_h[d:d + 1, lanes] * v_blk[t]
            snew_ref[0, d] = new
        for t in range(L):
            o_ref[rows, t * B:(t + 1) * B] = o_ref[rows, t * B:(t + 1) * B] + cross[t]
        yield

    @pl.when(i == 0)
    def _seed_retention_output():
        o_ref[...] = ointra_ref[...]

    @pl.when(i < prompt_steps)
    def _prompt_rows():
        ffn_chunks(lambda rows: xp_ref[rows, :], yp_ref, extra=[state_piece()])

    @pl.when(i == prompt_steps)
    def _sample_rows():
        for hd in range(RET_HEADS):
            r0 = hd * HEAD_DIM
            o_h = o_ref[r0:r0 + HEAD_DIM, :]
            mu = jnp.sum(o_h, axis=0, keepdims=True) * (1.0 / HEAD_DIM)
            dlt = o_h - mu
            var = jnp.sum(dlt * dlt, axis=0, keepdims=True) * (1.0 / HEAD_DIM)
            o_ref[r0:r0 + HEAD_DIM, :] = dlt * lax.rsqrt(var + GN_EPS) * gain_ref[r0:r0 + HEAD_DIM, :]
        ret_out = (_silu(zg_ref[...]) * o_ref[...].T).astype(BF16)
        x1 = (xs_ref[...] + _dot(cvo_ref[...], wout_ref[:CONV_DIM, :])
              + _dot(ret_out, wout_ref[CONV_DIM:, :]))
        ffn_chunks(lambda rows: x1[rows], ys_ref)


def _ffn(x_p, xs, qd_t, kd_t, v_t, o_intra, zg, cvo, s4, gain_col, w_out_b, g2, w_gate, w_up, w_down, gf,
         dec_seq):
    prompt_steps = x_p.shape[0] // FFN_TILE
    n_seq = xs.shape[0] // dec_seq
    assert prompt_steps % RET_HEADS == 0
    e_piece = HEAD_DIM // (prompt_steps // RET_HEADS)
    assert e_piece % 8 == 0 and HEAD_DIM % e_piece == 0
    pieces_per_head = HEAD_DIM // e_piece
    sdec = jnp.asarray(np.broadcast_to(
        np.exp(np.float32(dec_seq) * _LOG_GAMMA)[:, None, None], (RET_HEADS, e_piece, n_seq)))
    full = lambda a: pl.BlockSpec(a.shape, lambda i: (0,) * a.ndim, pipeline_mode=pl.Buffered(1))
    prompt_tile = pl.BlockSpec((FFN_TILE, D_MODEL), lambda i: (jnp.minimum(i, prompt_steps - 1), 0))
    sample_out = pl.BlockSpec(xs.shape, lambda i: (0, 0))

    def piece_index(i):
        piece = jnp.minimum(i, prompt_steps - 1)
        return (piece // pieces_per_head, 0, piece % pieces_per_head, 0)

    state_piece = pl.BlockSpec((1, HEAD_DIM, e_piece, n_seq), piece_index)
    return pl.pallas_call(
        functools.partial(_ffn_kernel, prompt_steps=prompt_steps, n_seq=n_seq, dec_seq=dec_seq,
                          e_piece=e_piece),
        grid=(prompt_steps + 1,),
        in_specs=[prompt_tile, full(xs), full(qd_t), full(kd_t), full(v_t), full(o_intra), full(zg),
                  full(cvo), state_piece, full(gain_col), full(sdec), full(w_out_b), full(g2),
                  full(w_gate), full(w_up), full(w_down), full(gf)],
        out_specs=[prompt_tile, sample_out, state_piece],
        out_shape=[jax.ShapeDtypeStruct(x_p.shape, F32), jax.ShapeDtypeStruct(xs.shape, F32),
                   jax.ShapeDtypeStruct(s4.shape, F32)],
        scratch_shapes=[pltpu.VMEM(o_intra.shape, F32)],
        compiler_params=pltpu.CompilerParams(
            dimension_semantics=("arbitrary",), vmem_limit_bytes=VMEM_LIMIT),
        name="ffn",
    )(x_p, xs, qd_t, kd_t, v_t, o_intra, zg, cvo, s4, gain_col, sdec, w_out_b, g2, w_gate, w_up, w_down, gf)


def _gamma_pow(head, power):
    return float(np.exp(np.float32(power) * _LOG_GAMMA[head]))


def _sample_front_kernel(x_ref, st_ref, g1_ref, win32_ref, wvg32_ref, wconv_ref, wout32_ref, inv_ref,
                         convst_ref, qd_ref, kd_ref, v_ref, o_ref, zg_ref, cvo_ref,
                         win_ref, wout_ref, wvgt_ref, *, n_seq, dec_seq):
    s = pl.program_id(0)
    L, B = dec_seq, n_seq
    k_scale = HEAD_DIM ** -0.5

    slab = pl.ds(pl.multiple_of(s * CAST_ROWS, CAST_ROWS), CAST_ROWS)
    win_ref[...] = win32_ref[slab, :].astype(BF16)
    wout_ref[...] = wout32_ref[slab, :].astype(BF16)
    wvgt_ref[...] = wvg32_ref[...].T.astype(BF16)

    @pl.when(s == 0)
    def _dense_front():
        xn = _rmsnorm(x_ref[...], g1_ref[...]).astype(BF16)

        def proj(off):
            return _dot(xn, win32_ref[:, off:off + 512].astype(BF16))

        zb = proj(OFF_B)
        u = proj(OFF_C) * proj(OFF_H)
        ext = [st_ref[:, :CONV_DIM], st_ref[:, CONV_DIM:]] + [u[i * B:(i + 1) * B] for i in range(L)]
        wc = wconv_ref[...]
        for i in range(L):
            conv = wc[0:1] * ext[i] + wc[1:2] * ext[i + 1] + wc[2:3] * ext[i + 2]
            cvo_ref[i * B:(i + 1) * B, :] = (zb[i * B:(i + 1) * B] * conv).astype(BF16)
        convst_ref[...] = jnp.concatenate([ext[L], ext[L + 1]], axis=1)
        zg_ref[...] = proj(OFF_G)

        q_t = proj(OFF_Q).T
        k_t = proj(OFF_K).T
        v_t = proj(OFF_V).T
        v_ref[...] = v_t
        inv = inv_ref[...]
        cos_sin = [(jnp.cos(float(PAST_LEN + i) * inv), jnp.sin(float(PAST_LEN + i) * inv))
                   for i in range(L)]

        def rope(t, r0, i):
            cos, sin = cos_sin[i]
            t1 = t[r0:r0 + HALF_DIM, i * B:(i + 1) * B]
            t2 = t[r0 + HALF_DIM:r0 + HEAD_DIM, i * B:(i + 1) * B]
            return jnp.concatenate([t1 * cos - t2 * sin, t2 * cos + t1 * sin], axis=0)

        for hd in range(RET_HEADS):
            r0 = hd * HEAD_DIM
            qr = [rope(q_t, r0, i) for i in range(L)]
            kr = [rope(k_t, r0, i) for i in range(L)]
            for i in range(L):
                lanes = slice(i * B, (i + 1) * B)
                qd_ref[r0:r0 + HEAD_DIM, lanes] = qr[i] * _gamma_pow(hd, i + 1)
                kd_ref[r0:r0 + HEAD_DIM, lanes] = kr[i] * (_gamma_pow(hd, L - 1 - i) * k_scale)
                intra = jnp.zeros((HEAD_DIM, B), F32)
                for j in range(i + 1):
                    score = jnp.sum(qr[i] * kr[j], axis=0, keepdims=True) * (_gamma_pow(hd, i - j) * k_scale)
                    intra = intra + score * v_t[r0:r0 + HEAD_DIM, j * B:(j + 1) * B]
                o_ref[r0:r0 + HEAD_DIM, lanes] = intra


def _sample_front(xs, st, g1, w_in, w_conv, w_out, inv_col, dec_seq):
    rows = xs.shape[0]
    n_seq = rows // dec_seq
    n_steps = w_in.shape[0] // CAST_ROWS
    assert w_out.shape[0] == n_steps * CAST_ROWS and 2 * RET_DIM == n_steps * CAST_ROWS
    full = lambda a: pl.BlockSpec(a.shape, lambda s: (0,) * a.ndim, pipeline_mode=pl.Buffered(1))
    const = lambda shape: pl.BlockSpec(shape, lambda s: (0, 0))
    feat = (RET_DIM, rows)
    return pl.pallas_call(
        functools.partial(_sample_front_kernel, n_seq=n_seq, dec_seq=dec_seq),
        grid=(n_steps,),
        in_specs=[full(xs), full(st), full(g1), full(w_in),
                  pl.BlockSpec((w_in.shape[0], CAST_ROWS), lambda s: (0, OFF_V // CAST_ROWS + s)),
                  full(w_conv), full(w_out), full(inv_col)],
        out_specs=[const(st.shape), const(feat), const(feat), const(feat), const(feat),
                   const((rows, RET_DIM)), const((rows, CONV_DIM)),
                   pl.BlockSpec((CAST_ROWS, w_in.shape[1]), lambda s: (s, 0)),
                   pl.BlockSpec((CAST_ROWS, w_out.shape[1]), lambda s: (s, 0)),
                   pl.BlockSpec((CAST_ROWS, w_in.shape[0]), lambda s: (s, 0))],
        out_shape=[jax.ShapeDtypeStruct(st.shape, F32),
                   jax.ShapeDtypeStruct(feat, F32), jax.ShapeDtypeStruct(feat, F32),
                   jax.ShapeDtypeStruct(feat, F32), jax.ShapeDtypeStruct(feat, F32),
                   jax.ShapeDtypeStruct((rows, RET_DIM), F32), jax.ShapeDtypeStruct((rows, CONV_DIM), BF16),
                   jax.ShapeDtypeStruct(w_in.shape, BF16), jax.ShapeDtypeStruct(w_out.shape, BF16),
                   jax.ShapeDtypeStruct((2 * RET_DIM, w_in.shape[0]), BF16)],
        compiler_params=pltpu.CompilerParams(
            dimension_semantics=("arbitrary",), vmem_limit_bytes=VMEM_LIMIT),
        name="sample_front",
    )(xs, st, g1, w_in, w_in, w_conv, w_out, inv_col)


def kernel(x_prompt, x_sample, state_conv, state_ret, meta_tokens, norm1_g, w_in, w_conv, ret_norm_g,
           w_out, norm2_g, w_gate, w_up, w_down, final_norm_g):
    n_p, seq, _ = x_prompt.shape
    n_s, dec_seq, _ = x_sample.shape
    rows_s = n_s * dec_seq
    assert norm1_g.shape[0] == 1 and seq % TILE == 0 and n_p % SEQS_PER_STEP == 0
    assert (n_p * seq) % FFN_TILE == 0
    assert n_s % LANES == 0 and rows_s % FFN_CHUNK == 0

    g1, g2, gf = norm1_g[0][None], norm2_g[0][None], final_norm_g[None]
    lg_lane = jnp.asarray(_LOG_GAMMA[np.arange(RET_DIM) // HEAD_DIM][None])
    inv = ROPE_BASE ** (-jnp.arange(HALF_DIM, dtype=F32) / HALF_DIM)
    inv_lane = jnp.tile(inv, LANES // HALF_DIM)[None]

    gain_col = jnp.broadcast_to(ret_norm_g[0][:, None], (RET_DIM, rows_s))

    st = state_conv[0].reshape(n_s, 2 * CONV_DIM)
    inv_col = jnp.broadcast_to(inv[:, None], (HALF_DIM, n_s))
    xs = jnp.transpose(x_sample, (1, 0, 2)).reshape(rows_s, D_MODEL)
    conv_s, qd_t, kd_t, v_t, o_intra, zg, cvo, w_in_b, w_out_b, w_vg_t = _sample_front(
        xs, st, g1, w_in[0], w_conv[0], w_out[0], inv_col, dec_seq)

    x1_p, conv_p, ret_p, wg_b, wu_b, wd_b = _prompt_mixer(
        x_prompt, meta_tokens, g1, w_in_b, w_vg_t, w_conv[0], gain_col, w_out_b, lg_lane, inv_lane,
        w_gate[0], w_up[0], w_down[0])

    s4 = jnp.transpose(state_ret[0], (1, 2, 3, 0))
    y_p, y_s, s4_new = _ffn(x1_p.reshape(n_p * seq, D_MODEL), xs, qd_t, kd_t, v_t, o_intra, zg, cvo, s4,
                            gain_col, w_out_b, g2, wg_b, wu_b, wd_b, gf, dec_seq)
    y_prompt = y_p.reshape(n_p, seq, D_MODEL)
    y_sample = jnp.transpose(y_s.reshape(dec_seq, n_s, D_MODEL), (1, 0, 2))
    ret_s = jnp.transpose(s4_new, (3, 0, 1, 2))

    return (y_prompt, y_sample, conv_p[None], ret_p[None], conv_s.reshape(1, n_s, 2, CONV_DIM), ret_s[None])
```

```python
import functools

import numpy as np
import jax
import jax.numpy as jnp
from jax import lax
from jax.experimental import pallas as pl
from jax.experimental.pallas import tpu as pltpu

D_MODEL = 1024
N_META = 16
CONV_DIM = 512
RET_HEADS = 8
HEAD_DIM = 64
HALF_DIM = HEAD_DIM // 2
RET_DIM = RET_HEADS * HEAD_DIM
D_FF = 2816
PAST_LEN = 16384
ROPE_BASE = 10000.0
EPS = 1e-6
GN_EPS = 1e-5

OFF_B, OFF_C, OFF_H, OFF_Q, OFF_K, OFF_V, OFF_G = (i * 512 for i in range(7))

LANES = 128
GROUP = 256
HEADS_PER_GROUP = GROUP // HEAD_DIM
N_GROUPS = RET_DIM // GROUP
TILE = 256
SEQS_PER_STEP = 4
FFN_TILE = 512
FFN_CHUNK = 256
CAST_ROWS = 512
PHASE_LAG = 3
FFN_PHASE_LAG = 1
BF16_ROWS = 16
VMEM_LIMIT = 56 * 1024 * 1024

F32 = jnp.float32
BF16 = jnp.bfloat16

_LOG_GAMMA = np.log1p(-(2.0 ** (-5.0 - np.arange(RET_HEADS)))).astype(np.float32)


def _dot(a, b):
    return jnp.dot(a, b, preferred_element_type=F32)


def _dot_nt(a, b):
    return lax.dot_general(a, b, (((1,), (1,)), ((), ())), preferred_element_type=F32)


def _dot_tn(a, b):
    return lax.dot_general(a, b, (((0,), (0,)), ((), ())), preferred_element_type=F32)


def _rmsnorm(x, g):
    ms = jnp.mean(x * x, axis=-1, keepdims=True)
    return x * lax.rsqrt(ms + EPS) * g


def _silu(x):
    return x * jax.nn.sigmoid(x)


def _iota(shape, dim):
    return lax.broadcasted_iota(jnp.int32, shape, dim)


def _first_half_lanes():
    return (_iota((1, LANES), 1) & (HEAD_DIM - 1)) < HALF_DIM


def _rope(t, cos, signed_sin):
    first_half = _first_half_lanes()
    out = []
    for b in range(t.shape[1] // LANES):
        blk = t[:, b * LANES:(b + 1) * LANES]
        partner = jnp.where(first_half, pltpu.roll(blk, LANES - HALF_DIM, axis=1),
                            pltpu.roll(blk, HALF_DIM, axis=1))
        out.append(blk * cos + partner * signed_sin)
    return jnp.concatenate(out, axis=1)


def _ones_block():
    r = _iota((GROUP, GROUP), 0) >> 6
    c = _iota((GROUP, GROUP), 1) >> 6
    return jnp.where(r == c, 1.0, 0.0).astype(BF16)


def _trace_staggered(phase_iters, lag):
    live = list(range(len(phase_iters)))
    tick = 0
    while live:
        for k in list(live):
            if tick >= lag * k:
                try:
                    next(phase_iters[k])
                except StopIteration:
                    live.remove(k)
        tick += 1


def _prompt_mixer_kernel(x_ref, meta_ref, g1_ref, win_ref, wvgt_ref, wconv_ref, gain_ref, wout_ref,
                         lg_ref, inv_ref, wg32_ref, wu32_ref, wd32_ref,
                         x1_ref, convst_ref, retst_ref, wg_ref, wu_ref, wd_ref,
                         cos_ref, sin_ref, qdec_ref, kdec_ref, dmask_ref,
                         smask_ref, sdec_ref, gaincol_ref, tail_ref, state_ref, ubuf_ref, *, n_tiles):
    j = pl.program_id(0)
    p = pl.program_id(1)
    C = TILE
    k_scale = HEAD_DIM ** -0.5

    def proj(xn, off):
        return _dot(xn, win_ref[:, off:off + 512])

    @pl.when((j == 0) & (p == 0))
    def _init():
        i_f = _iota((C, 1), 0).astype(F32)
        lg = lg_ref[...]
        qdec_ref[...] = jnp.exp((i_f + 1.0) * lg)
        kdec_ref[...] = jnp.exp((C - 1.0 - i_f) * lg) * k_scale
        sdec_ref[...] = jnp.exp(C * lg)
        gaincol_ref[...] = jnp.broadcast_to(gain_ref[...], (C, RET_DIM)).T
        diff = (_iota((C, C), 1) - _iota((C, C), 0)).astype(F32)
        for h in range(RET_HEADS):
            dmask_ref[h] = jnp.where(
                diff >= 0, jnp.exp(jnp.maximum(diff, 0.0) * float(_LOG_GAMMA[h])) * k_scale, 0.0)
        smask = _ones_block().astype(F32)
        smask_ref[...] = smask

        xm = _rmsnorm(meta_ref[...], g1_ref[...]).astype(BF16)
        um = proj(xm, OFF_C) * proj(xm, OFF_H)
        vm = proj(xm, OFF_V).astype(BF16)
        m_f = _iota((N_META, 1), 0).astype(F32)
        ang = m_f * inv_ref[...]
        sin = jnp.sin(ang)
        kmr = _rope(proj(xm, OFF_K), jnp.cos(ang), jnp.where(_first_half_lanes(), -sin, sin))
        kmd = (kmr * (jnp.exp((N_META - 1.0 - m_f) * lg) * k_scale)).astype(BF16)
        for b in range(tail_ref.shape[0]):
            tail_ref[b] = um[N_META - 2:N_META, :]
            for g in range(N_GROUPS):
                sl = slice(g * GROUP, (g + 1) * GROUP)
                state_ref[b, g] = _dot_tn(vm[:, sl], kmd[:, sl]) * smask

    @pl.when(p == 0)
    def _rope_tables():
        pos = (N_META + j * C + _iota((C, LANES), 0)).astype(F32)
        ang = pos * inv_ref[...]
        sin = jnp.sin(ang)
        cos_ref[...] = jnp.cos(ang)
        sin_ref[...] = jnp.where(_first_half_lanes(), -sin, sin)

    cos = cos_ref[...]
    sin = sin_ref[...]
    head_of_lane = _iota((1, GROUP), 1) >> 6
    smask = smask_ref[...]
    wc = wconv_ref[...]

    def mix_sequence(slot):
        n = p * SEQS_PER_STEP + slot
        x = x_ref[slot]
        xn = _rmsnorm(x, g1_ref[...]).astype(BF16)

        zb = proj(xn, OFF_B)
        yield
        u = proj(xn, OFF_C) * proj(xn, OFF_H)
        ubuf_ref[slot, 6:8, :] = tail_ref[n]
        ubuf_ref[slot, 8:8 + C, :] = u
        conv = wc[0:1] * ubuf_ref[slot, 6:6 + C, :] + wc[1:2] * ubuf_ref[slot, 7:7 + C, :] + wc[2:3] * u
        conv_out = (zb * conv).astype(BF16)
        new_tail = u[C - 2:C, :]
        tail_ref[n] = new_tail

        yield

        qr = _rope(proj(xn, OFF_Q), cos, sin)
        yield
        kr = _rope(proj(xn, OFF_K), cos, sin)
        yield
        vg_t = _dot_nt(wvgt_ref[...], xn)
        v_t = vg_t[:RET_DIM].astype(BF16)
        yield
        qb = qr.astype(BF16)
        kb = kr.astype(BF16)
        qd = (qr * qdec_ref[...]).astype(BF16)
        kd = (kr * kdec_ref[...]).astype(BF16)
        o_parts = []
        new_states = []
        for g in range(N_GROUPS):
            sl = slice(g * GROUP, (g + 1) * GROUP)
            state_t = state_ref[n, g]
            cross_t = _dot_nt(state_t.astype(BF16), qd[:, sl])
            k_heads = jnp.concatenate(
                [jnp.where(head_of_lane == hh, kb[:, sl], jnp.zeros_like(kb[:, sl]))
                 for hh in range(HEADS_PER_GROUP)], axis=0)
            decay = dmask_ref[g * HEADS_PER_GROUP:(g + 1) * HEADS_PER_GROUP].reshape(HEADS_PER_GROUP * C, C)
            scores_t = (_dot_nt(k_heads, qb[:, sl]) * decay).astype(BF16)
            for hh in range(HEADS_PER_GROUP):
                h = g * HEADS_PER_GROUP + hh
                rows = slice(hh * HEAD_DIM, (hh + 1) * HEAD_DIM)
                o_parts.append(cross_t[rows] + _dot(v_t[h * HEAD_DIM:(h + 1) * HEAD_DIM],
                                                    scores_t[hh * C:(hh + 1) * C]))
            new_state = state_t * sdec_ref[:, sl] + _dot(v_t[sl], kd[:, sl]) * smask
            state_ref[n, g] = new_state
            new_states.append(new_state)
            yield

        normed = []
        for h, o_h in enumerate(o_parts):
            mu = jnp.sum(o_h, axis=0, keepdims=True) * (1.0 / HEAD_DIM)
            dlt = o_h - mu
            var = jnp.sum(dlt * dlt, axis=0, keepdims=True) * (1.0 / HEAD_DIM)
            normed.append(dlt * lax.rsqrt(var + GN_EPS) * gaincol_ref[h * HEAD_DIM:(h + 1) * HEAD_DIM, :])
        ret_out_t = (_silu(vg_t[RET_DIM:]) * jnp.concatenate(normed, axis=0)).astype(BF16)
        yield

        x1_ref[slot] = (x + _dot(conv_out, wout_ref[:CONV_DIM, :])
                        + _dot_tn(ret_out_t, wout_ref[CONV_DIM:, :]))
        finals[slot] = (new_tail, new_states)

    finals = [None] * SEQS_PER_STEP
    _trace_staggered([mix_sequence(slot) for slot in range(SEQS_PER_STEP)], PHASE_LAG)

    wg_ref[...] = wg32_ref[...].astype(BF16)
    wu_ref[...] = wu32_ref[...].astype(BF16)
    wd_ref[...] = wd32_ref[...].astype(BF16)

    @pl.when(j == n_tiles - 1)
    def _final_states():
        for slot, (new_tail, new_states) in enumerate(finals):
            n = p * SEQS_PER_STEP + slot
            convst_ref[n] = new_tail
            for g in range(N_GROUPS):
                state = new_states[g].T
                for hh in range(HEADS_PER_GROUP):
                    blk = slice(hh * HEAD_DIM, (hh + 1) * HEAD_DIM)
                    retst_ref[n, g * HEADS_PER_GROUP + hh] = state[blk, blk]


def _prompt_mixer(x_prompt, meta, g1, w_in, w_vg_t, w_conv, gain, w_out, lg_lane, inv_lane,
                  w_gate, w_up, w_down):
    n_seq, seq, _ = x_prompt.shape
    n_tiles = seq // TILE
    pairs = n_seq // SEQS_PER_STEP
    n_steps = n_tiles * pairs
    const2 = lambda j, p: (0, 0)
    full = lambda a: pl.BlockSpec(a.shape, const2, pipeline_mode=pl.Buffered(1))
    gu_rows = w_gate.shape[0] // n_steps
    d_rows = w_down.shape[0] // (n_steps // 2)
    assert gu_rows * n_steps == w_gate.shape[0] and d_rows * (n_steps // 2) == w_down.shape[0]
    assert gu_rows % BF16_ROWS == 0 and d_rows % BF16_ROWS == 0
    gu_slab = pl.BlockSpec((gu_rows, w_gate.shape[1]), lambda j, p: (j * pairs + p, 0))
    d_slab = pl.BlockSpec((d_rows, w_down.shape[1]), lambda j, p: ((j * pairs + p) // 2, 0))
    return pl.pallas_call(
        functools.partial(_prompt_mixer_kernel, n_tiles=n_tiles),
        grid=(n_tiles, pairs),
        in_specs=[pl.BlockSpec((SEQS_PER_STEP, TILE, D_MODEL), lambda j, p: (p, j, 0)),
                  full(meta), full(g1), full(w_in), full(w_vg_t), full(w_conv), full(gain),
                  full(w_out), full(lg_lane), full(inv_lane), gu_slab, gu_slab, d_slab],
        out_specs=[pl.BlockSpec((SEQS_PER_STEP, TILE, D_MODEL), lambda j, p: (p, j, 0)),
                   pl.BlockSpec((n_seq, 2, CONV_DIM), lambda j, p: (0, 0, 0)),
                   pl.BlockSpec((n_seq, RET_HEADS, HEAD_DIM, HEAD_DIM), lambda j, p: (0, 0, 0, 0)),
                   gu_slab, gu_slab, d_slab],
        out_shape=[jax.ShapeDtypeStruct((n_seq, seq, D_MODEL), F32),
                   jax.ShapeDtypeStruct((n_seq, 2, CONV_DIM), F32),
                   jax.ShapeDtypeStruct((n_seq, RET_HEADS, HEAD_DIM, HEAD_DIM), F32),
                   jax.ShapeDtypeStruct(w_gate.shape, BF16), jax.ShapeDtypeStruct(w_up.shape, BF16),
                   jax.ShapeDtypeStruct(w_down.shape, BF16)],
        scratch_shapes=[
            pltpu.VMEM((TILE, LANES), F32),
            pltpu.VMEM((TILE, LANES), F32),
            pltpu.VMEM((TILE, RET_DIM), F32),
            pltpu.VMEM((TILE, RET_DIM), F32),
            pltpu.VMEM((RET_HEADS, TILE, TILE), F32),
            pltpu.VMEM((GROUP, GROUP), F32),
            pltpu.VMEM((1, RET_DIM), F32),
            pltpu.VMEM((RET_DIM, TILE), F32),
            pltpu.VMEM((n_seq, 2, CONV_DIM), F32),
            pltpu.VMEM((n_seq, N_GROUPS, GROUP, GROUP), F32),
            pltpu.VMEM((SEQS_PER_STEP, TILE + 8, CONV_DIM), F32),
        ],
        compiler_params=pltpu.CompilerParams(
            dimension_semantics=("arbitrary", "arbitrary"), vmem_limit_bytes=VMEM_LIMIT),
        name="prompt_mixer",
    )(x_prompt, meta, g1, w_in, w_vg_t, w_conv, gain, w_out, lg_lane, inv_lane, w_gate, w_up, w_down)


def _ffn_kernel(xp_ref, xs_ref, qd_ref, kd_ref, v_ref, ointra_ref, zg_ref, cvo_ref, s_ref, gain_ref,
                sdec_ref, wout_ref, g2_ref, wg_ref, wu_ref, wd_ref, gf_ref,
                yp_ref, ys_ref, snew_ref, o_ref, *, prompt_steps, n_seq, dec_seq, e_piece):
    i = pl.program_id(0)
    L, B = dec_seq, n_seq
    pieces_per_head = HEAD_DIM // e_piece

    def chunk_phases(load, y_ref, rows):
        x = load(rows)
        xn = _rmsnorm(x, g2_ref[...]).astype(BF16)
        yield
        gate = _dot(xn, wg_ref[...])
        yield
        up = _dot(xn, wu_ref[...])
        yield
        hidden = (_silu(gate) * up).astype(BF16)
        yield
        y_ref[rows, :] = _rmsnorm(x + _dot(hidden, wd_ref[...]), gf_ref[...])

    def ffn_chunks(load, y_ref, extra=()):
        chunks = [slice(r, r + FFN_CHUNK) for r in range(0, y_ref.shape[0], FFN_CHUNK)]
        _trace_staggered([chunk_phases(load, y_ref, rows) for rows in chunks] + list(extra), FFN_PHASE_LAG)

    def state_piece():
        h = i // pieces_per_head
        base = pl.multiple_of(h * HEAD_DIM, HEAD_DIM)
        rows = pl.ds(pl.multiple_of(base + (i % pieces_per_head) * e_piece, e_piece), e_piece)
        qd_h = qd_ref[pl.ds(base, HEAD_DIM), :]
        kd_h = kd_ref[pl.ds(base, HEAD_DIM), :]
        state_decay = sdec_ref[h]
        v_blk = [v_ref[rows, j * B:(j + 1) * B] for j in range(L)]
        cross = [jnp.zeros((e_piece, B), F32) for _ in range(L)]
        for d in range(HEAD_DIM):
            s_de = s_ref[0, d]
            new = s_de * state_decay
            for t in range(L):
                lanes = slice(t * B, (t + 1) * B)
                cross[t] = cross[t] + qd_h[d:d + 1, lanes] * s_de
                new = new + kd_h[d:d + 1, lanes] * v_blk[t]
            snew_ref[0, d] = new
        for t in range(L):
            o_ref[rows, t * B:(t + 1) * B] = o_ref[rows, t * B:(t + 1) * B] + cross[t]
        yield

    @pl.when(i == 0)
    def _seed_retention_output():
        o_ref[...] = ointra_ref[...]

    @pl.when(i < prompt_steps)
    def _prompt_rows():
        ffn_chunks(lambda rows: xp_ref[rows, :], yp_ref, extra=[state_piece()])

    @pl.when(i == prompt_steps)
    def _sample_rows():
        gain_col = jnp.broadcast_to(gain_ref[...], (L * B, RET_DIM)).T
        for hd in range(RET_HEADS):
            r0 = hd * HEAD_DIM
            o_h = o_ref[r0:r0 + HEAD_DIM, :]
            mu = jnp.sum(o_h, axis=0, keepdims=True) * (1.0 / HEAD_DIM)
            dlt = o_h - mu
            var = jnp.sum(dlt * dlt, axis=0, keepdims=True) * (1.0 / HEAD_DIM)
            o_ref[r0:r0 + HEAD_DIM, :] = dlt * lax.rsqrt(var + GN_EPS) * gain_col[r0:r0 + HEAD_DIM, :]
        ret_out = (_silu(zg_ref[...]) * o_ref[...].T).astype(BF16)
        x1 = (xs_ref[...] + _dot(cvo_ref[...], wout_ref[:CONV_DIM, :])
              + _dot(ret_out, wout_ref[CONV_DIM:, :]))
        ffn_chunks(lambda rows: x1[rows], ys_ref)


def _ffn(x_p, xs, qd_t, kd_t, v_t, o_intra, zg, cvo, s4, gain, w_out_b, g2, w_gate, w_up, w_down, gf,
         dec_seq):
    prompt_steps = x_p.shape[0] // FFN_TILE
    n_seq = xs.shape[0] // dec_seq
    assert prompt_steps % RET_HEADS == 0
    e_piece = HEAD_DIM // (prompt_steps // RET_HEADS)
    assert e_piece % 8 == 0 and HEAD_DIM % e_piece == 0
    pieces_per_head = HEAD_DIM // e_piece
    sdec = jnp.asarray(np.exp(np.float32(dec_seq) * _LOG_GAMMA))
    full = lambda a: pl.BlockSpec(a.shape, lambda i: (0,) * a.ndim, pipeline_mode=pl.Buffered(1))
    prompt_tile = pl.BlockSpec((FFN_TILE, D_MODEL), lambda i: (jnp.minimum(i, prompt_steps - 1), 0))
    sample_out = pl.BlockSpec(xs.shape, lambda i: (0, 0))

    def piece_index(i):
        piece = jnp.minimum(i, prompt_steps - 1)
        return (piece // pieces_per_head, 0, piece % pieces_per_head, 0)

    state_piece = pl.BlockSpec((1, HEAD_DIM, e_piece, n_seq), piece_index)
    return pl.pallas_call(
        functools.partial(_ffn_kernel, prompt_steps=prompt_steps, n_seq=n_seq, dec_seq=dec_seq,
                          e_piece=e_piece),
        grid=(prompt_steps + 1,),
        in_specs=[prompt_tile, full(xs), full(qd_t), full(kd_t), full(v_t), full(o_intra), full(zg),
                  full(cvo), state_piece, full(gain), pl.BlockSpec(memory_space=pltpu.SMEM), full(w_out_b),
                  full(g2),
                  full(w_gate), full(w_up), full(w_down), full(gf)],
        out_specs=[prompt_tile, sample_out, state_piece],
        out_shape=[jax.ShapeDtypeStruct(x_p.shape, F32), jax.ShapeDtypeStruct(xs.shape, F32),
                   jax.ShapeDtypeStruct(s4.shape, F32)],
        scratch_shapes=[pltpu.VMEM(o_intra.shape, F32)],
        compiler_params=pltpu.CompilerParams(
            dimension_semantics=("arbitrary",), vmem_limit_bytes=VMEM_LIMIT),
        name="ffn",
    )(x_p, xs, qd_t, kd_t, v_t, o_intra, zg, cvo, s4, gain, sdec, w_out_b, g2, w_gate, w_up, w_down, gf)


def _gamma_pow(head, power):
    return float(np.exp(np.float32(power) * _LOG_GAMMA[head]))


def _sample_front_kernel(x_ref, st_ref, g1_ref, win32_ref, wvg32_ref, wconv_ref, wout32_ref, inv_ref,
                         convst_ref, qd_ref, kd_ref, v_ref, o_ref, zg_ref, cvo_ref,
                         win_ref, wout_ref, wvgt_ref, *, n_seq, dec_seq):
    s = pl.program_id(0)
    L, B = dec_seq, n_seq
    k_scale = HEAD_DIM ** -0.5

    slab = pl.ds(pl.multiple_of(s * CAST_ROWS, CAST_ROWS), CAST_ROWS)
    win_ref[...] = win32_ref[slab, :].astype(BF16)
    wout_ref[...] = wout32_ref[slab, :].astype(BF16)
    wvgt_ref[...] = wvg32_ref[...].T.astype(BF16)

    @pl.when(s == 0)
    def _dense_front():
        xn = _rmsnorm(x_ref[...], g1_ref[...]).astype(BF16)

        def proj(off):
            return _dot(xn, win32_ref[:, off:off + 512].astype(BF16))

        zb = proj(OFF_B)
        u = proj(OFF_C) * proj(OFF_H)
        ext = [st_ref[:, :CONV_DIM], st_ref[:, CONV_DIM:]] + [u[i * B:(i + 1) * B] for i in range(L)]
        wc = wconv_ref[...]
        for i in range(L):
            conv = wc[0:1] * ext[i] + wc[1:2] * ext[i + 1] + wc[2:3] * ext[i + 2]
            cvo_ref[i * B:(i + 1) * B, :] = (zb[i * B:(i + 1) * B] * conv).astype(BF16)
        convst_ref[...] = jnp.concatenate([ext[L], ext[L + 1]], axis=1)
        zg_ref[...] = proj(OFF_G)

        q_t = proj(OFF_Q).T
        k_t = proj(OFF_K).T
        v_t = proj(OFF_V).T
        v_ref[...] = v_t
        inv = jnp.broadcast_to(inv_ref[...], (B, LANES)).T[:HALF_DIM]
        cos_sin = [(jnp.cos(float(PAST_LEN + i) * inv), jnp.sin(float(PAST_LEN + i) * inv))
                   for i in range(L)]

        def rope(t, r0, i):
            cos, sin = cos_sin[i]
            t1 = t[r0:r0 + HALF_DIM, i * B:(i + 1) * B]
            t2 = t[r0 + HALF_DIM:r0 + HEAD_DIM, i * B:(i + 1) * B]
            return jnp.concatenate([t1 * cos - t2 * sin, t2 * cos + t1 * sin], axis=0)

        for hd in range(RET_HEADS):
            r0 = hd * HEAD_DIM
            qr = [rope(q_t, r0, i) for i in range(L)]
            kr = [rope(k_t, r0, i) for i in range(L)]
            for i in range(L):
                lanes = slice(i * B, (i + 1) * B)
                qd_ref[r0:r0 + HEAD_DIM, lanes] = qr[i] * _gamma_pow(hd, i + 1)
                kd_ref[r0:r0 + HEAD_DIM, lanes] = kr[i] * (_gamma_pow(hd, L - 1 - i) * k_scale)
                intra = jnp.zeros((HEAD_DIM, B), F32)
                for j in range(i + 1):
                    score = jnp.sum(qr[i] * kr[j], axis=0, keepdims=True) * (_gamma_pow(hd, i - j) * k_scale)
                    intra = intra + score * v_t[r0:r0 + HEAD_DIM, j * B:(j + 1) * B]
                o_ref[r0:r0 + HEAD_DIM, lanes] = intra


def _sample_front(xs, st, g1, w_in, w_conv, w_out, inv_lane, dec_seq):
    rows = xs.shape[0]
    n_seq = rows // dec_seq
    n_steps = w_in.shape[0] // CAST_ROWS
    assert w_out.shape[0] == n_steps * CAST_ROWS and 2 * RET_DIM == n_steps * CAST_ROWS
    full = lambda a: pl.BlockSpec(a.shape, lambda s: (0,) * a.ndim, pipeline_mode=pl.Buffered(1))
    const = lambda shape: pl.BlockSpec(shape, lambda s: (0, 0))
    feat = (RET_DIM, rows)
    return pl.pallas_call(
        functools.partial(_sample_front_kernel, n_seq=n_seq, dec_seq=dec_seq),
        grid=(n_steps,),
        in_specs=[full(xs), full(st), full(g1), full(w_in),
                  pl.BlockSpec((w_in.shape[0], CAST_ROWS), lambda s: (0, OFF_V // CAST_ROWS + s)),
                  full(w_conv), full(w_out), full(inv_lane)],
        out_specs=[const(st.shape), const(feat), const(feat), const(feat), const(feat),
                   const((rows, RET_DIM)), const((rows, CONV_DIM)),
                   pl.BlockSpec((CAST_ROWS, w_in.shape[1]), lambda s: (s, 0)),
                   pl.BlockSpec((CAST_ROWS, w_out.shape[1]), lambda s: (s, 0)),
                   pl.BlockSpec((CAST_ROWS, w_in.shape[0]), lambda s: (s, 0))],
        out_shape=[jax.ShapeDtypeStruct(st.shape, F32),
                   jax.ShapeDtypeStruct(feat, F32), jax.ShapeDtypeStruct(feat, F32),
                   jax.ShapeDtypeStruct(feat, F32), jax.ShapeDtypeStruct(feat, F32),
                   jax.ShapeDtypeStruct((rows, RET_DIM), F32), jax.ShapeDtypeStruct((rows, CONV_DIM), BF16),
                   jax.ShapeDtypeStruct(w_in.shape, BF16), jax.ShapeDtypeStruct(w_out.shape, BF16),
                   jax.ShapeDtypeStruct((2 * RET_DIM, w_in.shape[0]), BF16)],
        compiler_params=pltpu.CompilerParams(
            dimension_semantics=("arbitrary",), vmem_limit_bytes=VMEM_LIMIT),
        name="sample_front",
    )(xs, st, g1, w_in, w_in, w_conv, w_out, inv_lane)


def kernel(x_prompt, x_sample, state_conv, state_ret, meta_tokens, norm1_g, w_in, w_conv, ret_norm_g,
           w_out, norm2_g, w_gate, w_up, w_down, final_norm_g):
    n_p, seq, _ = x_prompt.shape
    n_s, dec_seq, _ = x_sample.shape
    rows_s = n_s * dec_seq
    assert norm1_g.shape[0] == 1 and seq % TILE == 0 and n_p % SEQS_PER_STEP == 0
    assert (n_p * seq) % FFN_TILE == 0
    assert n_s % LANES == 0 and rows_s % FFN_CHUNK == 0

    g1, g2, gf = norm1_g[0][None], norm2_g[0][None], final_norm_g[None]
    lg_lane = jnp.asarray(_LOG_GAMMA[np.arange(RET_DIM) // HEAD_DIM][None])
    inv = ROPE_BASE ** (-jnp.arange(HALF_DIM, dtype=F32) / HALF_DIM)
    inv_lane = jnp.tile(inv, LANES // HALF_DIM)[None]

    gain = ret_norm_g[0][None]

    st = state_conv[0].reshape(n_s, 2 * CONV_DIM)
    xs = jnp.transpose(x_sample, (1, 0, 2)).reshape(rows_s, D_MODEL)
    conv_s, qd_t, kd_t, v_t, o_intra, zg, cvo, w_in_b, w_out_b, w_vg_t = _sample_front(
        xs, st, g1, w_in[0], w_conv[0], w_out[0], inv_lane, dec_seq)

    x1_p, conv_p, ret_p, wg_b, wu_b, wd_b = _prompt_mixer(
        x_prompt, meta_tokens, g1, w_in_b, w_vg_t, w_conv[0], gain, w_out_b, lg_lane, inv_lane,
        w_gate[0], w_up[0], w_down[0])

    s4 = jnp.transpose(state_ret[0], (1, 2, 3, 0))
    y_p, y_s, s4_new = _ffn(x1_p.reshape(n_p * seq, D_MODEL), xs, qd_t, kd_t, v_t, o_intra, zg, cvo, s4,
                            gain, w_out_b, g2, wg_b, wu_b, wd_b, gf, dec_seq)
    y_prompt = y_p.reshape(n_p, seq, D_MODEL)
    y_sample = jnp.transpose(y_s.reshape(dec_seq, n_s, D_MODEL), (1, 0, 2))
    ret_s = jnp.transpose(s4_new, (3, 0, 1, 2))

    return (y_prompt, y_sample, conv_p[None], ret_p[None], conv_s.reshape(1, n_s, 2, CONV_DIM), ret_s[None])
```

```python
import functools

import numpy as np
import jax
import jax.numpy as jnp
from jax import lax
from jax.experimental import pallas as pl
from jax.experimental.pallas import tpu as pltpu

D_MODEL = 1024
N_META = 16
CONV_DIM = 512
RET_HEADS = 8
HEAD_DIM = 64
HALF_DIM = HEAD_DIM // 2
RET_DIM = RET_HEADS * HEAD_DIM
D_FF = 2816
PAST_LEN = 16384
ROPE_BASE = 10000.0
EPS = 1e-6
GN_EPS = 1e-5

OFF_B, OFF_C, OFF_H, OFF_Q, OFF_K, OFF_V, OFF_G = (i * 512 for i in range(7))

LANES = 128
GROUP = 256
HEADS_PER_GROUP = GROUP // HEAD_DIM
N_GROUPS = RET_DIM // GROUP
TILE = 256
SEQS_PER_STEP = 4
FFN_TILE = 512
FFN_CHUNK = 256
CAST_ROWS = 256
PHASE_LAG = 3
FFN_PHASE_LAG = 1
BF16_ROWS = 16
VMEM_LIMIT = 56 * 1024 * 1024

F32 = jnp.float32
BF16 = jnp.bfloat16

_LOG_GAMMA = np.log1p(-(2.0 ** (-5.0 - np.arange(RET_HEADS)))).astype(np.float32)


def _dot(a, b):
    return jnp.dot(a, b, preferred_element_type=F32)


def _dot_nt(a, b):
    return lax.dot_general(a, b, (((1,), (1,)), ((), ())), preferred_element_type=F32)


def _dot_tn(a, b):
    return lax.dot_general(a, b, (((0,), (0,)), ((), ())), preferred_element_type=F32)


def _rmsnorm(x, g):
    ms = jnp.mean(x * x, axis=-1, keepdims=True)
    return x * lax.rsqrt(ms + EPS) * g


def _silu(x):
    return x * jax.nn.sigmoid(x)


def _iota(shape, dim):
    return lax.broadcasted_iota(jnp.int32, shape, dim)


def _first_half_lanes():
    return (_iota((1, LANES), 1) & (HEAD_DIM - 1)) < HALF_DIM


def _rope(t, cos, signed_sin):
    first_half = _first_half_lanes()
    out = []
    for b in range(t.shape[1] // LANES):
        blk = t[:, b * LANES:(b + 1) * LANES]
        partner = jnp.where(first_half, pltpu.roll(blk, LANES - HALF_DIM, axis=1),
                            pltpu.roll(blk, HALF_DIM, axis=1))
        out.append(blk * cos + partner * signed_sin)
    return jnp.concatenate(out, axis=1)


def _ones_block():
    r = _iota((GROUP, GROUP), 0) >> 6
    c = _iota((GROUP, GROUP), 1) >> 6
    return jnp.where(r == c, 1.0, 0.0).astype(BF16)


def _trace_staggered(phase_iters, lag):
    live = list(range(len(phase_iters)))
    tick = 0
    while live:
        for k in list(live):
            if tick >= lag * k:
                try:
                    next(phase_iters[k])
                except StopIteration:
                    live.remove(k)
        tick += 1


def _prompt_mixer_kernel(x_ref, meta_ref, g1_ref, win_ref, wvgt_ref, wconv_ref, gain_ref, wout_ref,
                         lg_ref, inv_ref, wg32_ref, wu32_ref, wd32_ref,
                         x1_ref, convst_ref, retst_ref, wg_ref, wu_ref, wd_ref,
                         cos_ref, sin_ref, qdec_ref, kdec_ref, dmask_ref,
                         smask_ref, sdec_ref, gaincol_ref, tail_ref, state_ref, ubuf_ref, *, n_tiles):
    j = pl.program_id(0)
    p = pl.program_id(1)
    C = TILE
    k_scale = HEAD_DIM ** -0.5

    def proj(xn, off):
        return _dot(xn, win_ref[:, off:off + 512])

    @pl.when((j == 0) & (p == 0))
    def _init():
        i_f = _iota((C, 1), 0).astype(F32)
        lg = lg_ref[...]
        qdec_ref[...] = jnp.exp((i_f + 1.0) * lg)
        kdec_ref[...] = jnp.exp((C - 1.0 - i_f) * lg) * k_scale
        sdec_ref[...] = jnp.exp(C * lg)
        gaincol_ref[...] = jnp.broadcast_to(gain_ref[...], (C, RET_DIM)).T
        diff = (_iota((C, C), 1) - _iota((C, C), 0)).astype(F32)
        for h in range(RET_HEADS):
            dmask_ref[h] = jnp.where(
                diff >= 0, jnp.exp(jnp.maximum(diff, 0.0) * float(_LOG_GAMMA[h])) * k_scale, 0.0)
        smask = _ones_block().astype(F32)
        smask_ref[...] = smask

        xm = _rmsnorm(meta_ref[...], g1_ref[...]).astype(BF16)
        um = proj(xm, OFF_C) * proj(xm, OFF_H)
        vm = proj(xm, OFF_V).astype(BF16)
        m_f = _iota((N_META, 1), 0).astype(F32)
        ang = m_f * inv_ref[...]
        sin = jnp.sin(ang)
        kmr = _rope(proj(xm, OFF_K), jnp.cos(ang), jnp.where(_first_half_lanes(), -sin, sin))
        kmd = (kmr * (jnp.exp((N_META - 1.0 - m_f) * lg) * k_scale)).astype(BF16)
        for b in range(tail_ref.shape[0]):
            tail_ref[b] = um[N_META - 2:N_META, :]
            for g in range(N_GROUPS):
                sl = slice(g * GROUP, (g + 1) * GROUP)
                state_ref[b, g] = _dot_tn(vm[:, sl], kmd[:, sl]) * smask

    @pl.when(p == 0)
    def _rope_tables():
        pos = (N_META + j * C + _iota((C, LANES), 0)).astype(F32)
        ang = pos * inv_ref[...]
        sin = jnp.sin(ang)
        cos_ref[...] = jnp.cos(ang)
        sin_ref[...] = jnp.where(_first_half_lanes(), -sin, sin)

    cos = cos_ref[...]
    sin = sin_ref[...]
    head_of_lane = _iota((1, GROUP), 1) >> 6
    smask = smask_ref[...]
    wc = wconv_ref[...]

    def mix_sequence(slot):
        n = p * SEQS_PER_STEP + slot
        x = x_ref[slot]
        xn = _rmsnorm(x, g1_ref[...]).astype(BF16)

        zb = proj(xn, OFF_B)
        yield
        u = proj(xn, OFF_C) * proj(xn, OFF_H)
        ubuf_ref[slot, 6:8, :] = tail_ref[n]
        ubuf_ref[slot, 8:8 + C, :] = u
        conv = wc[0:1] * ubuf_ref[slot, 6:6 + C, :] + wc[1:2] * ubuf_ref[slot, 7:7 + C, :] + wc[2:3] * u
        conv_out = (zb * conv).astype(BF16)
        new_tail = u[C - 2:C, :]
        tail_ref[n] = new_tail

        yield

        qr = _rope(proj(xn, OFF_Q), cos, sin)
        yield
        kr = _rope(proj(xn, OFF_K), cos, sin)
        yield
        vg_t = _dot_nt(wvgt_ref[...], xn)
        v_t = vg_t[:RET_DIM].astype(BF16)
        yield
        qb = qr.astype(BF16)
        kb = kr.astype(BF16)
        qd = (qr * qdec_ref[...]).astype(BF16)
        kd = (kr * kdec_ref[...]).astype(BF16)
        o_parts = []
        new_states = []
        for g in range(N_GROUPS):
            sl = slice(g * GROUP, (g + 1) * GROUP)
            state_t = state_ref[n, g]
            cross_t = _dot_nt(state_t.astype(BF16), qd[:, sl])
            k_heads = jnp.concatenate(
                [jnp.where(head_of_lane == hh, kb[:, sl], jnp.zeros_like(kb[:, sl]))
                 for hh in range(HEADS_PER_GROUP)], axis=0)
            decay = dmask_ref[g * HEADS_PER_GROUP:(g + 1) * HEADS_PER_GROUP].reshape(HEADS_PER_GROUP * C, C)
            scores_t = (_dot_nt(k_heads, qb[:, sl]) * decay).astype(BF16)
            for hh in range(HEADS_PER_GROUP):
                h = g * HEADS_PER_GROUP + hh
                rows = slice(hh * HEAD_DIM, (hh + 1) * HEAD_DIM)
                o_parts.append(cross_t[rows] + _dot(v_t[h * HEAD_DIM:(h + 1) * HEAD_DIM],
                                                    scores_t[hh * C:(hh + 1) * C]))
            new_state = state_t * sdec_ref[:, sl] + _dot(v_t[sl], kd[:, sl]) * smask
            state_ref[n, g] = new_state
            new_states.append(new_state)
            yield

        normed = []
        for h, o_h in enumerate(o_parts):
            mu = jnp.sum(o_h, axis=0, keepdims=True) * (1.0 / HEAD_DIM)
            dlt = o_h - mu
            var = jnp.sum(dlt * dlt, axis=0, keepdims=True) * (1.0 / HEAD_DIM)
            normed.append(dlt * lax.rsqrt(var + GN_EPS) * gaincol_ref[h * HEAD_DIM:(h + 1) * HEAD_DIM, :])
        ret_out_t = (_silu(vg_t[RET_DIM:]) * jnp.concatenate(normed, axis=0)).astype(BF16)
        yield

        x1_ref[slot] = (x + _dot(conv_out, wout_ref[:CONV_DIM, :])
                        + _dot_tn(ret_out_t, wout_ref[CONV_DIM:, :]))
        finals[slot] = (new_tail, new_states)

    finals = [None] * SEQS_PER_STEP
    _trace_staggered([mix_sequence(slot) for slot in range(SEQS_PER_STEP)], PHASE_LAG)

    wg_ref[...] = wg32_ref[...].astype(BF16)
    wu_ref[...] = wu32_ref[...].astype(BF16)
    wd_ref[...] = wd32_ref[...].astype(BF16)

    @pl.when(j == n_tiles - 1)
    def _final_states():
        for slot, (new_tail, new_states) in enumerate(finals):
            n = p * SEQS_PER_STEP + slot
            convst_ref[n] = new_tail
            for g in range(N_GROUPS):
                state = new_states[g].T
                for hh in range(HEADS_PER_GROUP):
                    blk = slice(hh * HEAD_DIM, (hh + 1) * HEAD_DIM)
                    retst_ref[n, g * HEADS_PER_GROUP + hh] = state[blk, blk]


def _prompt_mixer(x_prompt, meta, g1, w_in, w_vg_t, w_conv, gain, w_out, lg_lane, inv_lane,
                  w_gate, w_up, w_down):
    n_seq, seq, _ = x_prompt.shape
    n_tiles = seq // TILE
    pairs = n_seq // SEQS_PER_STEP
    n_steps = n_tiles * pairs
    const2 = lambda j, p: (0, 0)
    full = lambda a: pl.BlockSpec(a.shape, const2, pipeline_mode=pl.Buffered(1))
    gu_rows = w_gate.shape[0] // n_steps
    d_rows = w_down.shape[0] // (n_steps // 2)
    assert gu_rows * n_steps == w_gate.shape[0] and d_rows * (n_steps // 2) == w_down.shape[0]
    assert gu_rows % BF16_ROWS == 0 and d_rows % BF16_ROWS == 0
    gu_slab = pl.BlockSpec((gu_rows, w_gate.shape[1]), lambda j, p: (j * pairs + p, 0))
    d_slab = pl.BlockSpec((d_rows, w_down.shape[1]), lambda j, p: ((j * pairs + p) // 2, 0))
    return pl.pallas_call(
        functools.partial(_prompt_mixer_kernel, n_tiles=n_tiles),
        grid=(n_tiles, pairs),
        in_specs=[pl.BlockSpec((SEQS_PER_STEP, TILE, D_MODEL), lambda j, p: (p, j, 0)),
                  full(meta), full(g1), full(w_in), full(w_vg_t), full(w_conv), full(gain),
                  full(w_out), full(lg_lane), full(inv_lane), gu_slab, gu_slab, d_slab],
        out_specs=[pl.BlockSpec((SEQS_PER_STEP, TILE, D_MODEL), lambda j, p: (p, j, 0)),
                   pl.BlockSpec((n_seq, 2, CONV_DIM), lambda j, p: (0, 0, 0)),
                   pl.BlockSpec((n_seq, RET_HEADS, HEAD_DIM, HEAD_DIM), lambda j, p: (0, 0, 0, 0)),
                   gu_slab, gu_slab, d_slab],
        out_shape=[jax.ShapeDtypeStruct((n_seq, seq, D_MODEL), F32),
                   jax.ShapeDtypeStruct((n_seq, 2, CONV_DIM), F32),
                   jax.ShapeDtypeStruct((n_seq, RET_HEADS, HEAD_DIM, HEAD_DIM), F32),
                   jax.ShapeDtypeStruct(w_gate.shape, BF16), jax.ShapeDtypeStruct(w_up.shape, BF16),
                   jax.ShapeDtypeStruct(w_down.shape, BF16)],
        scratch_shapes=[
            pltpu.VMEM((TILE, LANES), F32),
            pltpu.VMEM((TILE, LANES), F32),
            pltpu.VMEM((TILE, RET_DIM), F32),
            pltpu.VMEM((TILE, RET_DIM), F32),
            pltpu.VMEM((RET_HEADS, TILE, TILE), F32),
            pltpu.VMEM((GROUP, GROUP), F32),
            pltpu.VMEM((1, RET_DIM), F32),
            pltpu.VMEM((RET_DIM, TILE), F32),
            pltpu.VMEM((n_seq, 2, CONV_DIM), F32),
            pltpu.VMEM((n_seq, N_GROUPS, GROUP, GROUP), F32),
            pltpu.VMEM((SEQS_PER_STEP, TILE + 8, CONV_DIM), F32),
        ],
        compiler_params=pltpu.CompilerParams(
            dimension_semantics=("arbitrary", "arbitrary"), vmem_limit_bytes=VMEM_LIMIT),
        name="prompt_mixer",
    )(x_prompt, meta, g1, w_in, w_vg_t, w_conv, gain, w_out, lg_lane, inv_lane, w_gate, w_up, w_down)


def _ffn_kernel(xp_ref, xs_ref, qd_ref, kd_ref, v_ref, ointra_ref, zg_ref, cvo_ref, s_ref, gain_ref,
                sdec_ref, wout_ref, g2_ref, wg_ref, wu_ref, wd_ref, gf_ref,
                yp_ref, ys_ref, snew_ref, o_ref, *, prompt_steps, n_seq, dec_seq, e_piece):
    i = pl.program_id(0)
    L, B = dec_seq, n_seq
    pieces_per_head = HEAD_DIM // e_piece

    def chunk_phases(load, y_ref, rows):
        x = load(rows)
        xn = _rmsnorm(x, g2_ref[...]).astype(BF16)
        yield
        gate = _dot(xn, wg_ref[...])
        yield
        up = _dot(xn, wu_ref[...])
        yield
        hidden = (_silu(gate) * up).astype(BF16)
        yield
        y_ref[rows, :] = _rmsnorm(x + _dot(hidden, wd_ref[...]), gf_ref[...])

    def ffn_chunks(load, y_ref, extra=()):
        chunks = [slice(r, r + FFN_CHUNK) for r in range(0, y_ref.shape[0], FFN_CHUNK)]
        _trace_staggered([chunk_phases(load, y_ref, rows) for rows in chunks] + list(extra), FFN_PHASE_LAG)

    def state_piece():
        h = i // pieces_per_head
        base = pl.multiple_of(h * HEAD_DIM, HEAD_DIM)
        rows = pl.ds(pl.multiple_of(base + (i % pieces_per_head) * e_piece, e_piece), e_piece)
        qd_h = qd_ref[pl.ds(base, HEAD_DIM), :]
        kd_h = kd_ref[pl.ds(base, HEAD_DIM), :]
        state_decay = sdec_ref[h]
        v_blk = [v_ref[rows, j * B:(j + 1) * B] for j in range(L)]
        cross = [jnp.zeros((e_piece, B), F32) for _ in range(L)]
        for d in range(HEAD_DIM):
            s_de = s_ref[0, d]
            new = s_de * state_decay
            for t in range(L):
                lanes = slice(t * B, (t + 1) * B)
                cross[t] = cross[t] + qd_h[d:d + 1, lanes] * s_de
                new = new + kd_h[d:d + 1, lanes] * v_blk[t]
            snew_ref[0, d] = new
        for t in range(L):
            o_ref[rows, t * B:(t + 1) * B] = o_ref[rows, t * B:(t + 1) * B] + cross[t]
        yield

    @pl.when(i == 0)
    def _seed_retention_output():
        o_ref[...] = ointra_ref[...]

    @pl.when(i < prompt_steps)
    def _prompt_rows():
        ffn_chunks(lambda rows: xp_ref[rows, :], yp_ref, extra=[state_piece()])

    @pl.when(i == prompt_steps)
    def _sample_rows():
        gain_col = jnp.broadcast_to(gain_ref[...], (L * B, RET_DIM)).T
        for hd in range(RET_HEADS):
            r0 = hd * HEAD_DIM
            o_h = o_ref[r0:r0 + HEAD_DIM, :]
            mu = jnp.sum(o_h, axis=0, keepdims=True) * (1.0 / HEAD_DIM)
            dlt = o_h - mu
            var = jnp.sum(dlt * dlt, axis=0, keepdims=True) * (1.0 / HEAD_DIM)
            o_ref[r0:r0 + HEAD_DIM, :] = dlt * lax.rsqrt(var + GN_EPS) * gain_col[r0:r0 + HEAD_DIM, :]
        ret_out = (_silu(zg_ref[...]) * o_ref[...].T).astype(BF16)
        x1 = (xs_ref[...] + _dot(cvo_ref[...], wout_ref[:CONV_DIM, :])
              + _dot(ret_out, wout_ref[CONV_DIM:, :]))
        ffn_chunks(lambda rows: x1[rows], ys_ref)


def _ffn(x_p, xs, qd_t, kd_t, v_t, o_intra, zg, cvo, s4, gain, w_out_b, g2, w_gate, w_up, w_down, gf,
         dec_seq):
    prompt_steps = x_p.shape[0] // FFN_TILE
    n_seq = xs.shape[0] // dec_seq
    assert prompt_steps % RET_HEADS == 0
    e_piece = HEAD_DIM // (prompt_steps // RET_HEADS)
    assert e_piece % 8 == 0 and HEAD_DIM % e_piece == 0
    pieces_per_head = HEAD_DIM // e_piece
    sdec = jnp.asarray(np.exp(np.float32(dec_seq) * _LOG_GAMMA))
    full = lambda a: pl.BlockSpec(a.shape, lambda i: (0,) * a.ndim, pipeline_mode=pl.Buffered(1))
    prompt_tile = pl.BlockSpec((FFN_TILE, D_MODEL), lambda i: (jnp.minimum(i, prompt_steps - 1), 0))
    sample_out = pl.BlockSpec(xs.shape, lambda i: (0, 0))

    def piece_index(i):
        piece = jnp.minimum(i, prompt_steps - 1)
        return (piece // pieces_per_head, 0, piece % pieces_per_head, 0)

    state_piece = pl.BlockSpec((1, HEAD_DIM, e_piece, n_seq), piece_index)
    return pl.pallas_call(
        functools.partial(_ffn_kernel, prompt_steps=prompt_steps, n_seq=n_seq, dec_seq=dec_seq,
                          e_piece=e_piece),
        grid=(prompt_steps + 1,),
        in_specs=[prompt_tile, full(xs), full(qd_t), full(kd_t), full(v_t), full(o_intra), full(zg),
                  full(cvo), state_piece, full(gain), pl.BlockSpec(memory_space=pltpu.SMEM), full(w_out_b),
                  full(g2),
                  full(w_gate), full(w_up), full(w_down), full(gf)],
        out_specs=[prompt_tile, sample_out, state_piece],
        out_shape=[jax.ShapeDtypeStruct(x_p.shape, F32), jax.ShapeDtypeStruct(xs.shape, F32),
                   jax.ShapeDtypeStruct(s4.shape, F32)],
        scratch_shapes=[pltpu.VMEM(o_intra.shape, F32)],
        compiler_params=pltpu.CompilerParams(
            dimension_semantics=("arbitrary",), vmem_limit_bytes=VMEM_LIMIT),
        name="ffn",
    )(x_p, xs, qd_t, kd_t, v_t, o_intra, zg, cvo, s4, gain, sdec, w_out_b, g2, w_gate, w_up, w_down, gf)


def _gamma_pow(head, power):
    return float(np.exp(np.float32(power) * _LOG_GAMMA[head]))


def _sample_front_kernel(x_ref, st_ref, g1_ref, win32_ref, wvg32_ref, wconv_ref, wout32_ref, inv_ref,
                         convst_ref, qd_ref, kd_ref, v_ref, o_ref, zg_ref, cvo_ref,
                         win_ref, wout_ref, wvgt_ref, xn_ref, z_ref, *, n_seq, dec_seq, n_steps):
    s = pl.program_id(0)
    L, B = dec_seq, n_seq
    k_scale = HEAD_DIM ** -0.5

    @pl.when(s == 0)
    def _normalize():
        xn = _rmsnorm(x_ref[...], g1_ref[...]).astype(BF16)
        for c in range(n_steps):
            xn_ref[c] = xn[:, c * CAST_ROWS:(c + 1) * CAST_ROWS]

    w_slab = win32_ref[...].astype(BF16)
    win_ref[...] = w_slab
    wout_ref[...] = wout32_ref[...].astype(BF16)
    wvgt_ref[...] = wvg32_ref[...].T.astype(BF16)
    partial = _dot(xn_ref[s], w_slab)

    @pl.when(s == 0)
    def _first_slab():
        z_ref[...] = partial

    @pl.when(s > 0)
    def _next_slab():
        z_ref[...] = z_ref[...] + partial

    @pl.when(s == n_steps - 1)
    def _dense_front():
        def proj(off):
            return z_ref[:, off:off + 512]

        zb = proj(OFF_B)
        u = proj(OFF_C) * proj(OFF_H)
        ext = [st_ref[:, :CONV_DIM], st_ref[:, CONV_DIM:]] + [u[i * B:(i + 1) * B] for i in range(L)]
        wc = wconv_ref[...]
        for i in range(L):
            conv = wc[0:1] * ext[i] + wc[1:2] * ext[i + 1] + wc[2:3] * ext[i + 2]
            cvo_ref[i * B:(i + 1) * B, :] = (zb[i * B:(i + 1) * B] * conv).astype(BF16)
        convst_ref[...] = jnp.concatenate([ext[L], ext[L + 1]], axis=1)
        zg_ref[...] = proj(OFF_G)

        q_t = proj(OFF_Q).T
        k_t = proj(OFF_K).T
        v_t = proj(OFF_V).T
        v_ref[...] = v_t
        inv = jnp.broadcast_to(inv_ref[...], (B, LANES)).T[:HALF_DIM]
        cos_sin = [(jnp.cos(float(PAST_LEN + i) * inv), jnp.sin(float(PAST_LEN + i) * inv))
                   for i in range(L)]

        def rope(t, r0, i):
            cos, sin = cos_sin[i]
            t1 = t[r0:r0 + HALF_DIM, i * B:(i + 1) * B]
            t2 = t[r0 + HALF_DIM:r0 + HEAD_DIM, i * B:(i + 1) * B]
            return jnp.concatenate([t1 * cos - t2 * sin, t2 * cos + t1 * sin], axis=0)

        for hd in range(RET_HEADS):
            r0 = hd * HEAD_DIM
            qr = [rope(q_t, r0, i) for i in range(L)]
            kr = [rope(k_t, r0, i) for i in range(L)]
            for i in range(L):
                lanes = slice(i * B, (i + 1) * B)
                qd_ref[r0:r0 + HEAD_DIM, lanes] = qr[i] * _gamma_pow(hd, i + 1)
                kd_ref[r0:r0 + HEAD_DIM, lanes] = kr[i] * (_gamma_pow(hd, L - 1 - i) * k_scale)
                intra = jnp.zeros((HEAD_DIM, B), F32)
                for j in range(i + 1):
                    score = jnp.sum(qr[i] * kr[j], axis=0, keepdims=True) * (_gamma_pow(hd, i - j) * k_scale)
                    intra = intra + score * v_t[r0:r0 + HEAD_DIM, j * B:(j + 1) * B]
                o_ref[r0:r0 + HEAD_DIM, lanes] = intra


def _sample_front(xs, st, g1, w_in, w_conv, w_out, inv_lane, dec_seq):
    rows = xs.shape[0]
    n_seq = rows // dec_seq
    n_steps = w_in.shape[0] // CAST_ROWS
    assert w_out.shape[0] == n_steps * CAST_ROWS and 2 * RET_DIM == n_steps * CAST_ROWS
    full = lambda a: pl.BlockSpec(a.shape, lambda s: (0,) * a.ndim, pipeline_mode=pl.Buffered(1))
    const = lambda shape: pl.BlockSpec(shape, lambda s: (0, 0))
    feat = (RET_DIM, rows)
    return pl.pallas_call(
        functools.partial(_sample_front_kernel, n_seq=n_seq, dec_seq=dec_seq, n_steps=n_steps),
        grid=(n_steps,),
        in_specs=[full(xs), full(st), full(g1),
                  pl.BlockSpec((CAST_ROWS, w_in.shape[1]), lambda s: (s, 0)),
                  pl.BlockSpec((w_in.shape[0], CAST_ROWS), lambda s: (0, OFF_V // CAST_ROWS + s)),
                  full(w_conv),
                  pl.BlockSpec((CAST_ROWS, w_out.shape[1]), lambda s: (s, 0)),
                  full(inv_lane)],
        out_specs=[const(st.shape), const(feat), const(feat), const(feat), const(feat),
                   const((rows, RET_DIM)), const((rows, CONV_DIM)),
                   pl.BlockSpec((CAST_ROWS, w_in.shape[1]), lambda s: (s, 0)),
                   pl.BlockSpec((CAST_ROWS, w_out.shape[1]), lambda s: (s, 0)),
                   pl.BlockSpec((CAST_ROWS, w_in.shape[0]), lambda s: (s, 0))],
        out_shape=[jax.ShapeDtypeStruct(st.shape, F32),
                   jax.ShapeDtypeStruct(feat, F32), jax.ShapeDtypeStruct(feat, F32),
                   jax.ShapeDtypeStruct(feat, F32), jax.ShapeDtypeStruct(feat, F32),
                   jax.ShapeDtypeStruct((rows, RET_DIM), F32), jax.ShapeDtypeStruct((rows, CONV_DIM), BF16),
                   jax.ShapeDtypeStruct(w_in.shape, BF16), jax.ShapeDtypeStruct(w_out.shape, BF16),
                   jax.ShapeDtypeStruct((2 * RET_DIM, w_in.shape[0]), BF16)],
        scratch_shapes=[
            pltpu.VMEM((n_steps, rows, CAST_ROWS), BF16),
            pltpu.VMEM((rows, w_in.shape[1]), F32),
        ],
        compiler_params=pltpu.CompilerParams(
            dimension_semantics=("arbitrary",), vmem_limit_bytes=VMEM_LIMIT),
        name="sample_front",
    )(xs, st, g1, w_in, w_in, w_conv, w_out, inv_lane)


def kernel(x_prompt, x_sample, state_conv, state_ret, meta_tokens, norm1_g, w_in, w_conv, ret_norm_g,
           w_out, norm2_g, w_gate, w_up, w_down, final_norm_g):
    n_p, seq, _ = x_prompt.shape
    n_s, dec_seq, _ = x_sample.shape
    rows_s = n_s * dec_seq
    assert norm1_g.shape[0] == 1 and seq % TILE == 0 and n_p % SEQS_PER_STEP == 0
    assert (n_p * seq) % FFN_TILE == 0
    assert n_s % LANES == 0 and rows_s % FFN_CHUNK == 0

    g1, g2, gf = norm1_g[0][None], norm2_g[0][None], final_norm_g[None]
    lg_lane = jnp.asarray(_LOG_GAMMA[np.arange(RET_DIM) // HEAD_DIM][None])
    inv = ROPE_BASE ** (-jnp.arange(HALF_DIM, dtype=F32) / HALF_DIM)
    inv_lane = jnp.tile(inv, LANES // HALF_DIM)[None]

    gain = ret_norm_g[0][None]

    st = state_conv[0].reshape(n_s, 2 * CONV_DIM)
    xs = jnp.transpose(x_sample, (1, 0, 2)).reshape(rows_s, D_MODEL)
    conv_s, qd_t, kd_t, v_t, o_intra, zg, cvo, w_in_b, w_out_b, w_vg_t = _sample_front(
        xs, st, g1, w_in[0], w_conv[0], w_out[0], inv_lane, dec_seq)

    x1_p, conv_p, ret_p, wg_b, wu_b, wd_b = _prompt_mixer(
        x_prompt, meta_tokens, g1, w_in_b, w_vg_t, w_conv[0], gain, w_out_b, lg_lane, inv_lane,
        w_gate[0], w_up[0], w_down[0])

    s4 = jnp.transpose(state_ret[0], (1, 2, 3, 0))
    y_p, y_s, s4_new = _ffn(x1_p.reshape(n_p * seq, D_MODEL), xs, qd_t, kd_t, v_t, o_intra, zg, cvo, s4,
                            gain, w_out_b, g2, wg_b, wu_b, wd_b, gf, dec_seq)
    y_prompt = y_p.reshape(n_p, seq, D_MODEL)
    y_sample = jnp.transpose(y_s.reshape(dec_seq, n_s, D_MODEL), (1, 0, 2))
    ret_s = jnp.transpose(s4_new, (3, 0, 1, 2))

    return (y_prompt, y_sample, conv_p[None], ret_p[None], conv_s.reshape(1, n_s, 2, CONV_DIM), ret_s[None])
```

```python
import functools

import numpy as np
import jax
import jax.numpy as jnp
from jax import lax
from jax.experimental import pallas as pl
from jax.experimental.pallas import tpu as pltpu

D_MODEL = 1024
N_META = 16
CONV_DIM = 512
RET_HEADS = 8
HEAD_DIM = 64
HALF_DIM = HEAD_DIM // 2
RET_DIM = RET_HEADS * HEAD_DIM
D_FF = 2816
PAST_LEN = 16384
ROPE_BASE = 10000.0
EPS = 1e-6
GN_EPS = 1e-5

OFF_B, OFF_C, OFF_H, OFF_Q, OFF_K, OFF_V, OFF_G = (i * 512 for i in range(7))

LANES = 128
GROUP = 256
HEADS_PER_GROUP = GROUP // HEAD_DIM
N_GROUPS = RET_DIM // GROUP
TILE = 256
SEQS_PER_STEP = 4
FFN_TILE = 512
FFN_CHUNK = 256
CAST_ROWS = 512
PHASE_LAG = 3
FFN_PHASE_LAG = 1
BF16_ROWS = 16
VMEM_LIMIT = 56 * 1024 * 1024

F32 = jnp.float32
BF16 = jnp.bfloat16

_LOG_GAMMA = np.log1p(-(2.0 ** (-5.0 - np.arange(RET_HEADS)))).astype(np.float32)


def _dot(a, b):
    return jnp.dot(a, b, preferred_element_type=F32)


def _dot_nt(a, b):
    return lax.dot_general(a, b, (((1,), (1,)), ((), ())), preferred_element_type=F32)


def _dot_tn(a, b):
    return lax.dot_general(a, b, (((0,), (0,)), ((), ())), preferred_element_type=F32)


def _rmsnorm(x, g):
    ms = jnp.mean(x * x, axis=-1, keepdims=True)
    return x * lax.rsqrt(ms + EPS) * g


def _silu(x):
    return x * jax.nn.sigmoid(x)


def _iota(shape, dim):
    return lax.broadcasted_iota(jnp.int32, shape, dim)


def _first_half_lanes():
    return (_iota((1, LANES), 1) & (HEAD_DIM - 1)) < HALF_DIM


def _rope(t, cos, signed_sin):
    first_half = _first_half_lanes()
    out = []
    for b in range(t.shape[1] // LANES):
        blk = t[:, b * LANES:(b + 1) * LANES]
        partner = jnp.where(first_half, pltpu.roll(blk, LANES - HALF_DIM, axis=1),
                            pltpu.roll(blk, HALF_DIM, axis=1))
        out.append(blk * cos + partner * signed_sin)
    return jnp.concatenate(out, axis=1)


def _ones_block():
    r = _iota((GROUP, GROUP), 0) >> 6
    c = _iota((GROUP, GROUP), 1) >> 6
    return jnp.where(r == c, 1.0, 0.0).astype(BF16)


def _trace_staggered(phase_iters, lag):
    live = list(range(len(phase_iters)))
    tick = 0
    while live:
        for k in list(live):
            if tick >= lag * k:
                try:
                    next(phase_iters[k])
                except StopIteration:
                    live.remove(k)
        tick += 1


def _prompt_mixer_kernel(x_ref, meta_ref, g1_ref, win_ref, wvgt_ref, wconv_ref, gain_ref, wout_ref,
                         lg_ref, inv_ref, wg32_ref, wu32_ref, wd32_ref,
                         x1_ref, convst_ref, retst_ref, wg_ref, wu_ref, wd_ref,
                         cos_ref, sin_ref, qdec_ref, kdec_ref, dmask_ref,
                         smask_ref, sdec_ref, gaincol_ref, tail_ref, state_ref, ubuf_ref, *, n_tiles):
    j = pl.program_id(0)
    p = pl.program_id(1)
    C = TILE
    k_scale = HEAD_DIM ** -0.5

    def proj(xn, off):
        return _dot(xn, win_ref[:, off:off + 512])

    @pl.when((j == 0) & (p == 0))
    def _init():
        i_f = _iota((C, 1), 0).astype(F32)
        lg = lg_ref[...]
        qdec_ref[...] = jnp.exp((i_f + 1.0) * lg)
        kdec_ref[...] = jnp.exp((C - 1.0 - i_f) * lg) * k_scale
        sdec_ref[...] = jnp.exp(C * lg)
        gaincol_ref[...] = jnp.broadcast_to(gain_ref[...], (C, RET_DIM)).T
        diff = (_iota((C, C), 1) - _iota((C, C), 0)).astype(F32)
        for h in range(RET_HEADS):
            dmask_ref[h] = jnp.where(
                diff >= 0, jnp.exp(jnp.maximum(diff, 0.0) * float(_LOG_GAMMA[h])) * k_scale, 0.0)
        smask = _ones_block().astype(F32)
        smask_ref[...] = smask

        xm = _rmsnorm(meta_ref[...], g1_ref[...]).astype(BF16)
        um = proj(xm, OFF_C) * proj(xm, OFF_H)
        vm = proj(xm, OFF_V).astype(BF16)
        m_f = _iota((N_META, 1), 0).astype(F32)
        ang = m_f * inv_ref[...]
        sin = jnp.sin(ang)
        kmr = _rope(proj(xm, OFF_K), jnp.cos(ang), jnp.where(_first_half_lanes(), -sin, sin))
        kmd = (kmr * (jnp.exp((N_META - 1.0 - m_f) * lg) * k_scale)).astype(BF16)
        for b in range(tail_ref.shape[0]):
            tail_ref[b] = um[N_META - 2:N_META, :]
            for g in range(N_GROUPS):
                sl = slice(g * GROUP, (g + 1) * GROUP)
                state_ref[b, g] = _dot_tn(vm[:, sl], kmd[:, sl]) * smask

    @pl.when(p == 0)
    def _rope_tables():
        pos = (N_META + j * C + _iota((C, LANES), 0)).astype(F32)
        ang = pos * inv_ref[...]
        sin = jnp.sin(ang)
        cos_ref[...] = jnp.cos(ang)
        sin_ref[...] = jnp.where(_first_half_lanes(), -sin, sin)

    cos = cos_ref[...]
    sin = sin_ref[...]
    head_of_lane = _iota((1, GROUP), 1) >> 6
    smask = smask_ref[...]
    wc = wconv_ref[...]

    def mix_sequence(slot):
        n = p * SEQS_PER_STEP + slot
        x = x_ref[slot]
        xn = _rmsnorm(x, g1_ref[...]).astype(BF16)

        zb = proj(xn, OFF_B)
        yield
        u = proj(xn, OFF_C) * proj(xn, OFF_H)
        ubuf_ref[slot, 6:8, :] = tail_ref[n]
        ubuf_ref[slot, 8:8 + C, :] = u
        conv = wc[0:1] * ubuf_ref[slot, 6:6 + C, :] + wc[1:2] * ubuf_ref[slot, 7:7 + C, :] + wc[2:3] * u
        conv_out = (zb * conv).astype(BF16)
        new_tail = u[C - 2:C, :]
        tail_ref[n] = new_tail

        yield

        qr = _rope(proj(xn, OFF_Q), cos, sin)
        yield
        kr = _rope(proj(xn, OFF_K), cos, sin)
        yield
        vg_t = _dot_nt(wvgt_ref[...], xn)
        v_t = vg_t[:RET_DIM].astype(BF16)
        yield
        qb = qr.astype(BF16)
        kb = kr.astype(BF16)
        qd = (qr * qdec_ref[...]).astype(BF16)
        kd = (kr * kdec_ref[...]).astype(BF16)
        o_parts = []
        new_states = []
        for g in range(N_GROUPS):
            sl = slice(g * GROUP, (g + 1) * GROUP)
            state_t = state_ref[n, g]
            cross_t = _dot_nt(state_t.astype(BF16), qd[:, sl])
            k_heads = jnp.concatenate(
                [jnp.where(head_of_lane == hh, kb[:, sl], jnp.zeros_like(kb[:, sl]))
                 for hh in range(HEADS_PER_GROUP)], axis=0)
            decay = dmask_ref[g * HEADS_PER_GROUP:(g + 1) * HEADS_PER_GROUP].reshape(HEADS_PER_GROUP * C, C)
            scores_t = (_dot_nt(k_heads, qb[:, sl]) * decay).astype(BF16)
            for hh in range(HEADS_PER_GROUP):
                h = g * HEADS_PER_GROUP + hh
                rows = slice(hh * HEAD_DIM, (hh + 1) * HEAD_DIM)
                o_parts.append(cross_t[rows] + _dot(v_t[h * HEAD_DIM:(h + 1) * HEAD_DIM],
                                                    scores_t[hh * C:(hh + 1) * C]))
            new_state = state_t * sdec_ref[:, sl] + _dot(v_t[sl], kd[:, sl]) * smask
            state_ref[n, g] = new_state
            new_states.append(new_state)
            yield

        normed = []
        for h, o_h in enumerate(o_parts):
            mu = jnp.sum(o_h, axis=0, keepdims=True) * (1.0 / HEAD_DIM)
            dlt = o_h - mu
            var = jnp.sum(dlt * dlt, axis=0, keepdims=True) * (1.0 / HEAD_DIM)
            normed.append(dlt * lax.rsqrt(var + GN_EPS) * gaincol_ref[h * HEAD_DIM:(h + 1) * HEAD_DIM, :])
        ret_out_t = (_silu(vg_t[RET_DIM:]) * jnp.concatenate(normed, axis=0)).astype(BF16)
        yield

        x1_ref[slot] = (x + _dot(conv_out, wout_ref[:CONV_DIM, :])
                        + _dot_tn(ret_out_t, wout_ref[CONV_DIM:, :]))
        finals[slot] = (new_tail, new_states)

    finals = [None] * SEQS_PER_STEP
    _trace_staggered([mix_sequence(slot) for slot in range(SEQS_PER_STEP)], PHASE_LAG)

    wg_ref[...] = wg32_ref[...].astype(BF16)
    wu_ref[...] = wu32_ref[...].astype(BF16)
    wd_ref[...] = wd32_ref[...].astype(BF16)

    @pl.when(j == n_tiles - 1)
    def _final_states():
        for slot, (new_tail, new_states) in enumerate(finals):
            n = p * SEQS_PER_STEP + slot
            convst_ref[n] = new_tail
            for g in range(N_GROUPS):
                state = new_states[g].T
                for hh in range(HEADS_PER_GROUP):
                    blk = slice(hh * HEAD_DIM, (hh + 1) * HEAD_DIM)
                    retst_ref[n, g * HEADS_PER_GROUP + hh] = state[blk, blk]


def _prompt_mixer(x_prompt, meta, g1, w_in, w_vg_t, w_conv, gain, w_out, lg_lane, inv_lane,
                  w_gate, w_up, w_down):
    n_seq, seq, _ = x_prompt.shape
    n_tiles = seq // TILE
    pairs = n_seq // SEQS_PER_STEP
    n_steps = n_tiles * pairs
    const2 = lambda j, p: (0, 0)
    full = lambda a: pl.BlockSpec(a.shape, const2, pipeline_mode=pl.Buffered(1))
    gu_rows = w_gate.shape[0] // n_steps
    d_rows = w_down.shape[0] // (n_steps // 2)
    assert gu_rows * n_steps == w_gate.shape[0] and d_rows * (n_steps // 2) == w_down.shape[0]
    assert gu_rows % BF16_ROWS == 0 and d_rows % BF16_ROWS == 0
    gu_slab = pl.BlockSpec((gu_rows, w_gate.shape[1]), lambda j, p: (j * pairs + p, 0))
    d_slab = pl.BlockSpec((d_rows, w_down.shape[1]), lambda j, p: ((j * pairs + p) // 2, 0))
    return pl.pallas_call(
        functools.partial(_prompt_mixer_kernel, n_tiles=n_tiles),
        grid=(n_tiles, pairs),
        in_specs=[pl.BlockSpec((SEQS_PER_STEP, TILE, D_MODEL), lambda j, p: (p, j, 0)),
                  full(meta), full(g1), full(w_in), full(w_vg_t), full(w_conv), full(gain),
                  full(w_out), full(lg_lane), full(inv_lane), gu_slab, gu_slab, d_slab],
        out_specs=[pl.BlockSpec((SEQS_PER_STEP, TILE, D_MODEL), lambda j, p: (p, j, 0)),
                   pl.BlockSpec((n_seq, 2, CONV_DIM), lambda j, p: (0, 0, 0)),
                   pl.BlockSpec((n_seq, RET_HEADS, HEAD_DIM, HEAD_DIM), lambda j, p: (0, 0, 0, 0)),
                   gu_slab, gu_slab, d_slab],
        out_shape=[jax.ShapeDtypeStruct((n_seq, seq, D_MODEL), F32),
                   jax.ShapeDtypeStruct((n_seq, 2, CONV_DIM), F32),
                   jax.ShapeDtypeStruct((n_seq, RET_HEADS, HEAD_DIM, HEAD_DIM), F32),
                   jax.ShapeDtypeStruct(w_gate.shape, BF16), jax.ShapeDtypeStruct(w_up.shape, BF16),
                   jax.ShapeDtypeStruct(w_down.shape, BF16)],
        scratch_shapes=[
            pltpu.VMEM((TILE, LANES), F32),
            pltpu.VMEM((TILE, LANES), F32),
            pltpu.VMEM((TILE, RET_DIM), F32),
            pltpu.VMEM((TILE, RET_DIM), F32),
            pltpu.VMEM((RET_HEADS, TILE, TILE), F32),
            pltpu.VMEM((GROUP, GROUP), F32),
            pltpu.VMEM((1, RET_DIM), F32),
            pltpu.VMEM((RET_DIM, TILE), F32),
            pltpu.VMEM((n_seq, 2, CONV_DIM), F32),
            pltpu.VMEM((n_seq, N_GROUPS, GROUP, GROUP), F32),
            pltpu.VMEM((SEQS_PER_STEP, TILE + 8, CONV_DIM), F32),
        ],
        compiler_params=pltpu.CompilerParams(
            dimension_semantics=("arbitrary", "arbitrary"), vmem_limit_bytes=VMEM_LIMIT),
        name="prompt_mixer",
    )(x_prompt, meta, g1, w_in, w_vg_t, w_conv, gain, w_out, lg_lane, inv_lane, w_gate, w_up, w_down)


def _ffn_kernel(xp_ref, xs_ref, qd_ref, kd_ref, v_ref, ointra_ref, zg_ref, cvo_ref, s_ref, gain_ref,
                sdec_ref, wout_ref, g2_ref, wg_ref, wu_ref, wd_ref, gf_ref,
                yp_ref, ys_ref, snew_ref, o_ref, x2_ref, *, prompt_steps, n_seq, dec_seq, e_piece):
    i = pl.program_id(0)
    L, B = dec_seq, n_seq
    pieces_per_head = HEAD_DIM // e_piece

    def chunk_phases(load, finish, rows):
        x = load(rows)
        xn = _rmsnorm(x, g2_ref[...]).astype(BF16)
        yield
        gate = _dot(xn, wg_ref[...])
        yield
        up = _dot(xn, wu_ref[...])
        yield
        hidden = (_silu(gate) * up).astype(BF16)
        yield
        finish(rows, x + _dot(hidden, wd_ref[...]))

    def ffn_chunks(load, finish, n_rows, before=(), after=()):
        chunks = [slice(r, r + FFN_CHUNK) for r in range(0, n_rows, FFN_CHUNK)]
        _trace_staggered(list(before) + [chunk_phases(load, finish, rows) for rows in chunks] + list(after),
                         FFN_PHASE_LAG)

    def previous_tile_norm():
        yp_ref[...] = _rmsnorm(x2_ref[...], gf_ref[...])
        yield

    def stash_prenorm(rows, x2):
        x2_ref[rows, :] = x2

    def write_sample(rows, x2):
        ys_ref[rows, :] = _rmsnorm(x2, gf_ref[...])

    def state_piece():
        h = i // pieces_per_head
        base = pl.multiple_of(h * HEAD_DIM, HEAD_DIM)
        rows = pl.ds(pl.multiple_of(base + (i % pieces_per_head) * e_piece, e_piece), e_piece)
        qd_h = qd_ref[pl.ds(base, HEAD_DIM), :]
        kd_h = kd_ref[pl.ds(base, HEAD_DIM), :]
        state_decay = sdec_ref[h]
        v_blk = [v_ref[rows, j * B:(j + 1) * B] for j in range(L)]
        cross = [jnp.zeros((e_piece, B), F32) for _ in range(L)]
        for d in range(HEAD_DIM):
            s_de = s_ref[0, d]
            new = s_de * state_decay
            for t in range(L):
                lanes = slice(t * B, (t + 1) * B)
                cross[t] = cross[t] + qd_h[d:d + 1, lanes] * s_de
                new = new + kd_h[d:d + 1, lanes] * v_blk[t]
            snew_ref[0, d] = new
        for t in range(L):
            o_ref[rows, t * B:(t + 1) * B] = o_ref[rows, t * B:(t + 1) * B] + cross[t]
        yield

    @pl.when(i == 0)
    def _seed():
        o_ref[...] = ointra_ref[...]
        x2_ref[...] = jnp.zeros_like(x2_ref)

    @pl.when(i < prompt_steps)
    def _prompt_rows():
        ffn_chunks(lambda rows: xp_ref[rows, :], stash_prenorm, FFN_TILE,
                   before=[previous_tile_norm()], after=[state_piece()])

    @pl.when(i == prompt_steps)
    def _sample_rows():
        gain_col = jnp.broadcast_to(gain_ref[...], (L * B, RET_DIM)).T
        for hd in range(RET_HEADS):
            r0 = hd * HEAD_DIM
            o_h = o_ref[r0:r0 + HEAD_DIM, :]
            mu = jnp.sum(o_h, axis=0, keepdims=True) * (1.0 / HEAD_DIM)
            dlt = o_h - mu
            var = jnp.sum(dlt * dlt, axis=0, keepdims=True) * (1.0 / HEAD_DIM)
            o_ref[r0:r0 + HEAD_DIM, :] = dlt * lax.rsqrt(var + GN_EPS) * gain_col[r0:r0 + HEAD_DIM, :]
        ret_out = (_silu(zg_ref[...]) * o_ref[...].T).astype(BF16)
        x1 = (xs_ref[...] + _dot(cvo_ref[...], wout_ref[:CONV_DIM, :])
              + _dot(ret_out, wout_ref[CONV_DIM:, :]))
        ffn_chunks(lambda rows: x1[rows], write_sample, L * B, before=[previous_tile_norm()])


def _ffn(x_p, xs, qd_t, kd_t, v_t, o_intra, zg, cvo, s4, gain, w_out_b, g2, w_gate, w_up, w_down, gf,
         dec_seq):
    prompt_steps = x_p.shape[0] // FFN_TILE
    n_seq = xs.shape[0] // dec_seq
    assert prompt_steps % RET_HEADS == 0
    e_piece = HEAD_DIM // (prompt_steps // RET_HEADS)
    assert e_piece % 8 == 0 and HEAD_DIM % e_piece == 0
    pieces_per_head = HEAD_DIM // e_piece
    sdec = jnp.asarray(np.exp(np.float32(dec_seq) * _LOG_GAMMA))
    full = lambda a: pl.BlockSpec(a.shape, lambda i: (0,) * a.ndim, pipeline_mode=pl.Buffered(1))
    prompt_tile = pl.BlockSpec((FFN_TILE, D_MODEL), lambda i: (jnp.minimum(i, prompt_steps - 1), 0))
    prompt_out = pl.BlockSpec((FFN_TILE, D_MODEL), lambda i: (jnp.maximum(i - 1, 0), 0))
    sample_out = pl.BlockSpec(xs.shape, lambda i: (0, 0))

    def piece_index(i):
        piece = jnp.minimum(i, prompt_steps - 1)
        return (piece // pieces_per_head, 0, piece % pieces_per_head, 0)

    state_piece = pl.BlockSpec((1, HEAD_DIM, e_piece, n_seq), piece_index)
    return pl.pallas_call(
        functools.partial(_ffn_kernel, prompt_steps=prompt_steps, n_seq=n_seq, dec_seq=dec_seq,
                          e_piece=e_piece),
        grid=(prompt_steps + 1,),
        in_specs=[prompt_tile, full(xs), full(qd_t), full(kd_t), full(v_t), full(o_intra), full(zg),
                  full(cvo), state_piece, full(gain), pl.BlockSpec(memory_space=pltpu.SMEM), full(w_out_b),
                  full(g2),
                  full(w_gate), full(w_up), full(w_down), full(gf)],
        out_specs=[prompt_out, sample_out, state_piece],
        out_shape=[jax.ShapeDtypeStruct(x_p.shape, F32), jax.ShapeDtypeStruct(xs.shape, F32),
                   jax.ShapeDtypeStruct(s4.shape, F32)],
        scratch_shapes=[pltpu.VMEM(o_intra.shape, F32),
                        pltpu.VMEM((FFN_TILE, D_MODEL), F32)],
        compiler_params=pltpu.CompilerParams(
            dimension_semantics=("arbitrary",), vmem_limit_bytes=VMEM_LIMIT),
        name="ffn",
    )(x_p, xs, qd_t, kd_t, v_t, o_intra, zg, cvo, s4, gain, sdec, w_out_b, g2, w_gate, w_up, w_down, gf)


def _gamma_pow(head, power):
    return float(np.exp(np.float32(power) * _LOG_GAMMA[head]))


def _sample_front_kernel(x_ref, st_ref, g1_ref, win32_ref, wvg32_ref, wconv_ref, wout32_ref, inv_ref,
                         convst_ref, qd_ref, kd_ref, v_ref, o_ref, zg_ref, cvo_ref,
                         win_ref, wout_ref, wvgt_ref, *, n_seq, dec_seq):
    s = pl.program_id(0)
    L, B = dec_seq, n_seq
    k_scale = HEAD_DIM ** -0.5

    slab = pl.ds(pl.multiple_of(s * CAST_ROWS, CAST_ROWS), CAST_ROWS)
    win_ref[...] = win32_ref[slab, :].astype(BF16)
    wout_ref[...] = wout32_ref[slab, :].astype(BF16)
    wvgt_ref[...] = wvg32_ref[...].T.astype(BF16)

    @pl.when(s == 0)
    def _dense_front():
        xn = _rmsnorm(x_ref[...], g1_ref[...]).astype(BF16)

        def proj(off):
            return _dot(xn, win32_ref[:, off:off + 512].astype(BF16))

        zb = proj(OFF_B)
        u = proj(OFF_C) * proj(OFF_H)
        ext = [st_ref[:, :CONV_DIM], st_ref[:, CONV_DIM:]] + [u[i * B:(i + 1) * B] for i in range(L)]
        wc = wconv_ref[...]
        for i in range(L):
            conv = wc[0:1] * ext[i] + wc[1:2] * ext[i + 1] + wc[2:3] * ext[i + 2]
            cvo_ref[i * B:(i + 1) * B, :] = (zb[i * B:(i + 1) * B] * conv).astype(BF16)
        convst_ref[...] = jnp.concatenate([ext[L], ext[L + 1]], axis=1)
        zg_ref[...] = proj(OFF_G)

        q_t = proj(OFF_Q).T
        k_t = proj(OFF_K).T
        v_t = proj(OFF_V).T
        v_ref[...] = v_t
        inv = jnp.broadcast_to(inv_ref[...], (B, LANES)).T[:HALF_DIM]
        cos_sin = [(jnp.cos(float(PAST_LEN + i) * inv), jnp.sin(float(PAST_LEN + i) * inv))
                   for i in range(L)]

        def rope(t, r0, i):
            cos, sin = cos_sin[i]
            t1 = t[r0:r0 + HALF_DIM, i * B:(i + 1) * B]
            t2 = t[r0 + HALF_DIM:r0 + HEAD_DIM, i * B:(i + 1) * B]
            return jnp.concatenate([t1 * cos - t2 * sin, t2 * cos + t1 * sin], axis=0)

        for hd in range(RET_HEADS):
            r0 = hd * HEAD_DIM
            qr = [rope(q_t, r0, i) for i in range(L)]
            kr = [rope(k_t, r0, i) for i in range(L)]
            for i in range(L):
                lanes = slice(i * B, (i + 1) * B)
                qd_ref[r0:r0 + HEAD_DIM, lanes] = qr[i] * _gamma_pow(hd, i + 1)
                kd_ref[r0:r0 + HEAD_DIM, lanes] = kr[i] * (_gamma_pow(hd, L - 1 - i) * k_scale)
                intra = jnp.zeros((HEAD_DIM, B), F32)
                for j in range(i + 1):
                    score = jnp.sum(qr[i] * kr[j], axis=0, keepdims=True) * (_gamma_pow(hd, i - j) * k_scale)
                    intra = intra + score * v_t[r0:r0 + HEAD_DIM, j * B:(j + 1) * B]
                o_ref[r0:r0 + HEAD_DIM, lanes] = intra


def _sample_front(xs, st, g1, w_in, w_conv, w_out, inv_lane, dec_seq):
    rows = xs.shape[0]
    n_seq = rows // dec_seq
    n_steps = w_in.shape[0] // CAST_ROWS
    assert w_out.shape[0] == n_steps * CAST_ROWS and 2 * RET_DIM == n_steps * CAST_ROWS
    full = lambda a: pl.BlockSpec(a.shape, lambda s: (0,) * a.ndim, pipeline_mode=pl.Buffered(1))
    const = lambda shape: pl.BlockSpec(shape, lambda s: (0, 0))
    feat = (RET_DIM, rows)
    return pl.pallas_call(
        functools.partial(_sample_front_kernel, n_seq=n_seq, dec_seq=dec_seq),
        grid=(n_steps,),
        in_specs=[full(xs), full(st), full(g1), full(w_in),
                  pl.BlockSpec((w_in.shape[0], CAST_ROWS), lambda s: (0, OFF_V // CAST_ROWS + s)),
                  full(w_conv), full(w_out), full(inv_lane)],
        out_specs=[const(st.shape), const(feat), const(feat), const(feat), const(feat),
                   const((rows, RET_DIM)), const((rows, CONV_DIM)),
                   pl.BlockSpec((CAST_ROWS, w_in.shape[1]), lambda s: (s, 0)),
                   pl.BlockSpec((CAST_ROWS, w_out.shape[1]), lambda s: (s, 0)),
                   pl.BlockSpec((CAST_ROWS, w_in.shape[0]), lambda s: (s, 0))],
        out_shape=[jax.ShapeDtypeStruct(st.shape, F32),
                   jax.ShapeDtypeStruct(feat, F32), jax.ShapeDtypeStruct(feat, F32),
                   jax.ShapeDtypeStruct(feat, F32), jax.ShapeDtypeStruct(feat, F32),
                   jax.ShapeDtypeStruct((rows, RET_DIM), F32), jax.ShapeDtypeStruct((rows, CONV_DIM), BF16),
                   jax.ShapeDtypeStruct(w_in.shape, BF16), jax.ShapeDtypeStruct(w_out.shape, BF16),
                   jax.ShapeDtypeStruct((2 * RET_DIM, w_in.shape[0]), BF16)],
        compiler_params=pltpu.CompilerParams(
            dimension_semantics=("arbitrary",), vmem_limit_bytes=VMEM_LIMIT),
        name="sample_front",
    )(xs, st, g1, w_in, w_in, w_conv, w_out, inv_lane)


def kernel(x_prompt, x_sample, state_conv, state_ret, meta_tokens, norm1_g, w_in, w_conv, ret_norm_g,
           w_out, norm2_g, w_gate, w_up, w_down, final_norm_g):
    n_p, seq, _ = x_prompt.shape
    n_s, dec_seq, _ = x_sample.shape
    rows_s = n_s * dec_seq
    assert norm1_g.shape[0] == 1 and seq % TILE == 0 and n_p % SEQS_PER_STEP == 0
    assert (n_p * seq) % FFN_TILE == 0
    assert n_s % LANES == 0 and rows_s % FFN_CHUNK == 0

    g1, g2, gf = norm1_g[0][None], norm2_g[0][None], final_norm_g[None]
    lg_lane = jnp.asarray(_LOG_GAMMA[np.arange(RET_DIM) // HEAD_DIM][None])
    inv = ROPE_BASE ** (-jnp.arange(HALF_DIM, dtype=F32) / HALF_DIM)
    inv_lane = jnp.tile(inv, LANES // HALF_DIM)[None]

    gain = ret_norm_g[0][None]

    st = state_conv[0].reshape(n_s, 2 * CONV_DIM)
    xs = jnp.transpose(x_sample, (1, 0, 2)).reshape(rows_s, D_MODEL)
    conv_s, qd_t, kd_t, v_t, o_intra, zg, cvo, w_in_b, w_out_b, w_vg_t = _sample_front(
        xs, st, g1, w_in[0], w_conv[0], w_out[0], inv_lane, dec_seq)

    x1_p, conv_p, ret_p, wg_b, wu_b, wd_b = _prompt_mixer(
        x_prompt, meta_tokens, g1, w_in_b, w_vg_t, w_conv[0], gain, w_out_b, lg_lane, inv_lane,
        w_gate[0], w_up[0], w_down[0])

    s4 = jnp.transpose(state_ret[0], (1, 2, 3, 0))
    y_p, y_s, s4_new = _ffn(x1_p.reshape(n_p * seq, D_MODEL), xs, qd_t, kd_t, v_t, o_intra, zg, cvo, s4,
                            gain, w_out_b, g2, wg_b, wu_b, wd_b, gf, dec_seq)
    y_prompt = y_p.reshape(n_p, seq, D_MODEL)
    y_sample = jnp.transpose(y_s.reshape(dec_seq, n_s, D_MODEL), (1, 0, 2))
    ret_s = jnp.transpose(s4_new, (3, 0, 1, 2))

    return (y_prompt, y_sample, conv_p[None], ret_p[None], conv_s.reshape(1, n_s, 2, CONV_DIM), ret_s[None])
```

```python
import functools

import numpy as np
import jax
import jax.numpy as jnp
from jax import lax
from jax.experimental import pallas as pl
from jax.experimental.pallas import tpu as pltpu

D_MODEL = 1024
N_META = 16
CONV_DIM = 512
RET_HEADS = 8
HEAD_DIM = 64
HALF_DIM = HEAD_DIM // 2
RET_DIM = RET_HEADS * HEAD_DIM
D_FF = 2816
PAST_LEN = 16384
ROPE_BASE = 10000.0
EPS = 1e-6
GN_EPS = 1e-5

OFF_B, OFF_C, OFF_H, OFF_Q, OFF_K, OFF_V, OFF_G = (i * 512 for i in range(7))

LANES = 128
GROUP = 256
HEADS_PER_GROUP = GROUP // HEAD_DIM
N_GROUPS = RET_DIM // GROUP
TILE = 256
SEQS_PER_STEP = 4
FFN_TILE = 512
FFN_CHUNK = 256
CAST_ROWS = 512
PHASE_LAG = 3
FFN_PHASE_LAG = 1
BF16_ROWS = 16
VMEM_LIMIT = 56 * 1024 * 1024

F32 = jnp.float32
BF16 = jnp.bfloat16

_LOG_GAMMA = np.log1p(-(2.0 ** (-5.0 - np.arange(RET_HEADS)))).astype(np.float32)


def _dot(a, b):
    return jnp.dot(a, b, preferred_element_type=F32)


def _dot_nt(a, b):
    return lax.dot_general(a, b, (((1,), (1,)), ((), ())), preferred_element_type=F32)


def _dot_tn(a, b):
    return lax.dot_general(a, b, (((0,), (0,)), ((), ())), preferred_element_type=F32)


def _rmsnorm(x, g):
    ms = jnp.mean(x * x, axis=-1, keepdims=True)
    return x * lax.rsqrt(ms + EPS) * g


def _silu(x):
    return x * jax.nn.sigmoid(x)


def _iota(shape, dim):
    return lax.broadcasted_iota(jnp.int32, shape, dim)


def _first_half_lanes():
    return (_iota((1, LANES), 1) & (HEAD_DIM - 1)) < HALF_DIM


def _rope(t, cos, signed_sin):
    first_half = _first_half_lanes()
    out = []
    for b in range(t.shape[1] // LANES):
        blk = t[:, b * LANES:(b + 1) * LANES]
        partner = jnp.where(first_half, pltpu.roll(blk, LANES - HALF_DIM, axis=1),
                            pltpu.roll(blk, HALF_DIM, axis=1))
        out.append(blk * cos + partner * signed_sin)
    return jnp.concatenate(out, axis=1)


def _ones_block():
    r = _iota((GROUP, GROUP), 0) >> 6
    c = _iota((GROUP, GROUP), 1) >> 6
    return jnp.where(r == c, 1.0, 0.0).astype(BF16)


def _trace_staggered(phase_iters, lag):
    live = list(range(len(phase_iters)))
    tick = 0
    while live:
        for k in list(live):
            if tick >= lag * k:
                try:
                    next(phase_iters[k])
                except StopIteration:
                    live.remove(k)
        tick += 1


def _prompt_mixer_kernel(x_ref, meta_ref, g1_ref, win_ref, wvgt_ref, wconv_ref, gain_ref, wout_ref,
                         lg_ref, inv_ref, wg32_ref, wu32_ref, wd32_ref,
                         x1_ref, convst_ref, retst_ref, wg_ref, wu_ref, wd_ref,
                         cos_ref, sin_ref, qdec_ref, kdec_ref, dmask_ref,
                         smask_ref, sdec_ref, gaincol_ref, tail_ref, state_ref, ubuf_ref, *, n_tiles):
    j = pl.program_id(0)
    p = pl.program_id(1)
    C = TILE
    k_scale = HEAD_DIM ** -0.5

    def proj(xn, off):
        return _dot(xn, win_ref[:, off:off + 512])

    @pl.when((j == 0) & (p == 0))
    def _init():
        i_f = _iota((C, 1), 0).astype(F32)
        lg = lg_ref[...]
        qdec_ref[...] = jnp.exp((i_f + 1.0) * lg)
        kdec_ref[...] = jnp.exp((C - 1.0 - i_f) * lg) * k_scale
        sdec_ref[...] = jnp.exp(C * lg)
        gaincol_ref[...] = jnp.broadcast_to(gain_ref[...], (C, RET_DIM)).T
        diff = (_iota((C, C), 1) - _iota((C, C), 0)).astype(F32)
        for h in range(RET_HEADS):
            dmask_ref[h] = jnp.where(
                diff >= 0, jnp.exp(jnp.maximum(diff, 0.0) * float(_LOG_GAMMA[h])) * k_scale, 0.0)
        smask = _ones_block().astype(F32)
        smask_ref[...] = smask

        xm = _rmsnorm(meta_ref[...], g1_ref[...]).astype(BF16)
        um = proj(xm, OFF_C) * proj(xm, OFF_H)
        vm = proj(xm, OFF_V).astype(BF16)
        m_f = _iota((N_META, 1), 0).astype(F32)
        ang = m_f * inv_ref[...]
        sin = jnp.sin(ang)
        kmr = _rope(proj(xm, OFF_K), jnp.cos(ang), jnp.where(_first_half_lanes(), -sin, sin))
        kmd = (kmr * (jnp.exp((N_META - 1.0 - m_f) * lg) * k_scale)).astype(BF16)
        for b in range(tail_ref.shape[0]):
            tail_ref[b] = um[N_META - 2:N_META, :]
            for g in range(N_GROUPS):
                sl = slice(g * GROUP, (g + 1) * GROUP)
                state_ref[b, g] = _dot_tn(vm[:, sl], kmd[:, sl]) * smask

    @pl.when(p == 0)
    def _rope_tables():
        pos = (N_META + j * C + _iota((C, LANES), 0)).astype(F32)
        ang = pos * inv_ref[...]
        sin = jnp.sin(ang)
        cos_ref[...] = jnp.cos(ang)
        sin_ref[...] = jnp.where(_first_half_lanes(), -sin, sin)

    cos = cos_ref[...]
    sin = sin_ref[...]
    head_of_lane = _iota((1, GROUP), 1) >> 6
    smask = smask_ref[...]
    wc = wconv_ref[...]

    def mix_sequence(slot):
        n = p * SEQS_PER_STEP + slot
        x = x_ref[slot]
        xn = _rmsnorm(x, g1_ref[...]).astype(BF16)

        zb = proj(xn, OFF_B)
        yield
        u = proj(xn, OFF_C) * proj(xn, OFF_H)
        ubuf_ref[slot, 6:8, :] = tail_ref[n]
        ubuf_ref[slot, 8:8 + C, :] = u
        conv = wc[0] * ubuf_ref[slot, 6:6 + C, :] + wc[1] * ubuf_ref[slot, 7:7 + C, :] + wc[2] * u
        conv_out = (zb * conv).astype(BF16)
        new_tail = u[C - 2:C, :]
        tail_ref[n] = new_tail

        yield

        qr = _rope(proj(xn, OFF_Q), cos, sin)
        yield
        kr = _rope(proj(xn, OFF_K), cos, sin)
        yield
        vg_t = _dot_nt(wvgt_ref[...], xn)
        v_t = vg_t[:RET_DIM].astype(BF16)
        yield
        qb = qr.astype(BF16)
        kb = kr.astype(BF16)
        qd = (qr * qdec_ref[...]).astype(BF16)
        kd = (kr * kdec_ref[...]).astype(BF16)
        o_parts = []
        new_states = []
        for g in range(N_GROUPS):
            sl = slice(g * GROUP, (g + 1) * GROUP)
            state_t = state_ref[n, g]
            cross_t = _dot_nt(state_t.astype(BF16), qd[:, sl])
            k_heads = jnp.concatenate(
                [jnp.where(head_of_lane == hh, kb[:, sl], jnp.zeros_like(kb[:, sl]))
                 for hh in range(HEADS_PER_GROUP)], axis=0)
            decay = dmask_ref[g * HEADS_PER_GROUP:(g + 1) * HEADS_PER_GROUP].reshape(HEADS_PER_GROUP * C, C)
            scores_t = (_dot_nt(k_heads, qb[:, sl]) * decay).astype(BF16)
            for hh in range(HEADS_PER_GROUP):
                h = g * HEADS_PER_GROUP + hh
                rows = slice(hh * HEAD_DIM, (hh + 1) * HEAD_DIM)
                o_parts.append(cross_t[rows] + _dot(v_t[h * HEAD_DIM:(h + 1) * HEAD_DIM],
                                                    scores_t[hh * C:(hh + 1) * C]))
            new_state = state_t * sdec_ref[:, sl] + _dot(v_t[sl], kd[:, sl]) * smask
            state_ref[n, g] = new_state
            new_states.append(new_state)
            yield

        normed = []
        for h, o_h in enumerate(o_parts):
            mu = jnp.sum(o_h, axis=0, keepdims=True) * (1.0 / HEAD_DIM)
            dlt = o_h - mu
            var = jnp.sum(dlt * dlt, axis=0, keepdims=True) * (1.0 / HEAD_DIM)
            normed.append(dlt * lax.rsqrt(var + GN_EPS) * gaincol_ref[h * HEAD_DIM:(h + 1) * HEAD_DIM, :])
        ret_out_t = (_silu(vg_t[RET_DIM:]) * jnp.concatenate(normed, axis=0)).astype(BF16)
        yield

        x1_ref[slot] = (x + _dot(conv_out, wout_ref[:CONV_DIM, :])
                        + _dot_tn(ret_out_t, wout_ref[CONV_DIM:, :]))
        finals[slot] = (new_tail, new_states)

    finals = [None] * SEQS_PER_STEP
    _trace_staggered([mix_sequence(slot) for slot in range(SEQS_PER_STEP)], PHASE_LAG)

    wg_ref[...] = wg32_ref[...].astype(BF16)
    wu_ref[...] = wu32_ref[...].astype(BF16)
    wd_ref[...] = wd32_ref[...].astype(BF16)

    @pl.when(j == n_tiles - 1)
    def _final_states():
        for slot, (new_tail, new_states) in enumerate(finals):
            n = p * SEQS_PER_STEP + slot
            convst_ref[n] = new_tail
            for g in range(N_GROUPS):
                state = new_states[g].T
                for hh in range(HEADS_PER_GROUP):
                    blk = slice(hh * HEAD_DIM, (hh + 1) * HEAD_DIM)
                    retst_ref[n, g * HEADS_PER_GROUP + hh] = state[blk, blk]


def _prompt_mixer(x_prompt, meta, g1, w_in, w_vg_t, w_conv, gain, w_out, lg_lane, inv_lane,
                  w_gate, w_up, w_down):
    n_seq, seq, _ = x_prompt.shape
    n_tiles = seq // TILE
    pairs = n_seq // SEQS_PER_STEP
    n_steps = n_tiles * pairs
    full = lambda a: pl.BlockSpec(a.shape, lambda j, p: (0,) * a.ndim, pipeline_mode=pl.Buffered(1))
    gu_rows = w_gate.shape[0] // n_steps
    d_rows = w_down.shape[0] // (n_steps // 2)
    assert gu_rows * n_steps == w_gate.shape[0] and d_rows * (n_steps // 2) == w_down.shape[0]
    assert gu_rows % BF16_ROWS == 0 and d_rows % BF16_ROWS == 0
    gu_slab = pl.BlockSpec((gu_rows, w_gate.shape[1]), lambda j, p: (j * pairs + p, 0))
    d_slab = pl.BlockSpec((d_rows, w_down.shape[1]), lambda j, p: ((j * pairs + p) // 2, 0))
    return pl.pallas_call(
        functools.partial(_prompt_mixer_kernel, n_tiles=n_tiles),
        grid=(n_tiles, pairs),
        in_specs=[pl.BlockSpec((SEQS_PER_STEP, TILE, D_MODEL), lambda j, p: (p, j, 0)),
                  full(meta), full(g1), full(w_in), full(w_vg_t), full(w_conv), full(gain),
                  full(w_out), full(lg_lane), full(inv_lane), gu_slab, gu_slab, d_slab],
        out_specs=[pl.BlockSpec((SEQS_PER_STEP, TILE, D_MODEL), lambda j, p: (p, j, 0)),
                   pl.BlockSpec((n_seq, 2, CONV_DIM), lambda j, p: (0, 0, 0)),
                   pl.BlockSpec((n_seq, RET_HEADS, HEAD_DIM, HEAD_DIM), lambda j, p: (0, 0, 0, 0)),
                   gu_slab, gu_slab, d_slab],
        out_shape=[jax.ShapeDtypeStruct((n_seq, seq, D_MODEL), F32),
                   jax.ShapeDtypeStruct((n_seq, 2, CONV_DIM), F32),
                   jax.ShapeDtypeStruct((n_seq, RET_HEADS, HEAD_DIM, HEAD_DIM), F32),
                   jax.ShapeDtypeStruct(w_gate.shape, BF16), jax.ShapeDtypeStruct(w_up.shape, BF16),
                   jax.ShapeDtypeStruct(w_down.shape, BF16)],
        scratch_shapes=[
            pltpu.VMEM((TILE, LANES), F32),
            pltpu.VMEM((TILE, LANES), F32),
            pltpu.VMEM((TILE, RET_DIM), F32),
            pltpu.VMEM((TILE, RET_DIM), F32),
            pltpu.VMEM((RET_HEADS, TILE, TILE), F32),
            pltpu.VMEM((GROUP, GROUP), F32),
            pltpu.VMEM((1, RET_DIM), F32),
            pltpu.VMEM((RET_DIM, TILE), F32),
            pltpu.VMEM((n_seq, 2, CONV_DIM), F32),
            pltpu.VMEM((n_seq, N_GROUPS, GROUP, GROUP), F32),
            pltpu.VMEM((SEQS_PER_STEP, TILE + 8, CONV_DIM), F32),
        ],
        compiler_params=pltpu.CompilerParams(
            dimension_semantics=("arbitrary", "arbitrary"), vmem_limit_bytes=VMEM_LIMIT),
        name="prompt_mixer",
    )(x_prompt, meta, g1, w_in, w_vg_t, w_conv, gain, w_out, lg_lane, inv_lane, w_gate, w_up, w_down)


def _ffn_kernel(xp_ref, xs_ref, qd_ref, kd_ref, v_ref, ointra_ref, zg_ref, cvo_ref, s_ref, gain_ref,
                sdec_ref, wout_ref, g2_ref, wg_ref, wu_ref, wd_ref, gf_ref,
                yp_ref, ys_ref, snew_ref, o_ref, x2_ref, *, prompt_steps, n_seq, dec_seq, e_piece):
    i = pl.program_id(0)
    L, B = dec_seq, n_seq
    pieces_per_head = HEAD_DIM // e_piece

    def chunk_phases(load, finish, rows):
        x = load(rows)
        xn = _rmsnorm(x, g2_ref[...]).astype(BF16)
        yield
        gate = _dot(xn, wg_ref[...])
        yield
        up = _dot(xn, wu_ref[...])
        yield
        hidden = (_silu(gate) * up).astype(BF16)
        yield
        finish(rows, x + _dot(hidden, wd_ref[...]))

    def ffn_chunks(load, finish, n_rows, before=(), after=()):
        chunks = [slice(r, r + FFN_CHUNK) for r in range(0, n_rows, FFN_CHUNK)]
        _trace_staggered(list(before) + [chunk_phases(load, finish, rows) for rows in chunks] + list(after),
                         FFN_PHASE_LAG)

    def previous_tile_norm():
        yp_ref[...] = _rmsnorm(x2_ref[...], gf_ref[...])
        yield

    def stash_prenorm(rows, x2):
        x2_ref[rows, :] = x2

    def write_sample(rows, x2):
        ys_ref[rows, :] = _rmsnorm(x2, gf_ref[...])

    def state_piece():
        h = i // pieces_per_head
        base = pl.multiple_of(h * HEAD_DIM, HEAD_DIM)
        rows = pl.ds(pl.multiple_of(base + (i % pieces_per_head) * e_piece, e_piece), e_piece)
        qd_h = qd_ref[pl.ds(base, HEAD_DIM), :]
        kd_h = kd_ref[pl.ds(base, HEAD_DIM), :]
        state_decay = sdec_ref[h]
        v_blk = [v_ref[rows, j * B:(j + 1) * B] for j in range(L)]
        cross = [jnp.zeros((e_piece, B), F32) for _ in range(L)]
        for d in range(HEAD_DIM):
            s_de = s_ref[0, d]
            new = s_de * state_decay
            for t in range(L):
                lanes = slice(t * B, (t + 1) * B)
                cross[t] = cross[t] + qd_h[d:d + 1, lanes] * s_de
                new = new + kd_h[d:d + 1, lanes] * v_blk[t]
            snew_ref[0, d] = new
        for t in range(L):
            o_ref[rows, t * B:(t + 1) * B] = o_ref[rows, t * B:(t + 1) * B] + cross[t]
        yield

    @pl.when(i == 0)
    def _seed():
        o_ref[...] = ointra_ref[...]
        x2_ref[...] = jnp.zeros_like(x2_ref)

    @pl.when(i < prompt_steps)
    def _prompt_rows():
        ffn_chunks(lambda rows: xp_ref[rows, :], stash_prenorm, FFN_TILE,
                   before=[previous_tile_norm()], after=[state_piece()])

    @pl.when(i == prompt_steps)
    def _sample_rows():
        gain_col = jnp.broadcast_to(gain_ref[...], (L * B, RET_DIM)).T
        for hd in range(RET_HEADS):
            r0 = hd * HEAD_DIM
            o_h = o_ref[r0:r0 + HEAD_DIM, :]
            mu = jnp.sum(o_h, axis=0, keepdims=True) * (1.0 / HEAD_DIM)
            dlt = o_h - mu
            var = jnp.sum(dlt * dlt, axis=0, keepdims=True) * (1.0 / HEAD_DIM)
            o_ref[r0:r0 + HEAD_DIM, :] = dlt * lax.rsqrt(var + GN_EPS) * gain_col[r0:r0 + HEAD_DIM, :]
        ret_out = (_silu(zg_ref[...]) * o_ref[...].T).astype(BF16)
        x1 = (xs_ref[...] + _dot(cvo_ref[...], wout_ref[:CONV_DIM, :])
              + _dot(ret_out, wout_ref[CONV_DIM:, :]))
        ffn_chunks(lambda rows: x1[rows], write_sample, L * B, before=[previous_tile_norm()])


def _ffn(x_p, xs, qd_t, kd_t, v_t, o_intra, zg, cvo, s4, gain, w_out_b, g2, w_gate, w_up, w_down, gf,
         dec_seq):
    prompt_steps = x_p.shape[0] // FFN_TILE
    n_seq = xs.shape[0] // dec_seq
    assert prompt_steps % RET_HEADS == 0
    e_piece = HEAD_DIM // (prompt_steps // RET_HEADS)
    assert e_piece % 8 == 0 and HEAD_DIM % e_piece == 0
    pieces_per_head = HEAD_DIM // e_piece
    sdec = jnp.asarray(np.exp(np.float32(dec_seq) * _LOG_GAMMA))
    full = lambda a: pl.BlockSpec(a.shape, lambda i: (0,) * a.ndim, pipeline_mode=pl.Buffered(1))
    prompt_tile = pl.BlockSpec((FFN_TILE, D_MODEL), lambda i: (jnp.minimum(i, prompt_steps - 1), 0))
    prompt_out = pl.BlockSpec((FFN_TILE, D_MODEL), lambda i: (jnp.maximum(i - 1, 0), 0))
    sample_out = pl.BlockSpec(xs.shape, lambda i: (0, 0))

    def piece_index(i):
        piece = jnp.minimum(i, prompt_steps - 1)
        return (piece // pieces_per_head, 0, piece % pieces_per_head, 0)

    state_piece = pl.BlockSpec((1, HEAD_DIM, e_piece, n_seq), piece_index)
    return pl.pallas_call(
        functools.partial(_ffn_kernel, prompt_steps=prompt_steps, n_seq=n_seq, dec_seq=dec_seq,
                          e_piece=e_piece),
        grid=(prompt_steps + 1,),
        in_specs=[prompt_tile, full(xs), full(qd_t), full(kd_t), full(v_t), full(o_intra), full(zg),
                  full(cvo), state_piece, full(gain), pl.BlockSpec(memory_space=pltpu.SMEM), full(w_out_b),
                  full(g2),
                  full(w_gate), full(w_up), full(w_down), full(gf)],
        out_specs=[prompt_out, sample_out, state_piece],
        out_shape=[jax.ShapeDtypeStruct(x_p.shape, F32), jax.ShapeDtypeStruct(xs.shape, F32),
                   jax.ShapeDtypeStruct(s4.shape, F32)],
        scratch_shapes=[pltpu.VMEM(o_intra.shape, F32),
                        pltpu.VMEM((FFN_TILE, D_MODEL), F32)],
        compiler_params=pltpu.CompilerParams(
            dimension_semantics=("arbitrary",), vmem_limit_bytes=VMEM_LIMIT),
        name="ffn",
    )(x_p, xs, qd_t, kd_t, v_t, o_intra, zg, cvo, s4, gain, sdec, w_out_b, g2, w_gate, w_up, w_down, gf)


def _gamma_pow(head, power):
    return float(np.exp(np.float32(power) * _LOG_GAMMA[head]))


def _sample_front_kernel(x_ref, st_ref, g1_ref, win32_ref, wvg32_ref, wconv_ref, wout32_ref, inv_ref,
                         convst_ref, qd_ref, kd_ref, v_ref, o_ref, zg_ref, cvo_ref,
                         win_ref, wout_ref, wvgt_ref, *, n_seq, dec_seq):
    s = pl.program_id(0)
    L, B = dec_seq, n_seq
    k_scale = HEAD_DIM ** -0.5

    slab = pl.ds(pl.multiple_of(s * CAST_ROWS, CAST_ROWS), CAST_ROWS)
    win_ref[...] = win32_ref[slab, :].astype(BF16)
    wout_ref[...] = wout32_ref[slab, :].astype(BF16)
    wvgt_ref[...] = wvg32_ref[...].T.astype(BF16)

    @pl.when(s == 0)
    def _dense_front():
        xn = _rmsnorm(x_ref[...], g1_ref[...]).astype(BF16)

        def proj(off):
            return _dot(xn, win32_ref[:, off:off + 512].astype(BF16))

        zb = proj(OFF_B)
        u = proj(OFF_C) * proj(OFF_H)
        ext = [st_ref[:, 0, :], st_ref[:, 1, :]] + [u[i * B:(i + 1) * B] for i in range(L)]
        wc = wconv_ref[...]
        for i in range(L):
            conv = wc[0] * ext[i] + wc[1] * ext[i + 1] + wc[2] * ext[i + 2]
            cvo_ref[i * B:(i + 1) * B, :] = (zb[i * B:(i + 1) * B] * conv).astype(BF16)
        convst_ref[:, 0, :] = ext[L]
        convst_ref[:, 1, :] = ext[L + 1]
        zg_ref[...] = proj(OFF_G)

        q_t = proj(OFF_Q).T
        k_t = proj(OFF_K).T
        v_t = proj(OFF_V).T
        v_ref[...] = v_t
        inv = jnp.broadcast_to(inv_ref[...], (B, LANES)).T[:HALF_DIM]
        cos_sin = [(jnp.cos(float(PAST_LEN + i) * inv), jnp.sin(float(PAST_LEN + i) * inv))
                   for i in range(L)]

        def rope(t, r0, i):
            cos, sin = cos_sin[i]
            t1 = t[r0:r0 + HALF_DIM, i * B:(i + 1) * B]
            t2 = t[r0 + HALF_DIM:r0 + HEAD_DIM, i * B:(i + 1) * B]
            return jnp.concatenate([t1 * cos - t2 * sin, t2 * cos + t1 * sin], axis=0)

        for hd in range(RET_HEADS):
            r0 = hd * HEAD_DIM
            qr = [rope(q_t, r0, i) for i in range(L)]
            kr = [rope(k_t, r0, i) for i in range(L)]
            for i in range(L):
                lanes = slice(i * B, (i + 1) * B)
                qd_ref[r0:r0 + HEAD_DIM, lanes] = qr[i] * _gamma_pow(hd, i + 1)
                kd_ref[r0:r0 + HEAD_DIM, lanes] = kr[i] * (_gamma_pow(hd, L - 1 - i) * k_scale)
                intra = jnp.zeros((HEAD_DIM, B), F32)
                for j in range(i + 1):
                    score = jnp.sum(qr[i] * kr[j], axis=0, keepdims=True) * (_gamma_pow(hd, i - j) * k_scale)
                    intra = intra + score * v_t[r0:r0 + HEAD_DIM, j * B:(j + 1) * B]
                o_ref[r0:r0 + HEAD_DIM, lanes] = intra


def _sample_front(xs, st, g1, w_in, w_conv, w_out, inv_lane, dec_seq):
    rows = xs.shape[0]
    n_seq = rows // dec_seq
    n_steps = w_in.shape[0] // CAST_ROWS
    assert w_out.shape[0] == n_steps * CAST_ROWS and 2 * RET_DIM == n_steps * CAST_ROWS
    full = lambda a: pl.BlockSpec(a.shape, lambda s: (0,) * a.ndim, pipeline_mode=pl.Buffered(1))
    const = lambda shape: pl.BlockSpec(shape, lambda s: (0, 0))
    feat = (RET_DIM, rows)
    return pl.pallas_call(
        functools.partial(_sample_front_kernel, n_seq=n_seq, dec_seq=dec_seq),
        grid=(n_steps,),
        in_specs=[full(xs), full(st), full(g1), full(w_in),
                  pl.BlockSpec((w_in.shape[0], CAST_ROWS), lambda s: (0, OFF_V // CAST_ROWS + s)),
                  full(w_conv), full(w_out), full(inv_lane)],
        out_specs=[pl.BlockSpec(st.shape, lambda s: (0, 0, 0)),
                   const(feat), const(feat), const(feat), const(feat),
                   const((rows, RET_DIM)), const((rows, CONV_DIM)),
                   pl.BlockSpec((CAST_ROWS, w_in.shape[1]), lambda s: (s, 0)),
                   pl.BlockSpec((CAST_ROWS, w_out.shape[1]), lambda s: (s, 0)),
                   pl.BlockSpec((CAST_ROWS, w_in.shape[0]), lambda s: (s, 0))],
        out_shape=[jax.ShapeDtypeStruct(st.shape, F32),
                   jax.ShapeDtypeStruct(feat, F32), jax.ShapeDtypeStruct(feat, F32),
                   jax.ShapeDtypeStruct(feat, F32), jax.ShapeDtypeStruct(feat, F32),
                   jax.ShapeDtypeStruct((rows, RET_DIM), F32), jax.ShapeDtypeStruct((rows, CONV_DIM), BF16),
                   jax.ShapeDtypeStruct(w_in.shape, BF16), jax.ShapeDtypeStruct(w_out.shape, BF16),
                   jax.ShapeDtypeStruct((2 * RET_DIM, w_in.shape[0]), BF16)],
        compiler_params=pltpu.CompilerParams(
            dimension_semantics=("arbitrary",), vmem_limit_bytes=VMEM_LIMIT),
        name="sample_front",
    )(xs, st, g1, w_in, w_in, w_conv, w_out, inv_lane)


def kernel(x_prompt, x_sample, state_conv, state_ret, meta_tokens, norm1_g, w_in, w_conv, ret_norm_g,
           w_out, norm2_g, w_gate, w_up, w_down, final_norm_g):
    n_p, seq, _ = x_prompt.shape
    n_s, dec_seq, _ = x_sample.shape
    rows_s = n_s * dec_seq
    assert norm1_g.shape[0] == 1 and seq % TILE == 0 and n_p % SEQS_PER_STEP == 0
    assert (n_p * seq) % FFN_TILE == 0
    assert n_s % LANES == 0 and rows_s % FFN_CHUNK == 0

    g1, g2, gf = norm1_g[0][None], norm2_g[0][None], final_norm_g[None]
    lg_lane = jnp.asarray(_LOG_GAMMA[np.arange(RET_DIM) // HEAD_DIM][None])
    inv = ROPE_BASE ** (-jnp.arange(HALF_DIM, dtype=F32) / HALF_DIM)
    inv_lane = jnp.tile(inv, LANES // HALF_DIM)[None]

    gain = ret_norm_g[0][None]

    w_conv_rows = jnp.transpose(w_conv, (1, 0, 2))
    xs = jnp.transpose(x_sample, (1, 0, 2)).reshape(rows_s, D_MODEL)
    conv_s, qd_t, kd_t, v_t, o_intra, zg, cvo, w_in_b, w_out_b, w_vg_t = _sample_front(
        xs, state_conv[0], g1, w_in[0], w_conv_rows, w_out[0], inv_lane, dec_seq)

    x1_p, conv_p, ret_p, wg_b, wu_b, wd_b = _prompt_mixer(
        x_prompt, meta_tokens, g1, w_in_b, w_vg_t, w_conv_rows, gain, w_out_b, lg_lane, inv_lane,
        w_gate[0], w_up[0], w_down[0])

    s4 = jnp.transpose(state_ret[0], (1, 2, 3, 0))
    y_p, y_s, s4_new = _ffn(x1_p.reshape(n_p * seq, D_MODEL), xs, qd_t, kd_t, v_t, o_intra, zg, cvo, s4,
                            gain, w_out_b, g2, wg_b, wu_b, wd_b, gf, dec_seq)
    y_prompt = y_p.reshape(n_p, seq, D_MODEL)
    y_sample = jnp.transpose(y_s.reshape(dec_seq, n_s, D_MODEL), (1, 0, 2))
    ret_s = jnp.transpose(s4_new, (3, 0, 1, 2))

    return (y_prompt, y_sample, conv_p[None], ret_p[None], conv_s[None], ret_s[None])
```

```python
import functools

import numpy as np
import jax
import jax.numpy as jnp
from jax import lax
from jax.experimental import pallas as pl
from jax.experimental.pallas import tpu as pltpu

D_MODEL = 1024
N_META = 16
CONV_DIM = 512
RET_HEADS = 8
HEAD_DIM = 64
HALF_DIM = HEAD_DIM // 2
RET_DIM = RET_HEADS * HEAD_DIM
D_FF = 2816
PAST_LEN = 16384
ROPE_BASE = 10000.0
EPS = 1e-6
GN_EPS = 1e-5

OFF_B, OFF_C, OFF_H, OFF_Q, OFF_K, OFF_V, OFF_G = (i * 512 for i in range(7))

LANES = 128
GROUP = 256
HEADS_PER_GROUP = GROUP // HEAD_DIM
N_GROUPS = RET_DIM // GROUP
TILE = 256
SEQS_PER_STEP = 4
FFN_TILE = 512
FFN_CHUNK = 256
CAST_ROWS = 512
PHASE_LAG = 3
FFN_PHASE_LAG = 1
BF16_ROWS = 16
VMEM_LIMIT = 56 * 1024 * 1024

F32 = jnp.float32
BF16 = jnp.bfloat16

_LOG_GAMMA = np.log1p(-(2.0 ** (-5.0 - np.arange(RET_HEADS)))).astype(np.float32)


def _dot(a, b):
    return jnp.dot(a, b, preferred_element_type=F32)


def _dot_nt(a, b):
    return lax.dot_general(a, b, (((1,), (1,)), ((), ())), preferred_element_type=F32)


def _dot_tn(a, b):
    return lax.dot_general(a, b, (((0,), (0,)), ((), ())), preferred_element_type=F32)


def _rmsnorm(x, g):
    ms = jnp.mean(x * x, axis=-1, keepdims=True)
    return x * lax.rsqrt(ms + EPS) * g


def _silu(x):
    return x * jax.nn.sigmoid(x)


def _iota(shape, dim):
    return lax.broadcasted_iota(jnp.int32, shape, dim)


def _first_half_lanes():
    return (_iota((1, LANES), 1) & (HEAD_DIM - 1)) < HALF_DIM


def _rope(t, cos, signed_sin):
    first_half = _first_half_lanes()
    out = []
    for b in range(t.shape[1] // LANES):
        blk = t[:, b * LANES:(b + 1) * LANES]
        partner = jnp.where(first_half, pltpu.roll(blk, LANES - HALF_DIM, axis=1),
                            pltpu.roll(blk, HALF_DIM, axis=1))
        out.append(blk * cos + partner * signed_sin)
    return jnp.concatenate(out, axis=1)


def _ones_block():
    r = _iota((GROUP, GROUP), 0) >> 6
    c = _iota((GROUP, GROUP), 1) >> 6
    return jnp.where(r == c, 1.0, 0.0).astype(BF16)


def _trace_staggered(phase_iters, lag):
    live = list(range(len(phase_iters)))
    tick = 0
    while live:
        for k in list(live):
            if tick >= lag * k:
                try:
                    next(phase_iters[k])
                except StopIteration:
                    live.remove(k)
        tick += 1


def _prompt_mixer_kernel(x_ref, meta_ref, g1_ref, win_ref, wvgt_ref, wconv_ref, gain_ref, wout_ref,
                         lg_ref, inv_ref, wg32_ref, wu32_ref, wd32_ref,
                         x1_ref, convst_ref, retst_ref, wg_ref, wu_ref, wd_ref,
                         cos_ref, sin_ref, qdec_ref, kdec_ref, dmask_ref,
                         smask_ref, sdec_ref, gaincol_ref, tail_ref, state_ref, ubuf_ref, *, n_tiles):
    j = pl.program_id(0)
    p = pl.program_id(1)
    C = TILE
    k_scale = HEAD_DIM ** -0.5

    def proj(xn, off):
        return _dot(xn, win_ref[:, off:off + 512])

    @pl.when((j == 0) & (p == 0))
    def _init():
        i_f = _iota((C, 1), 0).astype(F32)
        lg = lg_ref[...]
        qdec_ref[...] = jnp.exp((i_f + 1.0) * lg)
        kdec_ref[...] = jnp.exp((C - 1.0 - i_f) * lg) * k_scale
        sdec_ref[...] = jnp.exp(C * lg)
        gaincol_ref[...] = jnp.broadcast_to(gain_ref[...], (C, RET_DIM)).T
        diff = (_iota((C, C), 1) - _iota((C, C), 0)).astype(F32)
        for h in range(RET_HEADS):
            dmask_ref[h] = jnp.where(
                diff >= 0, jnp.exp(jnp.maximum(diff, 0.0) * float(_LOG_GAMMA[h])) * k_scale, 0.0)
        smask = _ones_block().astype(F32)
        smask_ref[...] = smask

        xm = _rmsnorm(meta_ref[...], g1_ref[...]).astype(BF16)
        um = proj(xm, OFF_C) * proj(xm, OFF_H)
        vm = proj(xm, OFF_V).astype(BF16)
        m_f = _iota((N_META, 1), 0).astype(F32)
        ang = m_f * inv_ref[...]
        sin = jnp.sin(ang)
        kmr = _rope(proj(xm, OFF_K), jnp.cos(ang), jnp.where(_first_half_lanes(), -sin, sin))
        kmd = (kmr * (jnp.exp((N_META - 1.0 - m_f) * lg) * k_scale)).astype(BF16)
        for b in range(tail_ref.shape[0]):
            tail_ref[b] = um[N_META - 2:N_META, :]
            for g in range(N_GROUPS):
                sl = slice(g * GROUP, (g + 1) * GROUP)
                state_ref[b, g] = _dot_tn(vm[:, sl], kmd[:, sl]) * smask

    @pl.when(p == 0)
    def _rope_tables():
        pos = (N_META + j * C + _iota((C, LANES), 0)).astype(F32)
        ang = pos * inv_ref[...]
        sin = jnp.sin(ang)
        cos_ref[...] = jnp.cos(ang)
        sin_ref[...] = jnp.where(_first_half_lanes(), -sin, sin)

    cos = cos_ref[...]
    sin = sin_ref[...]
    head_of_lane = _iota((1, GROUP), 1) >> 6
    smask = smask_ref[...]
    wc = wconv_ref[...]

    def mix_sequence(slot):
        n = p * SEQS_PER_STEP + slot
        x = x_ref[slot]
        xn = _rmsnorm(x, g1_ref[...]).astype(BF16)

        zb = proj(xn, OFF_B)
        yield
        u = proj(xn, OFF_C) * proj(xn, OFF_H)
        ubuf_ref[slot, 6:8, :] = tail_ref[n]
        ubuf_ref[slot, 8:8 + C, :] = u
        conv = wc[0] * ubuf_ref[slot, 6:6 + C, :] + wc[1] * ubuf_ref[slot, 7:7 + C, :] + wc[2] * u
        conv_out = (zb * conv).astype(BF16)
        new_tail = u[C - 2:C, :]
        tail_ref[n] = new_tail

        yield

        qr = _rope(proj(xn, OFF_Q), cos, sin)
        yield
        kr = _rope(proj(xn, OFF_K), cos, sin)
        yield
        vg_t = _dot_nt(wvgt_ref[...], xn)
        v_t = vg_t[:RET_DIM].astype(BF16)
        yield
        qb = qr.astype(BF16)
        kb = kr.astype(BF16)
        qd = (qr * qdec_ref[...]).astype(BF16)
        kd = (kr * kdec_ref[...]).astype(BF16)
        o_parts = []
        new_states = []
        for g in range(N_GROUPS):
            sl = slice(g * GROUP, (g + 1) * GROUP)
            state_t = state_ref[n, g]
            cross_t = _dot_nt(state_t.astype(BF16), qd[:, sl])
            k_heads = jnp.concatenate(
                [jnp.where(head_of_lane == hh, kb[:, sl], jnp.zeros_like(kb[:, sl]))
                 for hh in range(HEADS_PER_GROUP)], axis=0)
            decay = dmask_ref[g * HEADS_PER_GROUP:(g + 1) * HEADS_PER_GROUP].reshape(HEADS_PER_GROUP * C, C)
            scores_t = (_dot_nt(k_heads, qb[:, sl]) * decay).astype(BF16)
            for hh in range(HEADS_PER_GROUP):
                h = g * HEADS_PER_GROUP + hh
                rows = slice(hh * HEAD_DIM, (hh + 1) * HEAD_DIM)
                o_parts.append(cross_t[rows] + _dot(v_t[h * HEAD_DIM:(h + 1) * HEAD_DIM],
                                                    scores_t[hh * C:(hh + 1) * C]))
            new_state = state_t * sdec_ref[:, sl] + _dot(v_t[sl], kd[:, sl]) * smask
            state_ref[n, g] = new_state
            new_states.append(new_state)
            yield

        normed = []
        for h, o_h in enumerate(o_parts):
            mu = jnp.sum(o_h, axis=0, keepdims=True) * (1.0 / HEAD_DIM)
            dlt = o_h - mu
            var = jnp.sum(dlt * dlt, axis=0, keepdims=True) * (1.0 / HEAD_DIM)
            normed.append(dlt * lax.rsqrt(var + GN_EPS) * gaincol_ref[h * HEAD_DIM:(h + 1) * HEAD_DIM, :])
        ret_out_t = (_silu(vg_t[RET_DIM:]) * jnp.concatenate(normed, axis=0)).astype(BF16)
        yield

        x1_ref[slot] = (x + _dot(conv_out, wout_ref[:CONV_DIM, :])
                        + _dot_tn(ret_out_t, wout_ref[CONV_DIM:, :]))
        finals[slot] = (new_tail, new_states)

    finals = [None] * SEQS_PER_STEP
    _trace_staggered([mix_sequence(slot) for slot in range(SEQS_PER_STEP)], PHASE_LAG)

    wg_ref[...] = wg32_ref[...].astype(BF16)
    wu_ref[...] = wu32_ref[...].astype(BF16)
    wd_ref[...] = wd32_ref[...].astype(BF16)

    @pl.when(j == n_tiles - 1)
    def _final_states():
        for slot, (new_tail, new_states) in enumerate(finals):
            n = p * SEQS_PER_STEP + slot
            convst_ref[n] = new_tail
            for g in range(N_GROUPS):
                state = new_states[g].T
                for hh in range(HEADS_PER_GROUP):
                    blk = slice(hh * HEAD_DIM, (hh + 1) * HEAD_DIM)
                    retst_ref[n, g * HEADS_PER_GROUP + hh] = state[blk, blk]


def _prompt_mixer(x_prompt, meta, g1, w_in, w_vg_t, w_conv, gain, w_out, lg_lane, inv_lane,
                  w_gate, w_up, w_down):
    n_seq, seq, _ = x_prompt.shape
    n_tiles = seq // TILE
    pairs = n_seq // SEQS_PER_STEP
    n_steps = n_tiles * pairs
    full = lambda a: pl.BlockSpec(a.shape, lambda j, p: (0,) * a.ndim, pipeline_mode=pl.Buffered(1))
    gu_rows = w_gate.shape[0] // n_steps
    d_rows = w_down.shape[0] // (n_steps // 2)
    assert gu_rows * n_steps == w_gate.shape[0] and d_rows * (n_steps // 2) == w_down.shape[0]
    assert gu_rows % BF16_ROWS == 0 and d_rows % BF16_ROWS == 0
    gu_slab = pl.BlockSpec((gu_rows, w_gate.shape[1]), lambda j, p: (j * pairs + p, 0))
    d_slab = pl.BlockSpec((d_rows, w_down.shape[1]), lambda j, p: ((j * pairs + p) // 2, 0))
    return pl.pallas_call(
        functools.partial(_prompt_mixer_kernel, n_tiles=n_tiles),
        grid=(n_tiles, pairs),
        in_specs=[pl.BlockSpec((SEQS_PER_STEP, TILE, D_MODEL), lambda j, p: (p, j, 0)),
                  full(meta), full(g1), full(w_in), full(w_vg_t), full(w_conv), full(gain),
                  full(w_out), full(lg_lane), full(inv_lane), gu_slab, gu_slab, d_slab],
        out_specs=[pl.BlockSpec((SEQS_PER_STEP, TILE, D_MODEL), lambda j, p: (p, j, 0)),
                   pl.BlockSpec((n_seq, 2, CONV_DIM), lambda j, p: (0, 0, 0)),
                   pl.BlockSpec((n_seq, RET_HEADS, HEAD_DIM, HEAD_DIM), lambda j, p: (0, 0, 0, 0)),
                   gu_slab, gu_slab, d_slab],
        out_shape=[jax.ShapeDtypeStruct((n_seq, seq, D_MODEL), F32),
                   jax.ShapeDtypeStruct((n_seq, 2, CONV_DIM), F32),
                   jax.ShapeDtypeStruct((n_seq, RET_HEADS, HEAD_DIM, HEAD_DIM), F32),
                   jax.ShapeDtypeStruct(w_gate.shape, BF16), jax.ShapeDtypeStruct(w_up.shape, BF16),
                   jax.ShapeDtypeStruct(w_down.shape, BF16)],
        scratch_shapes=[
            pltpu.VMEM((TILE, LANES), F32),
            pltpu.VMEM((TILE, LANES), F32),
            pltpu.VMEM((TILE, RET_DIM), F32),
            pltpu.VMEM((TILE, RET_DIM), F32),
            pltpu.VMEM((RET_HEADS, TILE, TILE), F32),
            pltpu.VMEM((GROUP, GROUP), F32),
            pltpu.VMEM((1, RET_DIM), F32),
            pltpu.VMEM((RET_DIM, TILE), F32),
            pltpu.VMEM((n_seq, 2, CONV_DIM), F32),
            pltpu.VMEM((n_seq, N_GROUPS, GROUP, GROUP), F32),
            pltpu.VMEM((SEQS_PER_STEP, TILE + 8, CONV_DIM), F32),
        ],
        compiler_params=pltpu.CompilerParams(
            dimension_semantics=("arbitrary", "arbitrary"), vmem_limit_bytes=VMEM_LIMIT),
        name="prompt_mixer",
    )(x_prompt, meta, g1, w_in, w_vg_t, w_conv, gain, w_out, lg_lane, inv_lane, w_gate, w_up, w_down)


def _ffn_kernel(xp_ref, xs_ref, qd_ref, kd_ref, v_ref, ointra_ref, zg_ref, cvo_ref, s_ref, gain_ref,
                sdec_ref, wout_ref, g2_ref, wg_ref, wu_ref, wd_ref, gf_ref,
                yp_ref, ys_ref, snew_ref, o_ref, x2_ref, *, prompt_steps, n_seq, dec_seq, e_piece):
    i = pl.program_id(0)
    L, B = dec_seq, n_seq
    pieces_per_head = HEAD_DIM // e_piece

    def chunk_phases(load, finish, rows):
        x = load(rows)
        xn = _rmsnorm(x, g2_ref[...]).astype(BF16)
        yield
        gate = _dot(xn, wg_ref[...])
        yield
        up = _dot(xn, wu_ref[...])
        yield
        hidden = (_silu(gate) * up).astype(BF16)
        yield
        finish(rows, x + _dot(hidden, wd_ref[...]))

    def ffn_chunks(load, finish, n_rows, before=(), after=()):
        chunks = [slice(r, r + FFN_CHUNK) for r in range(0, n_rows, FFN_CHUNK)]
        _trace_staggered(list(before) + [chunk_phases(load, finish, rows) for rows in chunks] + list(after),
                         FFN_PHASE_LAG)

    def previous_tile_norm():
        yp_ref[...] = _rmsnorm(x2_ref[...], gf_ref[...])
        yield

    def stash_prenorm(rows, x2):
        x2_ref[rows, :] = x2

    def write_sample(rows, x2):
        y = _rmsnorm(x2, gf_ref[...])
        for k in range(FFN_CHUNK // B):
            ys_ref[:, rows.start // B + k, :] = y[k * B:(k + 1) * B]

    def state_piece():
        h = i // pieces_per_head
        base = pl.multiple_of(h * HEAD_DIM, HEAD_DIM)
        rows = pl.ds(pl.multiple_of(base + (i % pieces_per_head) * e_piece, e_piece), e_piece)
        qd_h = qd_ref[pl.ds(base, HEAD_DIM), :]
        kd_h = kd_ref[pl.ds(base, HEAD_DIM), :]
        state_decay = sdec_ref[h]
        v_blk = [v_ref[rows, j * B:(j + 1) * B] for j in range(L)]
        cross = [jnp.zeros((e_piece, B), F32) for _ in range(L)]
        for d in range(HEAD_DIM):
            s_de = s_ref[0, d]
            new = s_de * state_decay
            for t in range(L):
                lanes = slice(t * B, (t + 1) * B)
                cross[t] = cross[t] + qd_h[d:d + 1, lanes] * s_de
                new = new + kd_h[d:d + 1, lanes] * v_blk[t]
            snew_ref[0, d] = new
        for t in range(L):
            o_ref[rows, t * B:(t + 1) * B] = o_ref[rows, t * B:(t + 1) * B] + cross[t]
        yield

    @pl.when(i == 0)
    def _seed():
        o_ref[...] = ointra_ref[...]
        x2_ref[...] = jnp.zeros_like(x2_ref)

    @pl.when(i < prompt_steps)
    def _prompt_rows():
        ffn_chunks(lambda rows: xp_ref[rows, :], stash_prenorm, FFN_TILE,
                   before=[previous_tile_norm()], after=[state_piece()])

    @pl.when(i == prompt_steps)
    def _sample_rows():
        gain_col = jnp.broadcast_to(gain_ref[...], (L * B, RET_DIM)).T
        for hd in range(RET_HEADS):
            r0 = hd * HEAD_DIM
            o_h = o_ref[r0:r0 + HEAD_DIM, :]
            mu = jnp.sum(o_h, axis=0, keepdims=True) * (1.0 / HEAD_DIM)
            dlt = o_h - mu
            var = jnp.sum(dlt * dlt, axis=0, keepdims=True) * (1.0 / HEAD_DIM)
            o_ref[r0:r0 + HEAD_DIM, :] = dlt * lax.rsqrt(var + GN_EPS) * gain_col[r0:r0 + HEAD_DIM, :]
        ret_out = (_silu(zg_ref[...]) * o_ref[...].T).astype(BF16)
        xs = jnp.concatenate([xs_ref[:, t, :] for t in range(L)], axis=0)
        x1 = (xs + _dot(cvo_ref[...], wout_ref[:CONV_DIM, :])
              + _dot(ret_out, wout_ref[CONV_DIM:, :]))
        ffn_chunks(lambda rows: x1[rows], write_sample, L * B, before=[previous_tile_norm()])


def _ffn(x_p, xs, qd_t, kd_t, v_t, o_intra, zg, cvo, s4, gain, w_out_b, g2, w_gate, w_up, w_down, gf,
         dec_seq):
    prompt_steps = x_p.shape[0] // FFN_TILE
    n_seq = xs.shape[0]
    assert prompt_steps % RET_HEADS == 0 and FFN_CHUNK % n_seq == 0
    e_piece = HEAD_DIM // (prompt_steps // RET_HEADS)
    assert e_piece % 8 == 0 and HEAD_DIM % e_piece == 0
    pieces_per_head = HEAD_DIM // e_piece
    sdec = jnp.asarray(np.exp(np.float32(dec_seq) * _LOG_GAMMA))
    full = lambda a: pl.BlockSpec(a.shape, lambda i: (0,) * a.ndim, pipeline_mode=pl.Buffered(1))
    prompt_tile = pl.BlockSpec((FFN_TILE, D_MODEL), lambda i: (jnp.minimum(i, prompt_steps - 1), 0))
    prompt_out = pl.BlockSpec((FFN_TILE, D_MODEL), lambda i: (jnp.maximum(i - 1, 0), 0))
    sample_out = pl.BlockSpec(xs.shape, lambda i: (0, 0, 0))

    def piece_index(i):
        piece = jnp.minimum(i, prompt_steps - 1)
        return (piece // pieces_per_head, 0, piece % pieces_per_head, 0)

    state_piece = pl.BlockSpec((1, HEAD_DIM, e_piece, n_seq), piece_index)
    return pl.pallas_call(
        functools.partial(_ffn_kernel, prompt_steps=prompt_steps, n_seq=n_seq, dec_seq=dec_seq,
                          e_piece=e_piece),
        grid=(prompt_steps + 1,),
        in_specs=[prompt_tile, full(xs), full(qd_t), full(kd_t), full(v_t), full(o_intra), full(zg),
                  full(cvo), state_piece, full(gain), pl.BlockSpec(memory_space=pltpu.SMEM), full(w_out_b),
                  full(g2),
                  full(w_gate), full(w_up), full(w_down), full(gf)],
        out_specs=[prompt_out, sample_out, state_piece],
        out_shape=[jax.ShapeDtypeStruct(x_p.shape, F32), jax.ShapeDtypeStruct(xs.shape, F32),
                   jax.ShapeDtypeStruct(s4.shape, F32)],
        scratch_shapes=[pltpu.VMEM(o_intra.shape, F32),
                        pltpu.VMEM((FFN_TILE, D_MODEL), F32)],
        compiler_params=pltpu.CompilerParams(
            dimension_semantics=("arbitrary",), vmem_limit_bytes=VMEM_LIMIT),
        name="ffn",
    )(x_p, xs, qd_t, kd_t, v_t, o_intra, zg, cvo, s4, gain, sdec, w_out_b, g2, w_gate, w_up, w_down, gf)


def _gamma_pow(head, power):
    return float(np.exp(np.float32(power) * _LOG_GAMMA[head]))


def _sample_front_kernel(x_ref, st_ref, g1_ref, win32_ref, wvg32_ref, wconv_ref, wout32_ref, inv_ref,
                         convst_ref, qd_ref, kd_ref, v_ref, o_ref, zg_ref, cvo_ref,
                         win_ref, wout_ref, wvgt_ref, *, n_seq, dec_seq):
    s = pl.program_id(0)
    L, B = dec_seq, n_seq
    k_scale = HEAD_DIM ** -0.5

    slab = pl.ds(pl.multiple_of(s * CAST_ROWS, CAST_ROWS), CAST_ROWS)
    win_ref[...] = win32_ref[slab, :].astype(BF16)
    wout_ref[...] = wout32_ref[slab, :].astype(BF16)
    wvgt_ref[...] = wvg32_ref[...].T.astype(BF16)

    @pl.when(s == 0)
    def _dense_front():
        x = jnp.concatenate([x_ref[:, t, :] for t in range(L)], axis=0)
        xn = _rmsnorm(x, g1_ref[...]).astype(BF16)

        def proj(off):
            return _dot(xn, win32_ref[:, off:off + 512].astype(BF16))

        zb = proj(OFF_B)
        u = proj(OFF_C) * proj(OFF_H)
        ext = [st_ref[:, 0, :], st_ref[:, 1, :]] + [u[i * B:(i + 1) * B] for i in range(L)]
        wc = wconv_ref[...]
        for i in range(L):
            conv = wc[0] * ext[i] + wc[1] * ext[i + 1] + wc[2] * ext[i + 2]
            cvo_ref[i * B:(i + 1) * B, :] = (zb[i * B:(i + 1) * B] * conv).astype(BF16)
        convst_ref[:, 0, :] = ext[L]
        convst_ref[:, 1, :] = ext[L + 1]
        zg_ref[...] = proj(OFF_G)

        q_t = proj(OFF_Q).T
        k_t = proj(OFF_K).T
        v_t = proj(OFF_V).T
        v_ref[...] = v_t
        inv = jnp.broadcast_to(inv_ref[...], (B, LANES)).T[:HALF_DIM]
        cos_sin = [(jnp.cos(float(PAST_LEN + i) * inv), jnp.sin(float(PAST_LEN + i) * inv))
                   for i in range(L)]

        def rope(t, r0, i):
            cos, sin = cos_sin[i]
            t1 = t[r0:r0 + HALF_DIM, i * B:(i + 1) * B]
            t2 = t[r0 + HALF_DIM:r0 + HEAD_DIM, i * B:(i + 1) * B]
            return jnp.concatenate([t1 * cos - t2 * sin, t2 * cos + t1 * sin], axis=0)

        for hd in range(RET_HEADS):
            r0 = hd * HEAD_DIM
            qr = [rope(q_t, r0, i) for i in range(L)]
            kr = [rope(k_t, r0, i) for i in range(L)]
            for i in range(L):
                lanes = slice(i * B, (i + 1) * B)
                qd_ref[r0:r0 + HEAD_DIM, lanes] = qr[i] * _gamma_pow(hd, i + 1)
                kd_ref[r0:r0 + HEAD_DIM, lanes] = kr[i] * (_gamma_pow(hd, L - 1 - i) * k_scale)
                intra = jnp.zeros((HEAD_DIM, B), F32)
                for j in range(i + 1):
                    score = jnp.sum(qr[i] * kr[j], axis=0, keepdims=True) * (_gamma_pow(hd, i - j) * k_scale)
                    intra = intra + score * v_t[r0:r0 + HEAD_DIM, j * B:(j + 1) * B]
                o_ref[r0:r0 + HEAD_DIM, lanes] = intra


def _sample_front(xs, st, g1, w_in, w_conv, w_out, inv_lane, dec_seq):
    n_seq = xs.shape[0]
    rows = n_seq * dec_seq
    n_steps = w_in.shape[0] // CAST_ROWS
    assert w_out.shape[0] == n_steps * CAST_ROWS and 2 * RET_DIM == n_steps * CAST_ROWS
    full = lambda a: pl.BlockSpec(a.shape, lambda s: (0,) * a.ndim, pipeline_mode=pl.Buffered(1))
    const = lambda shape: pl.BlockSpec(shape, lambda s: (0, 0))
    feat = (RET_DIM, rows)
    return pl.pallas_call(
        functools.partial(_sample_front_kernel, n_seq=n_seq, dec_seq=dec_seq),
        grid=(n_steps,),
        in_specs=[full(xs), full(st), full(g1), full(w_in),
                  pl.BlockSpec((w_in.shape[0], CAST_ROWS), lambda s: (0, OFF_V // CAST_ROWS + s)),
                  full(w_conv), full(w_out), full(inv_lane)],
        out_specs=[pl.BlockSpec(st.shape, lambda s: (0, 0, 0)),
                   const(feat), const(feat), const(feat), const(feat),
                   const((rows, RET_DIM)), const((rows, CONV_DIM)),
                   pl.BlockSpec((CAST_ROWS, w_in.shape[1]), lambda s: (s, 0)),
                   pl.BlockSpec((CAST_ROWS, w_out.shape[1]), lambda s: (s, 0)),
                   pl.BlockSpec((CAST_ROWS, w_in.shape[0]), lambda s: (s, 0))],
        out_shape=[jax.ShapeDtypeStruct(st.shape, F32),
                   jax.ShapeDtypeStruct(feat, F32), jax.ShapeDtypeStruct(feat, F32),
                   jax.ShapeDtypeStruct(feat, F32), jax.ShapeDtypeStruct(feat, F32),
                   jax.ShapeDtypeStruct((rows, RET_DIM), F32), jax.ShapeDtypeStruct((rows, CONV_DIM), BF16),
                   jax.ShapeDtypeStruct(w_in.shape, BF16), jax.ShapeDtypeStruct(w_out.shape, BF16),
                   jax.ShapeDtypeStruct((2 * RET_DIM, w_in.shape[0]), BF16)],
        compiler_params=pltpu.CompilerParams(
            dimension_semantics=("arbitrary",), vmem_limit_bytes=VMEM_LIMIT),
        name="sample_front",
    )(xs, st, g1, w_in, w_in, w_conv, w_out, inv_lane)


def kernel(x_prompt, x_sample, state_conv, state_ret, meta_tokens, norm1_g, w_in, w_conv, ret_norm_g,
           w_out, norm2_g, w_gate, w_up, w_down, final_norm_g):
    n_p, seq, _ = x_prompt.shape
    n_s, dec_seq, _ = x_sample.shape
    rows_s = n_s * dec_seq
    assert norm1_g.shape[0] == 1 and seq % TILE == 0 and n_p % SEQS_PER_STEP == 0
    assert (n_p * seq) % FFN_TILE == 0
    assert n_s % LANES == 0 and rows_s % FFN_CHUNK == 0

    g1, g2, gf = norm1_g[0][None], norm2_g[0][None], final_norm_g[None]
    lg_lane = jnp.asarray(_LOG_GAMMA[np.arange(RET_DIM) // HEAD_DIM][None])
    inv = ROPE_BASE ** (-jnp.arange(HALF_DIM, dtype=F32) / HALF_DIM)
    inv_lane = jnp.tile(inv, LANES // HALF_DIM)[None]

    gain = ret_norm_g[0][None]

    w_conv_rows = jnp.transpose(w_conv, (1, 0, 2))
    conv_s, qd_t, kd_t, v_t, o_intra, zg, cvo, w_in_b, w_out_b, w_vg_t = _sample_front(
        x_sample, state_conv[0], g1, w_in[0], w_conv_rows, w_out[0], inv_lane, dec_seq)

    x1_p, conv_p, ret_p, wg_b, wu_b, wd_b = _prompt_mixer(
        x_prompt, meta_tokens, g1, w_in_b, w_vg_t, w_conv_rows, gain, w_out_b, lg_lane, inv_lane,
        w_gate[0], w_up[0], w_down[0])

    s4 = jnp.transpose(state_ret[0], (1, 2, 3, 0))
    y_p, y_sample, s4_new = _ffn(x1_p.reshape(n_p * seq, D_MODEL), x_sample, qd_t, kd_t, v_t, o_intra, zg,
                                 cvo, s4, gain, w_out_b, g2, wg_b, wu_b, wd_b, gf, dec_seq)
    y_prompt = y_p.reshape(n_p, seq, D_MODEL)
    ret_s = jnp.transpose(s4_new, (3, 0, 1, 2))

    return (y_prompt, y_sample, conv_p[None], ret_p[None], conv_s[None], ret_s[None])
```

```python
import functools

import numpy as np
import jax
import jax.numpy as jnp
from jax import lax
from jax.experimental import pallas as pl
from jax.experimental.pallas import tpu as pltpu

D_MODEL = 1024
N_META = 16
CONV_DIM = 512
RET_HEADS = 8
HEAD_DIM = 64
HALF_DIM = HEAD_DIM // 2
RET_DIM = RET_HEADS * HEAD_DIM
D_FF = 2816
PAST_LEN = 16384
ROPE_BASE = 10000.0
EPS = 1e-6
GN_EPS = 1e-5

OFF_B, OFF_C, OFF_H, OFF_Q, OFF_K, OFF_V, OFF_G = (i * 512 for i in range(7))

LANES = 128
GROUP = 256
HEADS_PER_GROUP = GROUP // HEAD_DIM
N_GROUPS = RET_DIM // GROUP
TILE = 256
SEQS_PER_STEP = 4
FFN_TILE = 512
FFN_CHUNK = 256
CAST_ROWS = 512
PHASE_LAG = 3
FFN_PHASE_LAG = 1
BF16_ROWS = 16
VMEM_LIMIT = 56 * 1024 * 1024

F32 = jnp.float32
BF16 = jnp.bfloat16

_LOG_GAMMA = np.log1p(-(2.0 ** (-5.0 - np.arange(RET_HEADS)))).astype(np.float32)


def _dot(a, b):
    return jnp.dot(a, b, preferred_element_type=F32)


def _dot_nt(a, b):
    return lax.dot_general(a, b, (((1,), (1,)), ((), ())), preferred_element_type=F32)


def _dot_tn(a, b):
    return lax.dot_general(a, b, (((0,), (0,)), ((), ())), preferred_element_type=F32)


def _rmsnorm(x, g):
    ms = jnp.mean(x * x, axis=-1, keepdims=True)
    return x * lax.rsqrt(ms + EPS) * g


def _silu(x):
    return x * jax.nn.sigmoid(x)


def _iota(shape, dim):
    return lax.broadcasted_iota(jnp.int32, shape, dim)


def _first_half_lanes():
    return (_iota((1, LANES), 1) & (HEAD_DIM - 1)) < HALF_DIM


def _rope(t, cos, signed_sin):
    first_half = _first_half_lanes()
    out = []
    for b in range(t.shape[1] // LANES):
        blk = t[:, b * LANES:(b + 1) * LANES]
        partner = jnp.where(first_half, pltpu.roll(blk, LANES - HALF_DIM, axis=1),
                            pltpu.roll(blk, HALF_DIM, axis=1))
        out.append(blk * cos + partner * signed_sin)
    return jnp.concatenate(out, axis=1)


def _ones_block():
    r = _iota((GROUP, GROUP), 0) >> 6
    c = _iota((GROUP, GROUP), 1) >> 6
    return jnp.where(r == c, 1.0, 0.0).astype(BF16)


def _trace_staggered(phase_iters, lag):
    live = list(range(len(phase_iters)))
    tick = 0
    while live:
        for k in list(live):
            if tick >= lag * k:
                try:
                    next(phase_iters[k])
                except StopIteration:
                    live.remove(k)
        tick += 1


def _prompt_mixer_kernel(x_ref, meta_ref, g1_ref, win_ref, wvgt_ref, wconv_ref, gain_ref, wout_ref,
                         lg_ref, inv_ref, wg32_ref, wu32_ref, wd32_ref,
                         x1_ref, convst_ref, retst_ref, wg_ref, wu_ref, wd_ref,
                         cos_ref, sin_ref, qdec_ref, kdec_ref, dmask_ref,
                         smask_ref, sdec_ref, gaincol_ref, tail_ref, state_ref, ubuf_ref, *, n_tiles):
    j = pl.program_id(0)
    p = pl.program_id(1)
    C = TILE
    k_scale = HEAD_DIM ** -0.5

    def proj(xn, off):
        return _dot(xn, win_ref[:, off:off + 512])

    @pl.when((j == 0) & (p == 0))
    def _init():
        i_f = _iota((C, 1), 0).astype(F32)
        cos_ref[...] = jnp.cos(i_f * inv_ref[...])
        sin_ref[...] = jnp.sin(i_f * inv_ref[...])
        lg = lg_ref[...]
        qdec_ref[...] = jnp.exp((i_f + 1.0) * lg)
        kdec_ref[...] = jnp.exp((C - 1.0 - i_f) * lg) * k_scale
        sdec_ref[...] = jnp.exp(C * lg)
        gaincol_ref[...] = jnp.broadcast_to(gain_ref[...], (C, RET_DIM)).T
        diff = (_iota((C, C), 1) - _iota((C, C), 0)).astype(F32)
        for h in range(RET_HEADS):
            dmask_ref[h] = jnp.where(
                diff >= 0, jnp.exp(jnp.maximum(diff, 0.0) * float(_LOG_GAMMA[h])) * k_scale, 0.0)
        smask = _ones_block().astype(F32)
        smask_ref[...] = smask

        xm = _rmsnorm(meta_ref[...], g1_ref[...]).astype(BF16)
        um = proj(xm, OFF_C) * proj(xm, OFF_H)
        vm = proj(xm, OFF_V).astype(BF16)
        m_f = _iota((N_META, 1), 0).astype(F32)
        ang = m_f * inv_ref[...]
        sin = jnp.sin(ang)
        kmr = _rope(proj(xm, OFF_K), jnp.cos(ang), jnp.where(_first_half_lanes(), -sin, sin))
        kmd = (kmr * (jnp.exp((N_META - 1.0 - m_f) * lg) * k_scale)).astype(BF16)
        for b in range(tail_ref.shape[0]):
            tail_ref[b] = um[N_META - 2:N_META, :]
            for g in range(N_GROUPS):
                sl = slice(g * GROUP, (g + 1) * GROUP)
                state_ref[b, g] = _dot_tn(vm[:, sl], kmd[:, sl]) * smask

    off = (N_META + j * C).astype(F32) * inv_ref[...]
    cos_off, sin_off = jnp.cos(off), jnp.sin(off)
    cos_row, sin_row = cos_ref[...], sin_ref[...]
    cos = cos_row * cos_off - sin_row * sin_off
    sin = sin_row * cos_off + cos_row * sin_off
    sin = jnp.where(_first_half_lanes(), -sin, sin)
    head_of_lane = _iota((1, GROUP), 1) >> 6
    smask = smask_ref[...]
    wc = wconv_ref[...]

    def mix_sequence(slot):
        n = p * SEQS_PER_STEP + slot
        x = x_ref[slot]
        xn = _rmsnorm(x, g1_ref[...]).astype(BF16)

        zb = proj(xn, OFF_B)
        yield
        u = proj(xn, OFF_C) * proj(xn, OFF_H)
        ubuf_ref[slot, 6:8, :] = tail_ref[n]
        ubuf_ref[slot, 8:8 + C, :] = u
        conv = wc[0] * ubuf_ref[slot, 6:6 + C, :] + wc[1] * ubuf_ref[slot, 7:7 + C, :] + wc[2] * u
        conv_out = (zb * conv).astype(BF16)
        new_tail = u[C - 2:C, :]
        tail_ref[n] = new_tail

        yield

        qr = _rope(proj(xn, OFF_Q), cos, sin)
        yield
        kr = _rope(proj(xn, OFF_K), cos, sin)
        yield
        vg_t = _dot_nt(wvgt_ref[...], xn)
        v_t = vg_t[:RET_DIM].astype(BF16)
        yield
        qb = qr.astype(BF16)
        kb = kr.astype(BF16)
        qd = (qr * qdec_ref[...]).astype(BF16)
        kd = (kr * kdec_ref[...]).astype(BF16)
        o_parts = []
        new_states = []
        for g in range(N_GROUPS):
            sl = slice(g * GROUP, (g + 1) * GROUP)
            state_t = state_ref[n, g]
            cross_t = _dot_nt(state_t.astype(BF16), qd[:, sl])
            k_heads = jnp.concatenate(
                [jnp.where(head_of_lane == hh, kb[:, sl], jnp.zeros_like(kb[:, sl]))
                 for hh in range(HEADS_PER_GROUP)], axis=0)
            decay = dmask_ref[g * HEADS_PER_GROUP:(g + 1) * HEADS_PER_GROUP].reshape(HEADS_PER_GROUP * C, C)
            scores_t = (_dot_nt(k_heads, qb[:, sl]) * decay).astype(BF16)
            for hh in range(HEADS_PER_GROUP):
                h = g * HEADS_PER_GROUP + hh
                rows = slice(hh * HEAD_DIM, (hh + 1) * HEAD_DIM)
                o_parts.append(cross_t[rows] + _dot(v_t[h * HEAD_DIM:(h + 1) * HEAD_DIM],
                                                    scores_t[hh * C:(hh + 1) * C]))
            new_state = state_t * sdec_ref[:, sl] + _dot(v_t[sl], kd[:, sl]) * smask
            state_ref[n, g] = new_state
            new_states.append(new_state)
            yield

        normed = []
        for h, o_h in enumerate(o_parts):
            mu = jnp.sum(o_h, axis=0, keepdims=True) * (1.0 / HEAD_DIM)
            dlt = o_h - mu
            var = jnp.sum(dlt * dlt, axis=0, keepdims=True) * (1.0 / HEAD_DIM)
            normed.append(dlt * lax.rsqrt(var + GN_EPS) * gaincol_ref[h * HEAD_DIM:(h + 1) * HEAD_DIM, :])
        ret_out_t = (_silu(vg_t[RET_DIM:]) * jnp.concatenate(normed, axis=0)).astype(BF16)
        yield

        x1_ref[slot] = (x + _dot(conv_out, wout_ref[:CONV_DIM, :])
                        + _dot_tn(ret_out_t, wout_ref[CONV_DIM:, :]))
        finals[slot] = (new_tail, new_states)

    finals = [None] * SEQS_PER_STEP
    _trace_staggered([mix_sequence(slot) for slot in range(SEQS_PER_STEP)], PHASE_LAG)

    wg_ref[...] = wg32_ref[...].astype(BF16)
    wu_ref[...] = wu32_ref[...].astype(BF16)
    wd_ref[...] = wd32_ref[...].astype(BF16)

    @pl.when(j == n_tiles - 1)
    def _final_states():
        for slot, (new_tail, new_states) in enumerate(finals):
            n = p * SEQS_PER_STEP + slot
            convst_ref[n] = new_tail
            for g in range(N_GROUPS):
                state = new_states[g].T
                for hh in range(HEADS_PER_GROUP):
                    blk = slice(hh * HEAD_DIM, (hh + 1) * HEAD_DIM)
                    retst_ref[n, g * HEADS_PER_GROUP + hh] = state[blk, blk]


def _prompt_mixer(x_prompt, meta, g1, w_in, w_vg_t, w_conv, gain, w_out, lg_lane, inv_lane,
                  w_gate, w_up, w_down):
    n_seq, seq, _ = x_prompt.shape
    n_tiles = seq // TILE
    pairs = n_seq // SEQS_PER_STEP
    n_steps = n_tiles * pairs
    full = lambda a: pl.BlockSpec(a.shape, lambda j, p: (0,) * a.ndim, pipeline_mode=pl.Buffered(1))
    gu_rows = w_gate.shape[0] // n_steps
    d_rows = w_down.shape[0] // (n_steps // 2)
    assert gu_rows * n_steps == w_gate.shape[0] and d_rows * (n_steps // 2) == w_down.shape[0]
    assert gu_rows % BF16_ROWS == 0 and d_rows % BF16_ROWS == 0
    gu_slab = pl.BlockSpec((gu_rows, w_gate.shape[1]), lambda j, p: (j * pairs + p, 0))
    d_slab = pl.BlockSpec((d_rows, w_down.shape[1]), lambda j, p: ((j * pairs + p) // 2, 0))
    return pl.pallas_call(
        functools.partial(_prompt_mixer_kernel, n_tiles=n_tiles),
        grid=(n_tiles, pairs),
        in_specs=[pl.BlockSpec((SEQS_PER_STEP, TILE, D_MODEL), lambda j, p: (p, j, 0)),
                  full(meta), full(g1), full(w_in), full(w_vg_t), full(w_conv), full(gain),
                  full(w_out), full(lg_lane), full(inv_lane), gu_slab, gu_slab, d_slab],
        out_specs=[pl.BlockSpec((SEQS_PER_STEP, TILE, D_MODEL), lambda j, p: (p, j, 0)),
                   pl.BlockSpec((n_seq, 2, CONV_DIM), lambda j, p: (0, 0, 0)),
                   pl.BlockSpec((n_seq, RET_HEADS, HEAD_DIM, HEAD_DIM), lambda j, p: (0, 0, 0, 0)),
                   gu_slab, gu_slab, d_slab],
        out_shape=[jax.ShapeDtypeStruct((n_seq, seq, D_MODEL), F32),
                   jax.ShapeDtypeStruct((n_seq, 2, CONV_DIM), F32),
                   jax.ShapeDtypeStruct((n_seq, RET_HEADS, HEAD_DIM, HEAD_DIM), F32),
                   jax.ShapeDtypeStruct(w_gate.shape, BF16), jax.ShapeDtypeStruct(w_up.shape, BF16),
                   jax.ShapeDtypeStruct(w_down.shape, BF16)],
        scratch_shapes=[
            pltpu.VMEM((TILE, LANES), F32),
            pltpu.VMEM((TILE, LANES), F32),
            pltpu.VMEM((TILE, RET_DIM), F32),
            pltpu.VMEM((TILE, RET_DIM), F32),
            pltpu.VMEM((RET_HEADS, TILE, TILE), F32),
            pltpu.VMEM((GROUP, GROUP), F32),
            pltpu.VMEM((1, RET_DIM), F32),
            pltpu.VMEM((RET_DIM, TILE), F32),
            pltpu.VMEM((n_seq, 2, CONV_DIM), F32),
            pltpu.VMEM((n_seq, N_GROUPS, GROUP, GROUP), F32),
            pltpu.VMEM((SEQS_PER_STEP, TILE + 8, CONV_DIM), F32),
        ],
        compiler_params=pltpu.CompilerParams(
            dimension_semantics=("arbitrary", "arbitrary"), vmem_limit_bytes=VMEM_LIMIT),
        name="prompt_mixer",
    )(x_prompt, meta, g1, w_in, w_vg_t, w_conv, gain, w_out, lg_lane, inv_lane, w_gate, w_up, w_down)


def _ffn_kernel(xp_ref, xs_ref, qd_ref, kd_ref, v_ref, ointra_ref, zg_ref, cvo_ref, s_ref, gain_ref,
                sdec_ref, wout_ref, g2_ref, wg_ref, wu_ref, wd_ref, gf_ref,
                yp_ref, ys_ref, snew_ref, o_ref, x2_ref, *, prompt_steps, n_seq, dec_seq, e_piece):
    i = pl.program_id(0)
    L, B = dec_seq, n_seq
    pieces_per_head = HEAD_DIM // e_piece

    def chunk_phases(load, finish, rows):
        x = load(rows)
        xn = _rmsnorm(x, g2_ref[...]).astype(BF16)
        yield
        gate = _dot(xn, wg_ref[...])
        yield
        up = _dot(xn, wu_ref[...])
        yield
        hidden = (_silu(gate) * up).astype(BF16)
        yield
        finish(rows, x + _dot(hidden, wd_ref[...]))

    def ffn_chunks(load, finish, n_rows, before=(), after=()):
        chunks = [slice(r, r + FFN_CHUNK) for r in range(0, n_rows, FFN_CHUNK)]
        _trace_staggered(list(before) + [chunk_phases(load, finish, rows) for rows in chunks] + list(after),
                         FFN_PHASE_LAG)

    def previous_tile_norm():
        yp_ref[...] = _rmsnorm(x2_ref[...], gf_ref[...])
        yield

    def stash_prenorm(rows, x2):
        x2_ref[rows, :] = x2

    def write_sample(rows, x2):
        y = _rmsnorm(x2, gf_ref[...])
        for k in range(FFN_CHUNK // B):
            ys_ref[:, rows.start // B + k, :] = y[k * B:(k + 1) * B]

    def state_piece():
        h = i // pieces_per_head
        base = pl.multiple_of(h * HEAD_DIM, HEAD_DIM)
        rows = pl.ds(pl.multiple_of(base + (i % pieces_per_head) * e_piece, e_piece), e_piece)
        qd_h = qd_ref[pl.ds(base, HEAD_DIM), :]
        kd_h = kd_ref[pl.ds(base, HEAD_DIM), :]
        state_decay = sdec_ref[h]
        v_blk = [v_ref[rows, j * B:(j + 1) * B] for j in range(L)]
        cross = [jnp.zeros((e_piece, B), F32) for _ in range(L)]
        for d in range(HEAD_DIM):
            s_de = s_ref[0, d]
            new = s_de * state_decay
            for t in range(L):
                lanes = slice(t * B, (t + 1) * B)
                cross[t] = cross[t] + qd_h[d:d + 1, lanes] * s_de
                new = new + kd_h[d:d + 1, lanes] * v_blk[t]
            snew_ref[0, d] = new
        for t in range(L):
            o_ref[rows, t * B:(t + 1) * B] = o_ref[rows, t * B:(t + 1) * B] + cross[t]
        yield

    @pl.when(i == 0)
    def _seed():
        o_ref[...] = ointra_ref[...]
        x2_ref[...] = jnp.zeros_like(x2_ref)

    @pl.when(i < prompt_steps)
    def _prompt_rows():
        ffn_chunks(lambda rows: xp_ref[rows, :], stash_prenorm, FFN_TILE,
                   before=[previous_tile_norm()], after=[state_piece()])

    @pl.when(i == prompt_steps)
    def _sample_rows():
        gain_col = jnp.broadcast_to(gain_ref[...], (L * B, RET_DIM)).T
        for hd in range(RET_HEADS):
            r0 = hd * HEAD_DIM
            o_h = o_ref[r0:r0 + HEAD_DIM, :]
            mu = jnp.sum(o_h, axis=0, keepdims=True) * (1.0 / HEAD_DIM)
            dlt = o_h - mu
            var = jnp.sum(dlt * dlt, axis=0, keepdims=True) * (1.0 / HEAD_DIM)
            o_ref[r0:r0 + HEAD_DIM, :] = dlt * lax.rsqrt(var + GN_EPS) * gain_col[r0:r0 + HEAD_DIM, :]
        ret_out = (_silu(zg_ref[...]) * o_ref[...].T).astype(BF16)
        xs = jnp.concatenate([xs_ref[:, t, :] for t in range(L)], axis=0)
        x1 = (xs + _dot(cvo_ref[...], wout_ref[:CONV_DIM, :])
              + _dot(ret_out, wout_ref[CONV_DIM:, :]))
        ffn_chunks(lambda rows: x1[rows], write_sample, L * B, before=[previous_tile_norm()])


def _ffn(x_p, xs, qd_t, kd_t, v_t, o_intra, zg, cvo, s4, gain, w_out_b, g2, w_gate, w_up, w_down, gf,
         dec_seq):
    prompt_steps = x_p.shape[0] // FFN_TILE
    n_seq = xs.shape[0]
    assert prompt_steps % RET_HEADS == 0 and FFN_CHUNK % n_seq == 0
    e_piece = HEAD_DIM // (prompt_steps // RET_HEADS)
    assert e_piece % 8 == 0 and HEAD_DIM % e_piece == 0
    pieces_per_head = HEAD_DIM // e_piece
    sdec = jnp.asarray(np.exp(np.float32(dec_seq) * _LOG_GAMMA))
    full = lambda a: pl.BlockSpec(a.shape, lambda i: (0,) * a.ndim, pipeline_mode=pl.Buffered(1))
    prompt_tile = pl.BlockSpec((FFN_TILE, D_MODEL), lambda i: (jnp.minimum(i, prompt_steps - 1), 0))
    prompt_out = pl.BlockSpec((FFN_TILE, D_MODEL), lambda i: (jnp.maximum(i - 1, 0), 0))
    sample_out = pl.BlockSpec(xs.shape, lambda i: (0, 0, 0))

    def piece_index(i):
        piece = jnp.minimum(i, prompt_steps - 1)
        return (piece // pieces_per_head, 0, piece % pieces_per_head, 0)

    state_piece = pl.BlockSpec((1, HEAD_DIM, e_piece, n_seq), piece_index)
    return pl.pallas_call(
        functools.partial(_ffn_kernel, prompt_steps=prompt_steps, n_seq=n_seq, dec_seq=dec_seq,
                          e_piece=e_piece),
        grid=(prompt_steps + 1,),
        in_specs=[prompt_tile, full(xs), full(qd_t), full(kd_t), full(v_t), full(o_intra), full(zg),
                  full(cvo), state_piece, full(gain), pl.BlockSpec(memory_space=pltpu.SMEM), full(w_out_b),
                  full(g2),
                  full(w_gate), full(w_up), full(w_down), full(gf)],
        out_specs=[prompt_out, sample_out, state_piece],
        out_shape=[jax.ShapeDtypeStruct(x_p.shape, F32), jax.ShapeDtypeStruct(xs.shape, F32),
                   jax.ShapeDtypeStruct(s4.shape, F32)],
        scratch_shapes=[pltpu.VMEM(o_intra.shape, F32),
                        pltpu.VMEM((FFN_TILE, D_MODEL), F32)],
        compiler_params=pltpu.CompilerParams(
            dimension_semantics=("arbitrary",), vmem_limit_bytes=VMEM_LIMIT),
        name="ffn",
    )(x_p, xs, qd_t, kd_t, v_t, o_intra, zg, cvo, s4, gain, sdec, w_out_b, g2, w_gate, w_up, w_down, gf)


def _gamma_pow(head, power):
    return float(np.exp(np.float32(power) * _LOG_GAMMA[head]))


def _sample_front_kernel(x_ref, st_ref, g1_ref, win32_ref, wvg32_ref, wconv_ref, wout32_ref, inv_ref,
                         convst_ref, qd_ref, kd_ref, v_ref, o_ref, zg_ref, cvo_ref,
                         win_ref, wout_ref, wvgt_ref, *, n_seq, dec_seq):
    s = pl.program_id(0)
    L, B = dec_seq, n_seq
    k_scale = HEAD_DIM ** -0.5

    slab = pl.ds(pl.multiple_of(s * CAST_ROWS, CAST_ROWS), CAST_ROWS)
    win_ref[...] = win32_ref[slab, :].astype(BF16)
    wout_ref[...] = wout32_ref[slab, :].astype(BF16)
    wvgt_ref[...] = wvg32_ref[...].T.astype(BF16)

    @pl.when(s == 0)
    def _dense_front():
        x = jnp.concatenate([x_ref[:, t, :] for t in range(L)], axis=0)
        xn = _rmsnorm(x, g1_ref[...]).astype(BF16)

        def proj(off):
            return _dot(xn, win32_ref[:, off:off + 512].astype(BF16))

        zb = proj(OFF_B)
        u = proj(OFF_C) * proj(OFF_H)
        ext = [st_ref[:, 0, :], st_ref[:, 1, :]] + [u[i * B:(i + 1) * B] for i in range(L)]
        wc = wconv_ref[...]
        for i in range(L):
            conv = wc[0] * ext[i] + wc[1] * ext[i + 1] + wc[2] * ext[i + 2]
            cvo_ref[i * B:(i + 1) * B, :] = (zb[i * B:(i + 1) * B] * conv).astype(BF16)
        convst_ref[:, 0, :] = ext[L]
        convst_ref[:, 1, :] = ext[L + 1]
        zg_ref[...] = proj(OFF_G)

        q_t = proj(OFF_Q).T
        k_t = proj(OFF_K).T
        v_t = proj(OFF_V).T
        v_ref[...] = v_t
        inv = jnp.broadcast_to(inv_ref[...], (B, LANES)).T[:HALF_DIM]
        cos_sin = [(jnp.cos(float(PAST_LEN + i) * inv), jnp.sin(float(PAST_LEN + i) * inv))
                   for i in range(L)]

        def rope(t, r0, i):
            cos, sin = cos_sin[i]
            t1 = t[r0:r0 + HALF_DIM, i * B:(i + 1) * B]
            t2 = t[r0 + HALF_DIM:r0 + HEAD_DIM, i * B:(i + 1) * B]
            return jnp.concatenate([t1 * cos - t2 * sin, t2 * cos + t1 * sin], axis=0)

        for hd in range(RET_HEADS):
            r0 = hd * HEAD_DIM
            qr = [rope(q_t, r0, i) for i in range(L)]
            kr = [rope(k_t, r0, i) for i in range(L)]
            for i in range(L):
                lanes = slice(i * B, (i + 1) * B)
                qd_ref[r0:r0 + HEAD_DIM, lanes] = qr[i] * _gamma_pow(hd, i + 1)
                kd_ref[r0:r0 + HEAD_DIM, lanes] = kr[i] * (_gamma_pow(hd, L - 1 - i) * k_scale)
                intra = jnp.zeros((HEAD_DIM, B), F32)
                for j in range(i + 1):
                    score = jnp.sum(qr[i] * kr[j], axis=0, keepdims=True) * (_gamma_pow(hd, i - j) * k_scale)
                    intra = intra + score * v_t[r0:r0 + HEAD_DIM, j * B:(j + 1) * B]
                o_ref[r0:r0 + HEAD_DIM, lanes] = intra


def _sample_front(xs, st, g1, w_in, w_conv, w_out, inv_lane, dec_seq):
    n_seq = xs.shape[0]
    rows = n_seq * dec_seq
    n_steps = w_in.shape[0] // CAST_ROWS
    assert w_out.shape[0] == n_steps * CAST_ROWS and 2 * RET_DIM == n_steps * CAST_ROWS
    full = lambda a: pl.BlockSpec(a.shape, lambda s: (0,) * a.ndim, pipeline_mode=pl.Buffered(1))
    const = lambda shape: pl.BlockSpec(shape, lambda s: (0, 0))
    feat = (RET_DIM, rows)
    return pl.pallas_call(
        functools.partial(_sample_front_kernel, n_seq=n_seq, dec_seq=dec_seq),
        grid=(n_steps,),
        in_specs=[full(xs), full(st), full(g1), full(w_in),
                  pl.BlockSpec((w_in.shape[0], CAST_ROWS), lambda s: (0, OFF_V // CAST_ROWS + s)),
                  full(w_conv), full(w_out), full(inv_lane)],
        out_specs=[pl.BlockSpec(st.shape, lambda s: (0, 0, 0)),
                   const(feat), const(feat), const(feat), const(feat),
                   const((rows, RET_DIM)), const((rows, CONV_DIM)),
                   pl.BlockSpec((CAST_ROWS, w_in.shape[1]), lambda s: (s, 0)),
                   pl.BlockSpec((CAST_ROWS, w_out.shape[1]), lambda s: (s, 0)),
                   pl.BlockSpec((CAST_ROWS, w_in.shape[0]), lambda s: (s, 0))],
        out_shape=[jax.ShapeDtypeStruct(st.shape, F32),
                   jax.ShapeDtypeStruct(feat, F32), jax.ShapeDtypeStruct(feat, F32),
                   jax.ShapeDtypeStruct(feat, F32), jax.ShapeDtypeStruct(feat, F32),
                   jax.ShapeDtypeStruct((rows, RET_DIM), F32), jax.ShapeDtypeStruct((rows, CONV_DIM), BF16),
                   jax.ShapeDtypeStruct(w_in.shape, BF16), jax.ShapeDtypeStruct(w_out.shape, BF16),
                   jax.ShapeDtypeStruct((2 * RET_DIM, w_in.shape[0]), BF16)],
        compiler_params=pltpu.CompilerParams(
            dimension_semantics=("arbitrary",), vmem_limit_bytes=VMEM_LIMIT),
        name="sample_front",
    )(xs, st, g1, w_in, w_in, w_conv, w_out, inv_lane)


def kernel(x_prompt, x_sample, state_conv, state_ret, meta_tokens, norm1_g, w_in, w_conv, ret_norm_g,
           w_out, norm2_g, w_gate, w_up, w_down, final_norm_g):
    n_p, seq, _ = x_prompt.shape
    n_s, dec_seq, _ = x_sample.shape
    rows_s = n_s * dec_seq
    assert norm1_g.shape[0] == 1 and seq % TILE == 0 and n_p % SEQS_PER_STEP == 0
    assert (n_p * seq) % FFN_TILE == 0
    assert n_s % LANES == 0 and rows_s % FFN_CHUNK == 0

    g1, g2, gf = norm1_g[0][None], norm2_g[0][None], final_norm_g[None]
    lg_lane = jnp.asarray(_LOG_GAMMA[np.arange(RET_DIM) // HEAD_DIM][None])
    inv = ROPE_BASE ** (-jnp.arange(HALF_DIM, dtype=F32) / HALF_DIM)
    inv_lane = jnp.tile(inv, LANES // HALF_DIM)[None]

    gain = ret_norm_g[0][None]

    w_conv_rows = jnp.transpose(w_conv, (1, 0, 2))
    conv_s, qd_t, kd_t, v_t, o_intra, zg, cvo, w_in_b, w_out_b, w_vg_t = _sample_front(
        x_sample, state_conv[0], g1, w_in[0], w_conv_rows, w_out[0], inv_lane, dec_seq)

    x1_p, conv_p, ret_p, wg_b, wu_b, wd_b = _prompt_mixer(
        x_prompt, meta_tokens, g1, w_in_b, w_vg_t, w_conv_rows, gain, w_out_b, lg_lane, inv_lane,
        w_gate[0], w_up[0], w_down[0])

    s4 = jnp.transpose(state_ret[0], (1, 2, 3, 0))
    y_p, y_sample, s4_new = _ffn(x1_p.reshape(n_p * seq, D_MODEL), x_sample, qd_t, kd_t, v_t, o_intra, zg,
                                 cvo, s4, gain, w_out_b, g2, wg_b, wu_b, wd_b, gf, dec_seq)
    y_prompt = y_p.reshape(n_p, seq, D_MODEL)
    ret_s = jnp.transpose(s4_new, (3, 0, 1, 2))

    return (y_prompt, y_sample, conv_p[None], ret_p[None], conv_s[None], ret_s[None])
```

```python
import functools

import numpy as np
import jax
import jax.numpy as jnp
from jax import lax
from jax.experimental import pallas as pl
from jax.experimental.pallas import tpu as pltpu

D_MODEL = 1024
N_META = 16
CONV_DIM = 512
RET_HEADS = 8
HEAD_DIM = 64
HALF_DIM = HEAD_DIM // 2
RET_DIM = RET_HEADS * HEAD_DIM
D_FF = 2816
PAST_LEN = 16384
ROPE_BASE = 10000.0
EPS = 1e-6
GN_EPS = 1e-5

OFF_B, OFF_C, OFF_H, OFF_Q, OFF_K, OFF_V, OFF_G = (i * 512 for i in range(7))

LANES = 128
GROUP = 256
HEADS_PER_GROUP = GROUP // HEAD_DIM
N_GROUPS = RET_DIM // GROUP
TILE = 256
SEQS_PER_STEP = 4
FFN_TILE = 512
FFN_CHUNK = 256
SECTION = 512
CAST_ROWS = 256
PHASE_LAG = 3
FFN_PHASE_LAG = 1
BF16_ROWS = 16
VMEM_LIMIT = 56 * 1024 * 1024

F32 = jnp.float32
BF16 = jnp.bfloat16

_LOG_GAMMA = np.log1p(-(2.0 ** (-5.0 - np.arange(RET_HEADS)))).astype(np.float32)


def _dot(a, b):
    return jnp.dot(a, b, preferred_element_type=F32)


def _dot_nt(a, b):
    return lax.dot_general(a, b, (((1,), (1,)), ((), ())), preferred_element_type=F32)


def _dot_tn(a, b):
    return lax.dot_general(a, b, (((0,), (0,)), ((), ())), preferred_element_type=F32)


def _rmsnorm(x, g):
    ms = jnp.mean(x * x, axis=-1, keepdims=True)
    return x * lax.rsqrt(ms + EPS) * g


def _silu(x):
    return x * jax.nn.sigmoid(x)


def _iota(shape, dim):
    return lax.broadcasted_iota(jnp.int32, shape, dim)


def _first_half_lanes():
    return (_iota((1, LANES), 1) & (HEAD_DIM - 1)) < HALF_DIM


def _rope(t, cos, signed_sin):
    first_half = _first_half_lanes()
    out = []
    for b in range(t.shape[1] // LANES):
        blk = t[:, b * LANES:(b + 1) * LANES]
        partner = jnp.where(first_half, pltpu.roll(blk, LANES - HALF_DIM, axis=1),
                            pltpu.roll(blk, HALF_DIM, axis=1))
        out.append(blk * cos + partner * signed_sin)
    return jnp.concatenate(out, axis=1)


def _ones_block():
    r = _iota((GROUP, GROUP), 0) >> 6
    c = _iota((GROUP, GROUP), 1) >> 6
    return jnp.where(r == c, 1.0, 0.0).astype(BF16)


def _trace_staggered(phase_iters, lag):
    live = list(range(len(phase_iters)))
    tick = 0
    while live:
        for k in list(live):
            if tick >= lag * k:
                try:
                    next(phase_iters[k])
                except StopIteration:
                    live.remove(k)
        tick += 1


def _prompt_mixer_kernel(x_ref, meta_ref, g1_ref, win_ref, wvgt_ref, wconv_ref, gain_ref, wout_ref,
                         lg_ref, inv_ref, wg32_ref, wu32_ref, wd32_ref,
                         x1_ref, convst_ref, retst_ref, wg_ref, wu_ref, wd_ref,
                         cos_ref, sin_ref, qdec_ref, kdec_ref, dmask_ref,
                         smask_ref, sdec_ref, gaincol_ref, tail_ref, state_ref, ubuf_ref, *, n_tiles):
    j = pl.program_id(0)
    p = pl.program_id(1)
    C = TILE
    k_scale = HEAD_DIM ** -0.5

    def proj(xn, off):
        return _dot(xn, win_ref[:, off:off + 512])

    @pl.when((j == 0) & (p == 0))
    def _init():
        i_f = _iota((C, 1), 0).astype(F32)
        cos_ref[...] = jnp.cos(i_f * inv_ref[...])
        sin_ref[...] = jnp.sin(i_f * inv_ref[...])
        lg = lg_ref[...]
        qdec_ref[...] = jnp.exp((i_f + 1.0) * lg)
        kdec_ref[...] = jnp.exp((C - 1.0 - i_f) * lg) * k_scale
        sdec_ref[...] = jnp.exp(C * lg)
        gaincol_ref[...] = jnp.broadcast_to(gain_ref[...], (C, RET_DIM)).T
        diff = (_iota((C, C), 1) - _iota((C, C), 0)).astype(F32)
        for h in range(RET_HEADS):
            dmask_ref[h] = jnp.where(
                diff >= 0, jnp.exp(jnp.maximum(diff, 0.0) * float(_LOG_GAMMA[h])) * k_scale, 0.0)
        smask = _ones_block().astype(F32)
        smask_ref[...] = smask

        xm = _rmsnorm(meta_ref[...], g1_ref[...]).astype(BF16)
        um = proj(xm, OFF_C) * proj(xm, OFF_H)
        vm = proj(xm, OFF_V).astype(BF16)
        m_f = _iota((N_META, 1), 0).astype(F32)
        ang = m_f * inv_ref[...]
        sin = jnp.sin(ang)
        kmr = _rope(proj(xm, OFF_K), jnp.cos(ang), jnp.where(_first_half_lanes(), -sin, sin))
        kmd = (kmr * (jnp.exp((N_META - 1.0 - m_f) * lg) * k_scale)).astype(BF16)
        for b in range(tail_ref.shape[0]):
            tail_ref[b] = um[N_META - 2:N_META, :]
            for g in range(N_GROUPS):
                sl = slice(g * GROUP, (g + 1) * GROUP)
                state_ref[b, g] = _dot_tn(vm[:, sl], kmd[:, sl]) * smask

    off = (N_META + j * C).astype(F32) * inv_ref[...]
    cos_off, sin_off = jnp.cos(off), jnp.sin(off)
    cos_row, sin_row = cos_ref[...], sin_ref[...]
    cos = cos_row * cos_off - sin_row * sin_off
    sin = sin_row * cos_off + cos_row * sin_off
    sin = jnp.where(_first_half_lanes(), -sin, sin)
    head_of_lane = _iota((1, GROUP), 1) >> 6
    smask = smask_ref[...]
    wc = wconv_ref[...]

    def mix_sequence(slot):
        n = p * SEQS_PER_STEP + slot
        x = x_ref[slot]
        xn = _rmsnorm(x, g1_ref[...]).astype(BF16)

        zb = proj(xn, OFF_B)
        yield
        u = proj(xn, OFF_C) * proj(xn, OFF_H)
        ubuf_ref[slot, 6:8, :] = tail_ref[n]
        ubuf_ref[slot, 8:8 + C, :] = u
        conv = wc[0] * ubuf_ref[slot, 6:6 + C, :] + wc[1] * ubuf_ref[slot, 7:7 + C, :] + wc[2] * u
        conv_out = (zb * conv).astype(BF16)
        new_tail = u[C - 2:C, :]
        tail_ref[n] = new_tail

        yield

        qr = _rope(proj(xn, OFF_Q), cos, sin)
        yield
        kr = _rope(proj(xn, OFF_K), cos, sin)
        yield
        vg_t = _dot_nt(wvgt_ref[...], xn)
        v_t = vg_t[:RET_DIM].astype(BF16)
        yield
        qb = qr.astype(BF16)
        kb = kr.astype(BF16)
        qd = (qr * qdec_ref[...]).astype(BF16)
        kd = (kr * kdec_ref[...]).astype(BF16)
        o_parts = []
        new_states = []
        for g in range(N_GROUPS):
            sl = slice(g * GROUP, (g + 1) * GROUP)
            state_t = state_ref[n, g]
            cross_t = _dot_nt(state_t.astype(BF16), qd[:, sl])
            k_heads = jnp.concatenate(
                [jnp.where(head_of_lane == hh, kb[:, sl], jnp.zeros_like(kb[:, sl]))
                 for hh in range(HEADS_PER_GROUP)], axis=0)
            decay = dmask_ref[g * HEADS_PER_GROUP:(g + 1) * HEADS_PER_GROUP].reshape(HEADS_PER_GROUP * C, C)
            scores_t = (_dot_nt(k_heads, qb[:, sl]) * decay).astype(BF16)
            for hh in range(HEADS_PER_GROUP):
                h = g * HEADS_PER_GROUP + hh
                rows = slice(hh * HEAD_DIM, (hh + 1) * HEAD_DIM)
                o_parts.append(cross_t[rows] + _dot(v_t[h * HEAD_DIM:(h + 1) * HEAD_DIM],
                                                    scores_t[hh * C:(hh + 1) * C]))
            new_state = state_t * sdec_ref[:, sl] + _dot(v_t[sl], kd[:, sl]) * smask
            state_ref[n, g] = new_state
            new_states.append(new_state)
            yield

        normed = []
        for h, o_h in enumerate(o_parts):
            mu = jnp.sum(o_h, axis=0, keepdims=True) * (1.0 / HEAD_DIM)
            dlt = o_h - mu
            var = jnp.sum(dlt * dlt, axis=0, keepdims=True) * (1.0 / HEAD_DIM)
            normed.append(dlt * lax.rsqrt(var + GN_EPS) * gaincol_ref[h * HEAD_DIM:(h + 1) * HEAD_DIM, :])
        ret_out_t = (_silu(vg_t[RET_DIM:]) * jnp.concatenate(normed, axis=0)).astype(BF16)
        yield

        x1_ref[slot] = (x + _dot(conv_out, wout_ref[:CONV_DIM, :])
                        + _dot_tn(ret_out_t, wout_ref[CONV_DIM:, :]))
        finals[slot] = (new_tail, new_states)

    finals = [None] * SEQS_PER_STEP
    _trace_staggered([mix_sequence(slot) for slot in range(SEQS_PER_STEP)], PHASE_LAG)

    wg_ref[...] = wg32_ref[...].astype(BF16)
    wu_ref[...] = wu32_ref[...].astype(BF16)
    wd_ref[...] = wd32_ref[...].astype(BF16)

    @pl.when(j == n_tiles - 1)
    def _final_states():
        for slot, (new_tail, new_states) in enumerate(finals):
            n = p * SEQS_PER_STEP + slot
            convst_ref[n] = new_tail
            for g in range(N_GROUPS):
                state = new_states[g].T
                for hh in range(HEADS_PER_GROUP):
                    blk = slice(hh * HEAD_DIM, (hh + 1) * HEAD_DIM)
                    retst_ref[n, g * HEADS_PER_GROUP + hh] = state[blk, blk]


def _prompt_mixer(x_prompt, meta, g1, w_in, w_vg_t, w_conv, gain, w_out, lg_lane, inv_lane,
                  w_gate, w_up, w_down):
    n_seq, seq, _ = x_prompt.shape
    n_tiles = seq // TILE
    pairs = n_seq // SEQS_PER_STEP
    n_steps = n_tiles * pairs
    full = lambda a: pl.BlockSpec(a.shape, lambda j, p: (0,) * a.ndim, pipeline_mode=pl.Buffered(1))
    gu_rows = w_gate.shape[0] // n_steps
    d_rows = w_down.shape[0] // (n_steps // 2)
    assert gu_rows * n_steps == w_gate.shape[0] and d_rows * (n_steps // 2) == w_down.shape[0]
    assert gu_rows % BF16_ROWS == 0 and d_rows % BF16_ROWS == 0
    gu_slab = pl.BlockSpec((gu_rows, w_gate.shape[1]), lambda j, p: (j * pairs + p, 0))
    d_slab = pl.BlockSpec((d_rows, w_down.shape[1]), lambda j, p: ((j * pairs + p) // 2, 0))
    return pl.pallas_call(
        functools.partial(_prompt_mixer_kernel, n_tiles=n_tiles),
        grid=(n_tiles, pairs),
        in_specs=[pl.BlockSpec((SEQS_PER_STEP, TILE, D_MODEL), lambda j, p: (p, j, 0)),
                  full(meta), full(g1), full(w_in), full(w_vg_t), full(w_conv), full(gain),
                  full(w_out), full(lg_lane), full(inv_lane), gu_slab, gu_slab, d_slab],
        out_specs=[pl.BlockSpec((SEQS_PER_STEP, TILE, D_MODEL), lambda j, p: (p, j, 0)),
                   pl.BlockSpec((n_seq, 2, CONV_DIM), lambda j, p: (0, 0, 0)),
                   pl.BlockSpec((n_seq, RET_HEADS, HEAD_DIM, HEAD_DIM), lambda j, p: (0, 0, 0, 0)),
                   gu_slab, gu_slab, d_slab],
        out_shape=[jax.ShapeDtypeStruct((n_seq, seq, D_MODEL), F32),
                   jax.ShapeDtypeStruct((n_seq, 2, CONV_DIM), F32),
                   jax.ShapeDtypeStruct((n_seq, RET_HEADS, HEAD_DIM, HEAD_DIM), F32),
                   jax.ShapeDtypeStruct(w_gate.shape, BF16), jax.ShapeDtypeStruct(w_up.shape, BF16),
                   jax.ShapeDtypeStruct(w_down.shape, BF16)],
        scratch_shapes=[
            pltpu.VMEM((TILE, LANES), F32),
            pltpu.VMEM((TILE, LANES), F32),
            pltpu.VMEM((TILE, RET_DIM), F32),
            pltpu.VMEM((TILE, RET_DIM), F32),
            pltpu.VMEM((RET_HEADS, TILE, TILE), F32),
            pltpu.VMEM((GROUP, GROUP), F32),
            pltpu.VMEM((1, RET_DIM), F32),
            pltpu.VMEM((RET_DIM, TILE), F32),
            pltpu.VMEM((n_seq, 2, CONV_DIM), F32),
            pltpu.VMEM((n_seq, N_GROUPS, GROUP, GROUP), F32),
            pltpu.VMEM((SEQS_PER_STEP, TILE + 8, CONV_DIM), F32),
        ],
        compiler_params=pltpu.CompilerParams(
            dimension_semantics=("arbitrary", "arbitrary"), vmem_limit_bytes=VMEM_LIMIT),
        name="prompt_mixer",
    )(x_prompt, meta, g1, w_in, w_vg_t, w_conv, gain, w_out, lg_lane, inv_lane, w_gate, w_up, w_down)


def _ffn_kernel(xp_ref, xs_ref, qd_ref, kd_ref, v_ref, ointra_ref, zg_ref, cvo_ref, s_ref, gain_ref,
                sdec_ref, wout_ref, g2_ref, wg_ref, wu_ref, wd_ref, gf_ref,
                yp_ref, ys_ref, snew_ref, o_ref, x2_ref, *, prompt_steps, n_seq, dec_seq, e_piece):
    i = pl.program_id(0)
    L, B = dec_seq, n_seq
    pieces_per_head = HEAD_DIM // e_piece

    def chunk_phases(load, finish, rows):
        x = load(rows)
        xn = _rmsnorm(x, g2_ref[...]).astype(BF16)
        yield
        gate = _dot(xn, wg_ref[...])
        yield
        up = _dot(xn, wu_ref[...])
        yield
        hidden = (_silu(gate) * up).astype(BF16)
        yield
        finish(rows, x + _dot(hidden, wd_ref[...]))

    def ffn_chunks(load, finish, n_rows, before=(), after=()):
        chunks = [slice(r, r + FFN_CHUNK) for r in range(0, n_rows, FFN_CHUNK)]
        _trace_staggered(list(before) + [chunk_phases(load, finish, rows) for rows in chunks] + list(after),
                         FFN_PHASE_LAG)

    def previous_tile_norm():
        yp_ref[...] = _rmsnorm(x2_ref[...], gf_ref[...])
        yield

    def stash_prenorm(rows, x2):
        x2_ref[rows, :] = x2

    def write_sample(rows, x2):
        y = _rmsnorm(x2, gf_ref[...])
        for k in range(FFN_CHUNK // B):
            ys_ref[:, rows.start // B + k, :] = y[k * B:(k + 1) * B]

    def state_piece():
        h = i // pieces_per_head
        base = pl.multiple_of(h * HEAD_DIM, HEAD_DIM)
        rows = pl.ds(pl.multiple_of(base + (i % pieces_per_head) * e_piece, e_piece), e_piece)
        qd_h = qd_ref[pl.ds(base, HEAD_DIM), :]
        kd_h = kd_ref[pl.ds(base, HEAD_DIM), :]
        state_decay = sdec_ref[h]
        v_blk = [v_ref[rows, j * B:(j + 1) * B] for j in range(L)]
        cross = [jnp.zeros((e_piece, B), F32) for _ in range(L)]
        for d in range(HEAD_DIM):
            s_de = s_ref[0, d]
            new = s_de * state_decay
            for t in range(L):
                lanes = slice(t * B, (t + 1) * B)
                cross[t] = cross[t] + qd_h[d:d + 1, lanes] * s_de
                new = new + kd_h[d:d + 1, lanes] * v_blk[t]
            snew_ref[0, d] = new
        for t in range(L):
            o_ref[rows, t * B:(t + 1) * B] = o_ref[rows, t * B:(t + 1) * B] + cross[t]
        yield

    @pl.when(i == 0)
    def _seed():
        o_ref[...] = ointra_ref[...]
        x2_ref[...] = jnp.zeros_like(x2_ref)

    @pl.when(i < prompt_steps)
    def _prompt_rows():
        ffn_chunks(lambda rows: xp_ref[rows, :], stash_prenorm, FFN_TILE,
                   before=[previous_tile_norm()], after=[state_piece()])

    @pl.when(i == prompt_steps)
    def _sample_rows():
        gain_col = jnp.broadcast_to(gain_ref[...], (L * B, RET_DIM)).T
        for hd in range(RET_HEADS):
            r0 = hd * HEAD_DIM
            o_h = o_ref[r0:r0 + HEAD_DIM, :]
            mu = jnp.sum(o_h, axis=0, keepdims=True) * (1.0 / HEAD_DIM)
            dlt = o_h - mu
            var = jnp.sum(dlt * dlt, axis=0, keepdims=True) * (1.0 / HEAD_DIM)
            o_ref[r0:r0 + HEAD_DIM, :] = dlt * lax.rsqrt(var + GN_EPS) * gain_col[r0:r0 + HEAD_DIM, :]
        ret_out = (_silu(zg_ref[...]) * o_ref[...].T).astype(BF16)
        xs = jnp.concatenate([xs_ref[:, t, :] for t in range(L)], axis=0)
        x1 = (xs + _dot(cvo_ref[...], wout_ref[:CONV_DIM, :])
              + _dot(ret_out, wout_ref[CONV_DIM:, :]))
        ffn_chunks(lambda rows: x1[rows], write_sample, L * B, before=[previous_tile_norm()])


def _ffn(x_p, xs, qd_t, kd_t, v_t, o_intra, zg, cvo, s4, gain, w_out_b, g2, w_gate, w_up, w_down, gf,
         dec_seq):
    prompt_steps = x_p.shape[0] // FFN_TILE
    n_seq = xs.shape[0]
    assert prompt_steps % RET_HEADS == 0 and FFN_CHUNK % n_seq == 0
    e_piece = HEAD_DIM // (prompt_steps // RET_HEADS)
    assert e_piece % 8 == 0 and HEAD_DIM % e_piece == 0
    pieces_per_head = HEAD_DIM // e_piece
    sdec = jnp.asarray(np.exp(np.float32(dec_seq) * _LOG_GAMMA))
    full = lambda a: pl.BlockSpec(a.shape, lambda i: (0,) * a.ndim, pipeline_mode=pl.Buffered(1))
    prompt_tile = pl.BlockSpec((FFN_TILE, D_MODEL), lambda i: (jnp.minimum(i, prompt_steps - 1), 0))
    prompt_out = pl.BlockSpec((FFN_TILE, D_MODEL), lambda i: (jnp.maximum(i - 1, 0), 0))
    sample_out = pl.BlockSpec(xs.shape, lambda i: (0, 0, 0))

    def piece_index(i):
        piece = jnp.minimum(i, prompt_steps - 1)
        return (piece // pieces_per_head, 0, piece % pieces_per_head, 0)

    state_piece = pl.BlockSpec((1, HEAD_DIM, e_piece, n_seq), piece_index)
    return pl.pallas_call(
        functools.partial(_ffn_kernel, prompt_steps=prompt_steps, n_seq=n_seq, dec_seq=dec_seq,
                          e_piece=e_piece),
        grid=(prompt_steps + 1,),
        in_specs=[prompt_tile, full(xs), full(qd_t), full(kd_t), full(v_t), full(o_intra), full(zg),
                  full(cvo), state_piece, full(gain), pl.BlockSpec(memory_space=pltpu.SMEM), full(w_out_b),
                  full(g2),
                  full(w_gate), full(w_up), full(w_down), full(gf)],
        out_specs=[prompt_out, sample_out, state_piece],
        out_shape=[jax.ShapeDtypeStruct(x_p.shape, F32), jax.ShapeDtypeStruct(xs.shape, F32),
                   jax.ShapeDtypeStruct(s4.shape, F32)],
        scratch_shapes=[pltpu.VMEM(o_intra.shape, F32),
                        pltpu.VMEM((FFN_TILE, D_MODEL), F32)],
        compiler_params=pltpu.CompilerParams(
            dimension_semantics=("arbitrary",), vmem_limit_bytes=VMEM_LIMIT),
        name="ffn",
    )(x_p, xs, qd_t, kd_t, v_t, o_intra, zg, cvo, s4, gain, sdec, w_out_b, g2, w_gate, w_up, w_down, gf)


def _gamma_pow(head, power):
    return float(np.exp(np.float32(power) * _LOG_GAMMA[head]))


def _sample_front_kernel(x_ref, st_ref, g1_ref, win32_ref, wconv_ref, wout32_ref, inv_ref,
                         convst_ref, qd_ref, kd_ref, v_ref, o_ref, zg_ref, cvo_ref,
                         win_ref, wout_ref, wvgt_ref, xn_ref, z_ref, *, n_seq, dec_seq, n_sections):
    s = pl.program_id(0)
    L, B = dec_seq, n_seq
    k_scale = HEAD_DIM ** -0.5

    @pl.when(s == 0)
    def _normalize():
        x = jnp.concatenate([x_ref[:, t, :] for t in range(L)], axis=0)
        xn_ref[...] = _rmsnorm(x, g1_ref[...]).astype(BF16)

    w_section = win32_ref[...].astype(BF16)
    win_ref[...] = w_section
    wout_ref[...] = wout32_ref[...].astype(BF16)
    z_ref[s] = _dot(xn_ref[...], w_section)

    @pl.when(s >= OFF_V // SECTION)
    def _transposed_section():
        wvgt_ref[...] = win32_ref[...].T.astype(BF16)

    @pl.when(s == n_sections - 1)
    def _dense_front():
        def proj(off):
            return z_ref[off // SECTION]

        zb = proj(OFF_B)
        u = proj(OFF_C) * proj(OFF_H)
        ext = [st_ref[:, 0, :], st_ref[:, 1, :]] + [u[i * B:(i + 1) * B] for i in range(L)]
        wc = wconv_ref[...]
        for i in range(L):
            conv = wc[0] * ext[i] + wc[1] * ext[i + 1] + wc[2] * ext[i + 2]
            cvo_ref[i * B:(i + 1) * B, :] = (zb[i * B:(i + 1) * B] * conv).astype(BF16)
        convst_ref[:, 0, :] = ext[L]
        convst_ref[:, 1, :] = ext[L + 1]
        zg_ref[...] = proj(OFF_G)

        q_t = proj(OFF_Q).T
        k_t = proj(OFF_K).T
        v_t = proj(OFF_V).T
        v_ref[...] = v_t
        inv = jnp.broadcast_to(inv_ref[...], (B, LANES)).T[:HALF_DIM]
        cos_sin = [(jnp.cos(float(PAST_LEN + i) * inv), jnp.sin(float(PAST_LEN + i) * inv))
                   for i in range(L)]

        def rope(t, r0, i):
            cos, sin = cos_sin[i]
            t1 = t[r0:r0 + HALF_DIM, i * B:(i + 1) * B]
            t2 = t[r0 + HALF_DIM:r0 + HEAD_DIM, i * B:(i + 1) * B]
            return jnp.concatenate([t1 * cos - t2 * sin, t2 * cos + t1 * sin], axis=0)

        for hd in range(RET_HEADS):
            r0 = hd * HEAD_DIM
            qr = [rope(q_t, r0, i) for i in range(L)]
            kr = [rope(k_t, r0, i) for i in range(L)]
            for i in range(L):
                lanes = slice(i * B, (i + 1) * B)
                qd_ref[r0:r0 + HEAD_DIM, lanes] = qr[i] * _gamma_pow(hd, i + 1)
                kd_ref[r0:r0 + HEAD_DIM, lanes] = kr[i] * (_gamma_pow(hd, L - 1 - i) * k_scale)
                intra = jnp.zeros((HEAD_DIM, B), F32)
                for j in range(i + 1):
                    score = jnp.sum(qr[i] * kr[j], axis=0, keepdims=True) * (_gamma_pow(hd, i - j) * k_scale)
                    intra = intra + score * v_t[r0:r0 + HEAD_DIM, j * B:(j + 1) * B]
                o_ref[r0:r0 + HEAD_DIM, lanes] = intra


def _sample_front(xs, st, g1, w_in, w_conv, w_out, inv_lane, dec_seq):
    n_seq = xs.shape[0]
    rows = n_seq * dec_seq
    n_sections = w_in.shape[1] // SECTION
    first_t = OFF_V // SECTION
    out_slabs = w_out.shape[0] // CAST_ROWS
    assert n_sections * SECTION == w_in.shape[1] and (n_sections - first_t) * SECTION == 2 * RET_DIM
    assert out_slabs * CAST_ROWS == w_out.shape[0] and out_slabs <= n_sections
    full = lambda a: pl.BlockSpec(a.shape, lambda s: (0,) * a.ndim, pipeline_mode=pl.Buffered(1))
    const = lambda shape: pl.BlockSpec(shape, lambda s: (0, 0))
    out_slab = pl.BlockSpec((CAST_ROWS, w_out.shape[1]), lambda s: (jnp.minimum(s, out_slabs - 1), 0))
    section = pl.BlockSpec((w_in.shape[0], SECTION), lambda s: (0, s))
    feat = (RET_DIM, rows)
    return pl.pallas_call(
        functools.partial(_sample_front_kernel, n_seq=n_seq, dec_seq=dec_seq, n_sections=n_sections),
        grid=(n_sections,),
        in_specs=[full(xs), full(st), full(g1), section, full(w_conv), out_slab, full(inv_lane)],
        out_specs=[pl.BlockSpec(st.shape, lambda s: (0, 0, 0)),
                   const(feat), const(feat), const(feat), const(feat),
                   const((rows, RET_DIM)), const((rows, CONV_DIM)),
                   section, out_slab,
                   pl.BlockSpec((SECTION, w_in.shape[0]), lambda s: (jnp.maximum(s - first_t, 0), 0))],
        out_shape=[jax.ShapeDtypeStruct(st.shape, F32),
                   jax.ShapeDtypeStruct(feat, F32), jax.ShapeDtypeStruct(feat, F32),
                   jax.ShapeDtypeStruct(feat, F32), jax.ShapeDtypeStruct(feat, F32),
                   jax.ShapeDtypeStruct((rows, RET_DIM), F32), jax.ShapeDtypeStruct((rows, CONV_DIM), BF16),
                   jax.ShapeDtypeStruct(w_in.shape, BF16), jax.ShapeDtypeStruct(w_out.shape, BF16),
                   jax.ShapeDtypeStruct((2 * RET_DIM, w_in.shape[0]), BF16)],
        scratch_shapes=[
            pltpu.VMEM((rows, w_in.shape[0]), BF16),
            pltpu.VMEM((n_sections, rows, SECTION), F32),
        ],
        compiler_params=pltpu.CompilerParams(
            dimension_semantics=("arbitrary",), vmem_limit_bytes=VMEM_LIMIT),
        name="sample_front",
    )(xs, st, g1, w_in, w_conv, w_out, inv_lane)


def kernel(x_prompt, x_sample, state_conv, state_ret, meta_tokens, norm1_g, w_in, w_conv, ret_norm_g,
           w_out, norm2_g, w_gate, w_up, w_down, final_norm_g):
    n_p, seq, _ = x_prompt.shape
    n_s, dec_seq, _ = x_sample.shape
    rows_s = n_s * dec_seq
    assert norm1_g.shape[0] == 1 and seq % TILE == 0 and n_p % SEQS_PER_STEP == 0
    assert (n_p * seq) % FFN_TILE == 0
    assert n_s % LANES == 0 and rows_s % FFN_CHUNK == 0

    g1, g2, gf = norm1_g[0][None], norm2_g[0][None], final_norm_g[None]
    lg_lane = jnp.asarray(_LOG_GAMMA[np.arange(RET_DIM) // HEAD_DIM][None])
    inv = ROPE_BASE ** (-jnp.arange(HALF_DIM, dtype=F32) / HALF_DIM)
    inv_lane = jnp.tile(inv, LANES // HALF_DIM)[None]

    gain = ret_norm_g[0][None]

    w_conv_rows = jnp.transpose(w_conv, (1, 0, 2))
    conv_s, qd_t, kd_t, v_t, o_intra, zg, cvo, w_in_b, w_out_b, w_vg_t = _sample_front(
        x_sample, state_conv[0], g1, w_in[0], w_conv_rows, w_out[0], inv_lane, dec_seq)

    x1_p, conv_p, ret_p, wg_b, wu_b, wd_b = _prompt_mixer(
        x_prompt, meta_tokens, g1, w_in_b, w_vg_t, w_conv_rows, gain, w_out_b, lg_lane, inv_lane,
        w_gate[0], w_up[0], w_down[0])

    s4 = jnp.transpose(state_ret[0], (1, 2, 3, 0))
    y_p, y_sample, s4_new = _ffn(x1_p.reshape(n_p * seq, D_MODEL), x_sample, qd_t, kd_t, v_t, o_intra, zg,
                                 cvo, s4, gain, w_out_b, g2, wg_b, wu_b, wd_b, gf, dec_seq)
    y_prompt = y_p.reshape(n_p, seq, D_MODEL)
    ret_s = jnp.transpose(s4_new, (3, 0, 1, 2))

    return (y_prompt, y_sample, conv_p[None], ret_p[None], conv_s[None], ret_s[None])
```

```python
import functools

import numpy as np
import jax
import jax.numpy as jnp
from jax import lax
from jax.experimental import pallas as pl
from jax.experimental.pallas import tpu as pltpu

D_MODEL = 1024
N_META = 16
CONV_DIM = 512
RET_HEADS = 8
HEAD_DIM = 64
HALF_DIM = HEAD_DIM // 2
RET_DIM = RET_HEADS * HEAD_DIM
PAST_LEN = 16384
ROPE_BASE = 10000.0
EPS = 1e-6
GN_EPS = 1e-5

OFF_B, OFF_C, OFF_H, OFF_Q, OFF_K, OFF_V, OFF_G = (i * 512 for i in range(7))

LANES = 128
GROUP = 256
HEADS_PER_GROUP = GROUP // HEAD_DIM
N_GROUPS = RET_DIM // GROUP
TILE = 256
SEQS_PER_STEP = 4
FFN_TILE = 512
FFN_CHUNK = 256
SECTION = 512
CAST_ROWS = 256
PHASE_LAG = 3
FFN_PHASE_LAG = 1
BF16_ROWS = 16
VMEM_LIMIT = 56 * 1024 * 1024

F32 = jnp.float32
BF16 = jnp.bfloat16

_LOG_GAMMA = np.log1p(-(2.0 ** (-5.0 - np.arange(RET_HEADS)))).astype(np.float32)


def _dot(a, b):
    return jnp.dot(a, b, preferred_element_type=F32)


def _dot_nt(a, b):
    return lax.dot_general(a, b, (((1,), (1,)), ((), ())), preferred_element_type=F32)


def _dot_tn(a, b):
    return lax.dot_general(a, b, (((0,), (0,)), ((), ())), preferred_element_type=F32)


def _rmsnorm(x, g):
    ms = jnp.mean(x * x, axis=-1, keepdims=True)
    return x * lax.rsqrt(ms + EPS) * g


def _silu(x):
    return x * jax.nn.sigmoid(x)


def _iota(shape, dim):
    return lax.broadcasted_iota(jnp.int32, shape, dim)


def _first_half_lanes():
    return (_iota((1, LANES), 1) & (HEAD_DIM - 1)) < HALF_DIM


def _rope(t, cos, signed_sin):
    first_half = _first_half_lanes()
    out = []
    for b in range(t.shape[1] // LANES):
        blk = t[:, b * LANES:(b + 1) * LANES]
        partner = jnp.where(first_half, pltpu.roll(blk, LANES - HALF_DIM, axis=1),
                            pltpu.roll(blk, HALF_DIM, axis=1))
        out.append(blk * cos + partner * signed_sin)
    return jnp.concatenate(out, axis=1)


def _ones_block():
    r = _iota((GROUP, GROUP), 0) >> 6
    c = _iota((GROUP, GROUP), 1) >> 6
    return jnp.where(r == c, 1.0, 0.0).astype(BF16)


def _trace_staggered(phase_iters, lag):
    live = list(range(len(phase_iters)))
    tick = 0
    while live:
        for k in list(live):
            if tick >= lag * k:
                try:
                    next(phase_iters[k])
                except StopIteration:
                    live.remove(k)
        tick += 1


def _prompt_mixer_kernel(x_ref, meta_ref, g1_ref, win_ref, wvgt_ref, wconv_ref, gain_ref, wout_ref,
                         lg_ref, inv_ref, wg32_ref, wu32_ref, wd32_ref,
                         x1_ref, convst_ref, retst_ref, wg_ref, wu_ref, wd_ref,
                         cos_ref, sin_ref, qdec_ref, kdec_ref, dmask_ref,
                         smask_ref, sdec_ref, gaincol_ref, tail_ref, state_ref, ubuf_ref, *, n_tiles):
    j = pl.program_id(0)
    p = pl.program_id(1)
    C = TILE
    k_scale = HEAD_DIM ** -0.5

    def proj(xn, off):
        return _dot(xn, win_ref[:, off:off + 512])

    @pl.when((j == 0) & (p == 0))
    def _init():
        i_f = _iota((C, 1), 0).astype(F32)
        cos_ref[...] = jnp.cos(i_f * inv_ref[...])
        sin_ref[...] = jnp.sin(i_f * inv_ref[...])
        lg = lg_ref[...]
        qdec_ref[...] = jnp.exp((i_f + 1.0) * lg)
        kdec_ref[...] = jnp.exp((C - 1.0 - i_f) * lg) * k_scale
        sdec_ref[...] = jnp.exp(C * lg)
        gaincol_ref[...] = jnp.broadcast_to(gain_ref[...], (C, RET_DIM)).T
        diff = (_iota((C, C), 1) - _iota((C, C), 0)).astype(F32)
        for h in range(RET_HEADS):
            dmask_ref[h] = jnp.where(
                diff >= 0, jnp.exp(jnp.maximum(diff, 0.0) * float(_LOG_GAMMA[h])) * k_scale, 0.0)
        smask = _ones_block().astype(F32)
        smask_ref[...] = smask

        xm = _rmsnorm(meta_ref[...], g1_ref[...]).astype(BF16)
        um = proj(xm, OFF_C) * proj(xm, OFF_H)
        vm = _dot_nt(xm, wvgt_ref[:RET_DIM, :]).astype(BF16)
        m_f = _iota((N_META, 1), 0).astype(F32)
        ang = m_f * inv_ref[...]
        sin = jnp.sin(ang)
        kmr = _rope(proj(xm, OFF_K), jnp.cos(ang), jnp.where(_first_half_lanes(), -sin, sin))
        kmd = (kmr * (jnp.exp((N_META - 1.0 - m_f) * lg) * k_scale)).astype(BF16)
        for b in range(tail_ref.shape[0]):
            tail_ref[b] = um[N_META - 2:N_META, :]
            for g in range(N_GROUPS):
                sl = slice(g * GROUP, (g + 1) * GROUP)
                state_ref[b, g] = _dot_tn(vm[:, sl], kmd[:, sl]) * smask

    off = (N_META + j * C).astype(F32) * inv_ref[...]
    cos_off, sin_off = jnp.cos(off), jnp.sin(off)
    cos_row, sin_row = cos_ref[...], sin_ref[...]
    cos = cos_row * cos_off - sin_row * sin_off
    sin = sin_row * cos_off + cos_row * sin_off
    sin = jnp.where(_first_half_lanes(), -sin, sin)
    head_of_lane = _iota((1, GROUP), 1) >> 6
    smask = smask_ref[...]
    wc = wconv_ref[...]

    def mix_sequence(slot):
        n = p * SEQS_PER_STEP + slot
        x = x_ref[slot]
        xn = _rmsnorm(x, g1_ref[...]).astype(BF16)

        zb = proj(xn, OFF_B)
        yield
        u = proj(xn, OFF_C) * proj(xn, OFF_H)
        ubuf_ref[slot, 6:8, :] = tail_ref[n]
        ubuf_ref[slot, 8:8 + C, :] = u
        conv = wc[0] * ubuf_ref[slot, 6:6 + C, :] + wc[1] * ubuf_ref[slot, 7:7 + C, :] + wc[2] * u
        conv_out = (zb * conv).astype(BF16)
        new_tail = u[C - 2:C, :]
        tail_ref[n] = new_tail

        yield

        qr = _rope(proj(xn, OFF_Q), cos, sin)
        yield
        kr = _rope(proj(xn, OFF_K), cos, sin)
        yield
        vg_t = _dot_nt(wvgt_ref[...], xn)
        v_t = vg_t[:RET_DIM].astype(BF16)
        yield
        qb = qr.astype(BF16)
        kb = kr.astype(BF16)
        qd = (qr * qdec_ref[...]).astype(BF16)
        kd = (kr * kdec_ref[...]).astype(BF16)
        o_parts = []
        new_states = []
        for g in range(N_GROUPS):
            sl = slice(g * GROUP, (g + 1) * GROUP)
            state_t = state_ref[n, g]
            cross_t = _dot_nt(state_t.astype(BF16), qd[:, sl])
            k_heads = jnp.concatenate(
                [jnp.where(head_of_lane == hh, kb[:, sl], jnp.zeros_like(kb[:, sl]))
                 for hh in range(HEADS_PER_GROUP)], axis=0)
            decay = dmask_ref[g * HEADS_PER_GROUP:(g + 1) * HEADS_PER_GROUP].reshape(HEADS_PER_GROUP * C, C)
            scores_t = (_dot_nt(k_heads, qb[:, sl]) * decay).astype(BF16)
            for hh in range(HEADS_PER_GROUP):
                h = g * HEADS_PER_GROUP + hh
                rows = slice(hh * HEAD_DIM, (hh + 1) * HEAD_DIM)
                o_parts.append(cross_t[rows] + _dot(v_t[h * HEAD_DIM:(h + 1) * HEAD_DIM],
                                                    scores_t[hh * C:(hh + 1) * C]))
            new_state = state_t * sdec_ref[:, sl] + _dot(v_t[sl], kd[:, sl]) * smask
            state_ref[n, g] = new_state
            new_states.append(new_state)
            yield

        normed = []
        for h, o_h in enumerate(o_parts):
            mu = jnp.sum(o_h, axis=0, keepdims=True) * (1.0 / HEAD_DIM)
            dlt = o_h - mu
            var = jnp.sum(dlt * dlt, axis=0, keepdims=True) * (1.0 / HEAD_DIM)
            normed.append(dlt * lax.rsqrt(var + GN_EPS) * gaincol_ref[h * HEAD_DIM:(h + 1) * HEAD_DIM, :])
        ret_out_t = (_silu(vg_t[RET_DIM:]) * jnp.concatenate(normed, axis=0)).astype(BF16)
        yield

        x1_ref[slot] = (x + _dot(conv_out, wout_ref[:CONV_DIM, :])
                        + _dot_tn(ret_out_t, wout_ref[CONV_DIM:, :]))
        finals[slot] = (new_tail, new_states)

    finals = [None] * SEQS_PER_STEP
    _trace_staggered([mix_sequence(slot) for slot in range(SEQS_PER_STEP)], PHASE_LAG)

    wg_ref[...] = wg32_ref[...].astype(BF16)
    wu_ref[...] = wu32_ref[...].astype(BF16)
    wd_ref[...] = wd32_ref[...].astype(BF16)

    @pl.when(j == n_tiles - 1)
    def _final_states():
        for slot, (new_tail, new_states) in enumerate(finals):
            n = p * SEQS_PER_STEP + slot
            convst_ref[n] = new_tail
            for g in range(N_GROUPS):
                state = new_states[g].T
                for hh in range(HEADS_PER_GROUP):
                    blk = slice(hh * HEAD_DIM, (hh + 1) * HEAD_DIM)
                    retst_ref[n, g * HEADS_PER_GROUP + hh] = state[blk, blk]


def _prompt_mixer(x_prompt, meta, g1, w_in, w_vg_t, w_conv, gain, w_out, lg_lane, inv_lane,
                  w_gate, w_up, w_down):
    n_seq, seq, _ = x_prompt.shape
    n_tiles = seq // TILE
    pairs = n_seq // SEQS_PER_STEP
    n_steps = n_tiles * pairs
    full = lambda a: pl.BlockSpec(a.shape, lambda j, p: (0,) * a.ndim, pipeline_mode=pl.Buffered(1))
    gu_rows = w_gate.shape[0] // n_steps
    d_rows = w_down.shape[0] // (n_steps // 2)
    assert gu_rows * n_steps == w_gate.shape[0] and d_rows * (n_steps // 2) == w_down.shape[0]
    assert gu_rows % BF16_ROWS == 0 and d_rows % BF16_ROWS == 0
    gu_slab = pl.BlockSpec((gu_rows, w_gate.shape[1]), lambda j, p: (j * pairs + p, 0))
    d_slab = pl.BlockSpec((d_rows, w_down.shape[1]), lambda j, p: ((j * pairs + p) // 2, 0))
    return pl.pallas_call(
        functools.partial(_prompt_mixer_kernel, n_tiles=n_tiles),
        grid=(n_tiles, pairs),
        in_specs=[pl.BlockSpec((SEQS_PER_STEP, TILE, D_MODEL), lambda j, p: (p, j, 0)),
                  full(meta), full(g1), full(w_in), full(w_vg_t), full(w_conv), full(gain),
                  full(w_out), full(lg_lane), full(inv_lane), gu_slab, gu_slab, d_slab],
        out_specs=[pl.BlockSpec((SEQS_PER_STEP, TILE, D_MODEL), lambda j, p: (p, j, 0)),
                   pl.BlockSpec((n_seq, 2, CONV_DIM), lambda j, p: (0, 0, 0)),
                   pl.BlockSpec((n_seq, RET_HEADS, HEAD_DIM, HEAD_DIM), lambda j, p: (0, 0, 0, 0)),
                   gu_slab, gu_slab, d_slab],
        out_shape=[jax.ShapeDtypeStruct((n_seq, seq, D_MODEL), F32),
                   jax.ShapeDtypeStruct((n_seq, 2, CONV_DIM), F32),
                   jax.ShapeDtypeStruct((n_seq, RET_HEADS, HEAD_DIM, HEAD_DIM), F32),
                   jax.ShapeDtypeStruct(w_gate.shape, BF16), jax.ShapeDtypeStruct(w_up.shape, BF16),
                   jax.ShapeDtypeStruct(w_down.shape, BF16)],
        scratch_shapes=[
            pltpu.VMEM((TILE, LANES), F32),
            pltpu.VMEM((TILE, LANES), F32),
            pltpu.VMEM((TILE, RET_DIM), F32),
            pltpu.VMEM((TILE, RET_DIM), F32),
            pltpu.VMEM((RET_HEADS, TILE, TILE), F32),
            pltpu.VMEM((GROUP, GROUP), F32),
            pltpu.VMEM((1, RET_DIM), F32),
            pltpu.VMEM((RET_DIM, TILE), F32),
            pltpu.VMEM((n_seq, 2, CONV_DIM), F32),
            pltpu.VMEM((n_seq, N_GROUPS, GROUP, GROUP), F32),
            pltpu.VMEM((SEQS_PER_STEP, TILE + 8, CONV_DIM), F32),
        ],
        compiler_params=pltpu.CompilerParams(
            dimension_semantics=("arbitrary", "arbitrary"), vmem_limit_bytes=VMEM_LIMIT),
        name="prompt_mixer",
    )(x_prompt, meta, g1, w_in, w_vg_t, w_conv, gain, w_out, lg_lane, inv_lane, w_gate, w_up, w_down)


def _ffn_kernel(xp_ref, xs_ref, qd_ref, kd_ref, v_ref, ointra_ref, zg_ref, cvo_ref, s_ref, gain_ref,
                sdec_ref, wout_ref, g2_ref, wg_ref, wu_ref, wd_ref, gf_ref,
                yp_ref, ys_ref, snew_ref, o_ref, x2_ref, *, prompt_steps, n_seq, dec_seq, e_piece):
    i = pl.program_id(0)
    L, B = dec_seq, n_seq
    pieces_per_head = HEAD_DIM // e_piece

    def chunk_phases(load, finish, rows):
        x = load(rows)
        xn = _rmsnorm(x, g2_ref[...]).astype(BF16)
        yield
        gate = _dot(xn, wg_ref[...])
        yield
        up = _dot(xn, wu_ref[...])
        yield
        hidden = (_silu(gate) * up).astype(BF16)
        yield
        finish(rows, x + _dot(hidden, wd_ref[...]))

    def ffn_chunks(load, finish, n_rows, before=(), after=()):
        chunks = [slice(r, r + FFN_CHUNK) for r in range(0, n_rows, FFN_CHUNK)]
        _trace_staggered(list(before) + [chunk_phases(load, finish, rows) for rows in chunks] + list(after),
                         FFN_PHASE_LAG)

    def previous_tile_norm():
        yp_ref[...] = _rmsnorm(x2_ref[...], gf_ref[...])
        yield

    def stash_prenorm(rows, x2):
        x2_ref[rows, :] = x2

    def write_sample(rows, x2):
        y = _rmsnorm(x2, gf_ref[...])
        for k in range(FFN_CHUNK // B):
            ys_ref[:, rows.start // B + k, :] = y[k * B:(k + 1) * B]

    def state_piece():
        h = i // pieces_per_head
        base = pl.multiple_of(h * HEAD_DIM, HEAD_DIM)
        rows = pl.ds(pl.multiple_of(base + (i % pieces_per_head) * e_piece, e_piece), e_piece)
        qd_h = qd_ref[pl.ds(base, HEAD_DIM), :]
        kd_h = kd_ref[pl.ds(base, HEAD_DIM), :]
        state_decay = sdec_ref[h]
        v_blk = [v_ref[rows, j * B:(j + 1) * B] for j in range(L)]
        cross = [jnp.zeros((e_piece, B), F32) for _ in range(L)]
        for d in range(HEAD_DIM):
            s_de = s_ref[0, d]
            new = s_de * state_decay
            for t in range(L):
                lanes = slice(t * B, (t + 1) * B)
                cross[t] = cross[t] + qd_h[d:d + 1, lanes] * s_de
                new = new + kd_h[d:d + 1, lanes] * v_blk[t]
            snew_ref[0, d] = new
        for t in range(L):
            o_ref[rows, t * B:(t + 1) * B] = o_ref[rows, t * B:(t + 1) * B] + cross[t]
        yield

    @pl.when(i == 0)
    def _seed():
        o_ref[...] = ointra_ref[...]
        x2_ref[...] = jnp.zeros_like(x2_ref)

    @pl.when(i < prompt_steps)
    def _prompt_rows():
        ffn_chunks(lambda rows: xp_ref[rows, :], stash_prenorm, FFN_TILE,
                   before=[previous_tile_norm()], after=[state_piece()])

    @pl.when(i == prompt_steps)
    def _sample_rows():
        gain_col = jnp.broadcast_to(gain_ref[...], (L * B, RET_DIM)).T
        for hd in range(RET_HEADS):
            r0 = hd * HEAD_DIM
            o_h = o_ref[r0:r0 + HEAD_DIM, :]
            mu = jnp.sum(o_h, axis=0, keepdims=True) * (1.0 / HEAD_DIM)
            dlt = o_h - mu
            var = jnp.sum(dlt * dlt, axis=0, keepdims=True) * (1.0 / HEAD_DIM)
            o_ref[r0:r0 + HEAD_DIM, :] = dlt * lax.rsqrt(var + GN_EPS) * gain_col[r0:r0 + HEAD_DIM, :]
        ret_out = (_silu(zg_ref[...]) * o_ref[...].T).astype(BF16)
        xs = jnp.concatenate([xs_ref[:, t, :] for t in range(L)], axis=0)
        x1 = (xs + _dot(cvo_ref[...], wout_ref[:CONV_DIM, :])
              + _dot(ret_out, wout_ref[CONV_DIM:, :]))
        ffn_chunks(lambda rows: x1[rows], write_sample, L * B, before=[previous_tile_norm()])


def _ffn(x_p, xs, qd_t, kd_t, v_t, o_intra, zg, cvo, s4, gain, w_out_b, g2, w_gate, w_up, w_down, gf,
         dec_seq):
    prompt_steps = x_p.shape[0] // FFN_TILE
    n_seq = xs.shape[0]
    assert prompt_steps % RET_HEADS == 0 and FFN_CHUNK % n_seq == 0
    e_piece = HEAD_DIM // (prompt_steps // RET_HEADS)
    assert e_piece % 8 == 0 and HEAD_DIM % e_piece == 0
    pieces_per_head = HEAD_DIM // e_piece
    sdec = jnp.asarray(np.exp(np.float32(dec_seq) * _LOG_GAMMA))
    full = lambda a: pl.BlockSpec(a.shape, lambda i: (0,) * a.ndim, pipeline_mode=pl.Buffered(1))
    prompt_tile = pl.BlockSpec((FFN_TILE, D_MODEL), lambda i: (jnp.minimum(i, prompt_steps - 1), 0))
    prompt_out = pl.BlockSpec((FFN_TILE, D_MODEL), lambda i: (jnp.maximum(i - 1, 0), 0))
    sample_out = pl.BlockSpec(xs.shape, lambda i: (0, 0, 0))

    def piece_index(i):
        piece = jnp.minimum(i, prompt_steps - 1)
        return (piece // pieces_per_head, 0, piece % pieces_per_head, 0)

    state_piece = pl.BlockSpec((1, HEAD_DIM, e_piece, n_seq), piece_index)
    return pl.pallas_call(
        functools.partial(_ffn_kernel, prompt_steps=prompt_steps, n_seq=n_seq, dec_seq=dec_seq,
                          e_piece=e_piece),
        grid=(prompt_steps + 1,),
        in_specs=[prompt_tile, full(xs), full(qd_t), full(kd_t), full(v_t), full(o_intra), full(zg),
                  full(cvo), state_piece, full(gain), pl.BlockSpec(memory_space=pltpu.SMEM), full(w_out_b),
                  full(g2),
                  full(w_gate), full(w_up), full(w_down), full(gf)],
        out_specs=[prompt_out, sample_out, state_piece],
        out_shape=[jax.ShapeDtypeStruct(x_p.shape, F32), jax.ShapeDtypeStruct(xs.shape, F32),
                   jax.ShapeDtypeStruct(s4.shape, F32)],
        scratch_shapes=[pltpu.VMEM(o_intra.shape, F32),
                        pltpu.VMEM((FFN_TILE, D_MODEL), F32)],
        compiler_params=pltpu.CompilerParams(
            dimension_semantics=("arbitrary",), vmem_limit_bytes=VMEM_LIMIT),
        name="ffn",
    )(x_p, xs, qd_t, kd_t, v_t, o_intra, zg, cvo, s4, gain, sdec, w_out_b, g2, w_gate, w_up, w_down, gf)


def _gamma_pow(head, power):
    return float(np.exp(np.float32(power) * _LOG_GAMMA[head]))


def _sample_front_kernel(x_ref, st_ref, g1_ref, win32_ref, wconv_ref, wout32_ref, inv_ref,
                         convst_ref, qd_ref, kd_ref, v_ref, o_ref, zg_ref, cvo_ref,
                         win_ref, wout_ref, wvgt_ref, xn_ref, z_ref, *, n_seq, dec_seq, n_sections):
    s = pl.program_id(0)
    L, B = dec_seq, n_seq
    k_scale = HEAD_DIM ** -0.5

    @pl.when(s == 0)
    def _normalize():
        x = jnp.concatenate([x_ref[:, t, :] for t in range(L)], axis=0)
        xn_ref[...] = _rmsnorm(x, g1_ref[...]).astype(BF16)

    w_section = win32_ref[...].astype(BF16)

    @pl.when(s < OFF_V // SECTION)
    def _plain_section():
        win_ref[...] = w_section

    wout_ref[...] = wout32_ref[...].astype(BF16)
    z_ref[s] = _dot(xn_ref[...], w_section)

    @pl.when(s >= OFF_V // SECTION)
    def _transposed_section():
        wvgt_ref[...] = win32_ref[...].T.astype(BF16)

    @pl.when(s == n_sections - 1)
    def _dense_front():
        def proj(off):
            return z_ref[off // SECTION]

        zb = proj(OFF_B)
        u = proj(OFF_C) * proj(OFF_H)
        ext = [st_ref[:, 0, :], st_ref[:, 1, :]] + [u[i * B:(i + 1) * B] for i in range(L)]
        wc = wconv_ref[...]
        for i in range(L):
            conv = wc[0] * ext[i] + wc[1] * ext[i + 1] + wc[2] * ext[i + 2]
            cvo_ref[i * B:(i + 1) * B, :] = (zb[i * B:(i + 1) * B] * conv).astype(BF16)
        convst_ref[:, 0, :] = ext[L]
        convst_ref[:, 1, :] = ext[L + 1]
        zg_ref[...] = proj(OFF_G)

        q_t = proj(OFF_Q).T
        k_t = proj(OFF_K).T
        v_t = proj(OFF_V).T
        v_ref[...] = v_t
        inv = jnp.broadcast_to(inv_ref[...], (B, LANES)).T[:HALF_DIM]
        cos_sin = [(jnp.cos(float(PAST_LEN + i) * inv), jnp.sin(float(PAST_LEN + i) * inv))
                   for i in range(L)]

        def rope(t, r0, i):
            cos, sin = cos_sin[i]
            t1 = t[r0:r0 + HALF_DIM, i * B:(i + 1) * B]
            t2 = t[r0 + HALF_DIM:r0 + HEAD_DIM, i * B:(i + 1) * B]
            return jnp.concatenate([t1 * cos - t2 * sin, t2 * cos + t1 * sin], axis=0)

        for hd in range(RET_HEADS):
            r0 = hd * HEAD_DIM
            qr = [rope(q_t, r0, i) for i in range(L)]
            kr = [rope(k_t, r0, i) for i in range(L)]
            for i in range(L):
                lanes = slice(i * B, (i + 1) * B)
                qd_ref[r0:r0 + HEAD_DIM, lanes] = qr[i] * _gamma_pow(hd, i + 1)
                kd_ref[r0:r0 + HEAD_DIM, lanes] = kr[i] * (_gamma_pow(hd, L - 1 - i) * k_scale)
                intra = jnp.zeros((HEAD_DIM, B), F32)
                for j in range(i + 1):
                    score = jnp.sum(qr[i] * kr[j], axis=0, keepdims=True) * (_gamma_pow(hd, i - j) * k_scale)
                    intra = intra + score * v_t[r0:r0 + HEAD_DIM, j * B:(j + 1) * B]
                o_ref[r0:r0 + HEAD_DIM, lanes] = intra


def _sample_front(xs, st, g1, w_in, w_conv, w_out, inv_lane, dec_seq):
    n_seq = xs.shape[0]
    rows = n_seq * dec_seq
    n_sections = w_in.shape[1] // SECTION
    first_t = OFF_V // SECTION
    out_slabs = w_out.shape[0] // CAST_ROWS
    assert n_sections * SECTION == w_in.shape[1] and (n_sections - first_t) * SECTION == 2 * RET_DIM
    assert out_slabs * CAST_ROWS == w_out.shape[0] and out_slabs <= n_sections
    full = lambda a: pl.BlockSpec(a.shape, lambda s: (0,) * a.ndim, pipeline_mode=pl.Buffered(1))
    const = lambda shape: pl.BlockSpec(shape, lambda s: (0, 0))
    out_slab = pl.BlockSpec((CAST_ROWS, w_out.shape[1]), lambda s: (jnp.minimum(s, out_slabs - 1), 0))
    section = pl.BlockSpec((w_in.shape[0], SECTION), lambda s: (0, s))
    feat = (RET_DIM, rows)
    return pl.pallas_call(
        functools.partial(_sample_front_kernel, n_seq=n_seq, dec_seq=dec_seq, n_sections=n_sections),
        grid=(n_sections,),
        in_specs=[full(xs), full(st), full(g1), section, full(w_conv), out_slab, full(inv_lane)],
        out_specs=[pl.BlockSpec(st.shape, lambda s: (0, 0, 0)),
                   const(feat), const(feat), const(feat), const(feat),
                   const((rows, RET_DIM)), const((rows, CONV_DIM)),
                   pl.BlockSpec((w_in.shape[0], SECTION), lambda s: (0, jnp.minimum(s, first_t - 1))), out_slab,
                   pl.BlockSpec((SECTION, w_in.shape[0]), lambda s: (jnp.maximum(s - first_t, 0), 0))],
        out_shape=[jax.ShapeDtypeStruct(st.shape, F32),
                   jax.ShapeDtypeStruct(feat, F32), jax.ShapeDtypeStruct(feat, F32),
                   jax.ShapeDtypeStruct(feat, F32), jax.ShapeDtypeStruct(feat, F32),
                   jax.ShapeDtypeStruct((rows, RET_DIM), F32), jax.ShapeDtypeStruct((rows, CONV_DIM), BF16),
                   jax.ShapeDtypeStruct((w_in.shape[0], OFF_V), BF16), jax.ShapeDtypeStruct(w_out.shape, BF16),
                   jax.ShapeDtypeStruct((2 * RET_DIM, w_in.shape[0]), BF16)],
        scratch_shapes=[
            pltpu.VMEM((rows, w_in.shape[0]), BF16),
            pltpu.VMEM((n_sections, rows, SECTION), F32),
        ],
        compiler_params=pltpu.CompilerParams(
            dimension_semantics=("arbitrary",), vmem_limit_bytes=VMEM_LIMIT),
        name="sample_front",
    )(xs, st, g1, w_in, w_conv, w_out, inv_lane)


def kernel(x_prompt, x_sample, state_conv, state_ret, meta_tokens, norm1_g, w_in, w_conv, ret_norm_g,
           w_out, norm2_g, w_gate, w_up, w_down, final_norm_g):
    n_p, seq, _ = x_prompt.shape
    n_s, dec_seq, _ = x_sample.shape
    rows_s = n_s * dec_seq
    assert norm1_g.shape[0] == 1 and seq % TILE == 0 and n_p % SEQS_PER_STEP == 0
    assert (n_p * seq) % FFN_TILE == 0
    assert n_s % LANES == 0 and rows_s % FFN_CHUNK == 0

    g1, g2, gf = norm1_g[0][None], norm2_g[0][None], final_norm_g[None]
    lg_lane = jnp.asarray(_LOG_GAMMA[np.arange(RET_DIM) // HEAD_DIM][None])
    inv = ROPE_BASE ** (-jnp.arange(HALF_DIM, dtype=F32) / HALF_DIM)
    inv_lane = jnp.tile(inv, LANES // HALF_DIM)[None]

    gain = ret_norm_g[0][None]

    w_conv_rows = jnp.transpose(w_conv, (1, 0, 2))
    conv_s, qd_t, kd_t, v_t, o_intra, zg, cvo, w_in_b, w_out_b, w_vg_t = _sample_front(
        x_sample, state_conv[0], g1, w_in[0], w_conv_rows, w_out[0], inv_lane, dec_seq)

    x1_p, conv_p, ret_p, wg_b, wu_b, wd_b = _prompt_mixer(
        x_prompt, meta_tokens, g1, w_in_b, w_vg_t, w_conv_rows, gain, w_out_b, lg_lane, inv_lane,
        w_gate[0], w_up[0], w_down[0])

    s4 = jnp.transpose(state_ret[0], (1, 2, 3, 0))
    y_p, y_sample, s4_new = _ffn(x1_p.reshape(n_p * seq, D_MODEL), x_sample, qd_t, kd_t, v_t, o_intra, zg,
                                 cvo, s4, gain, w_out_b, g2, wg_b, wu_b, wd_b, gf, dec_seq)
    y_prompt = y_p.reshape(n_p, seq, D_MODEL)
    ret_s = jnp.transpose(s4_new, (3, 0, 1, 2))

    return (y_prompt, y_sample, conv_p[None], ret_p[None], conv_s[None], ret_s[None])
```

```python
import functools

import numpy as np
import jax
import jax.numpy as jnp
from jax import lax
from jax.experimental import pallas as pl
from jax.experimental.pallas import tpu as pltpu

D_MODEL = 1024
N_META = 16
CONV_DIM = 512
RET_HEADS = 8
HEAD_DIM = 64
HALF_DIM = HEAD_DIM // 2
RET_DIM = RET_HEADS * HEAD_DIM
PAST_LEN = 16384
ROPE_BASE = 10000.0
EPS = 1e-6
GN_EPS = 1e-5

OFF_B, OFF_C, OFF_H, OFF_Q, OFF_K, OFF_V, OFF_G = (i * 512 for i in range(7))

LANES = 128
GROUP = 256
HEADS_PER_GROUP = GROUP // HEAD_DIM
N_GROUPS = RET_DIM // GROUP
TILE = 256
SEQS_PER_STEP = 4
FFN_TILE = 512
FFN_CHUNK = 256
SECTION = 1792
CAST_ROWS = 512
PHASE_LAG = 3
FFN_PHASE_LAG = 1
BF16_ROWS = 16
VMEM_LIMIT = 56 * 1024 * 1024

F32 = jnp.float32
BF16 = jnp.bfloat16

_LOG_GAMMA = np.log1p(-(2.0 ** (-5.0 - np.arange(RET_HEADS)))).astype(np.float32)


def _dot(a, b):
    return jnp.dot(a, b, preferred_element_type=F32)


def _dot_nt(a, b):
    return lax.dot_general(a, b, (((1,), (1,)), ((), ())), preferred_element_type=F32)


def _dot_tn(a, b):
    return lax.dot_general(a, b, (((0,), (0,)), ((), ())), preferred_element_type=F32)


def _rmsnorm(x, g):
    ms = jnp.mean(x * x, axis=-1, keepdims=True)
    return x * lax.rsqrt(ms + EPS) * g


def _silu(x):
    return x * jax.nn.sigmoid(x)


def _iota(shape, dim):
    return lax.broadcasted_iota(jnp.int32, shape, dim)


def _first_half_lanes():
    return (_iota((1, LANES), 1) & (HEAD_DIM - 1)) < HALF_DIM


def _rope(t, cos, signed_sin):
    first_half = _first_half_lanes()
    out = []
    for b in range(t.shape[1] // LANES):
        blk = t[:, b * LANES:(b + 1) * LANES]
        partner = jnp.where(first_half, pltpu.roll(blk, LANES - HALF_DIM, axis=1),
                            pltpu.roll(blk, HALF_DIM, axis=1))
        out.append(blk * cos + partner * signed_sin)
    return jnp.concatenate(out, axis=1)


def _ones_block():
    r = _iota((GROUP, GROUP), 0) >> 6
    c = _iota((GROUP, GROUP), 1) >> 6
    return jnp.where(r == c, 1.0, 0.0).astype(BF16)


def _trace_staggered(phase_iters, lag):
    live = list(range(len(phase_iters)))
    tick = 0
    while live:
        for k in list(live):
            if tick >= lag * k:
                try:
                    next(phase_iters[k])
                except StopIteration:
                    live.remove(k)
        tick += 1


def _prompt_mixer_kernel(x_ref, meta_ref, g1_ref, win_ref, wvgt_ref, wconv_ref, gain_ref, wout_ref,
                         lg_ref, inv_ref, wg32_ref, wu32_ref, wd32_ref,
                         x1_ref, convst_ref, retst_ref, wg_ref, wu_ref, wd_ref,
                         cos_ref, sin_ref, qdec_ref, kdec_ref, dmask_ref,
                         smask_ref, sdec_ref, gaincol_ref, tail_ref, state_ref, ubuf_ref, *, n_tiles):
    j = pl.program_id(0)
    p = pl.program_id(1)
    C = TILE
    k_scale = HEAD_DIM ** -0.5

    def proj(xn, off):
        return _dot(xn, win_ref[:, off:off + 512])

    @pl.when((j == 0) & (p == 0))
    def _init():
        i_f = _iota((C, 1), 0).astype(F32)
        cos_ref[...] = jnp.cos(i_f * inv_ref[...])
        sin_ref[...] = jnp.sin(i_f * inv_ref[...])
        lg = lg_ref[...]
        qdec_ref[...] = jnp.exp((i_f + 1.0) * lg)
        kdec_ref[...] = jnp.exp((C - 1.0 - i_f) * lg) * k_scale
        sdec_ref[...] = jnp.exp(C * lg)
        gaincol_ref[...] = jnp.broadcast_to(gain_ref[...], (C, RET_DIM)).T
        diff = (_iota((C, C), 1) - _iota((C, C), 0)).astype(F32)
        for h in range(RET_HEADS):
            dmask_ref[h] = jnp.where(
                diff >= 0, jnp.exp(jnp.maximum(diff, 0.0) * float(_LOG_GAMMA[h])) * k_scale, 0.0)
        smask = _ones_block().astype(F32)
        smask_ref[...] = smask

        xm = _rmsnorm(meta_ref[...], g1_ref[...]).astype(BF16)
        um = proj(xm, OFF_C) * proj(xm, OFF_H)
        vm = _dot_nt(xm, wvgt_ref[:RET_DIM, :]).astype(BF16)
        m_f = _iota((N_META, 1), 0).astype(F32)
        ang = m_f * inv_ref[...]
        sin = jnp.sin(ang)
        kmr = _rope(proj(xm, OFF_K), jnp.cos(ang), jnp.where(_first_half_lanes(), -sin, sin))
        kmd = (kmr * (jnp.exp((N_META - 1.0 - m_f) * lg) * k_scale)).astype(BF16)
        for b in range(tail_ref.shape[0]):
            tail_ref[b] = um[N_META - 2:N_META, :]
            for g in range(N_GROUPS):
                sl = slice(g * GROUP, (g + 1) * GROUP)
                state_ref[b, g] = _dot_tn(vm[:, sl], kmd[:, sl]) * smask

    off = (N_META + j * C).astype(F32) * inv_ref[...]
    cos_off, sin_off = jnp.cos(off), jnp.sin(off)
    cos_row, sin_row = cos_ref[...], sin_ref[...]
    cos = cos_row * cos_off - sin_row * sin_off
    sin = sin_row * cos_off + cos_row * sin_off
    sin = jnp.where(_first_half_lanes(), -sin, sin)
    head_of_lane = _iota((1, GROUP), 1) >> 6
    smask = smask_ref[...]
    wc = wconv_ref[...]

    def mix_sequence(slot):
        n = p * SEQS_PER_STEP + slot
        x = x_ref[slot]
        xn = _rmsnorm(x, g1_ref[...]).astype(BF16)

        zb = proj(xn, OFF_B)
        yield
        u = proj(xn, OFF_C) * proj(xn, OFF_H)
        ubuf_ref[slot, 6:8, :] = tail_ref[n]
        ubuf_ref[slot, 8:8 + C, :] = u
        conv = wc[0] * ubuf_ref[slot, 6:6 + C, :] + wc[1] * ubuf_ref[slot, 7:7 + C, :] + wc[2] * u
        conv_out = (zb * conv).astype(BF16)
        new_tail = u[C - 2:C, :]
        tail_ref[n] = new_tail

        yield

        qr = _rope(proj(xn, OFF_Q), cos, sin)
        yield
        kr = _rope(proj(xn, OFF_K), cos, sin)
        yield
        vg_t = _dot_nt(wvgt_ref[...], xn)
        v_t = vg_t[:RET_DIM].astype(BF16)
        yield
        qb = qr.astype(BF16)
        kb = kr.astype(BF16)
        qd = (qr * qdec_ref[...]).astype(BF16)
        kd = (kr * kdec_ref[...]).astype(BF16)
        o_parts = []
        new_states = []
        for g in range(N_GROUPS):
            sl = slice(g * GROUP, (g + 1) * GROUP)
            state_t = state_ref[n, g]
            cross_t = _dot_nt(state_t.astype(BF16), qd[:, sl])
            k_heads = jnp.concatenate(
                [jnp.where(head_of_lane == hh, kb[:, sl], jnp.zeros_like(kb[:, sl]))
                 for hh in range(HEADS_PER_GROUP)], axis=0)
            decay = dmask_ref[g * HEADS_PER_GROUP:(g + 1) * HEADS_PER_GROUP].reshape(HEADS_PER_GROUP * C, C)
            scores_t = (_dot_nt(k_heads, qb[:, sl]) * decay).astype(BF16)
            for hh in range(HEADS_PER_GROUP):
                h = g * HEADS_PER_GROUP + hh
                rows = slice(hh * HEAD_DIM, (hh + 1) * HEAD_DIM)
                o_parts.append(cross_t[rows] + _dot(v_t[h * HEAD_DIM:(h + 1) * HEAD_DIM],
                                                    scores_t[hh * C:(hh + 1) * C]))
            new_state = state_t * sdec_ref[:, sl] + _dot(v_t[sl], kd[:, sl]) * smask
            state_ref[n, g] = new_state
            new_states.append(new_state)
            yield

        normed = []
        for h, o_h in enumerate(o_parts):
            mu = jnp.sum(o_h, axis=0, keepdims=True) * (1.0 / HEAD_DIM)
            dlt = o_h - mu
            var = jnp.sum(dlt * dlt, axis=0, keepdims=True) * (1.0 / HEAD_DIM)
            normed.append(dlt * lax.rsqrt(var + GN_EPS) * gaincol_ref[h * HEAD_DIM:(h + 1) * HEAD_DIM, :])
        ret_out_t = (_silu(vg_t[RET_DIM:]) * jnp.concatenate(normed, axis=0)).astype(BF16)
        yield

        x1_ref[slot] = (x + _dot(conv_out, wout_ref[:CONV_DIM, :])
                        + _dot_tn(ret_out_t, wout_ref[CONV_DIM:, :]))
        finals[slot] = (new_tail, new_states)

    finals = [None] * SEQS_PER_STEP
    _trace_staggered([mix_sequence(slot) for slot in range(SEQS_PER_STEP)], PHASE_LAG)

    wg_ref[...] = wg32_ref[...].astype(BF16)
    wu_ref[...] = wu32_ref[...].astype(BF16)
    wd_ref[...] = wd32_ref[...].astype(BF16)

    @pl.when(j == n_tiles - 1)
    def _final_states():
        for slot, (new_tail, new_states) in enumerate(finals):
            n = p * SEQS_PER_STEP + slot
            convst_ref[n] = new_tail
            for g in range(N_GROUPS):
                state = new_states[g].T
                for hh in range(HEADS_PER_GROUP):
                    blk = slice(hh * HEAD_DIM, (hh + 1) * HEAD_DIM)
                    retst_ref[n, g * HEADS_PER_GROUP + hh] = state[blk, blk]


def _prompt_mixer(x_prompt, meta, g1, w_in, w_vg_t, w_conv, gain, w_out, lg_lane, inv_lane,
                  w_gate, w_up, w_down):
    n_seq, seq, _ = x_prompt.shape
    n_tiles = seq // TILE
    pairs = n_seq // SEQS_PER_STEP
    n_steps = n_tiles * pairs
    full = lambda a: pl.BlockSpec(a.shape, lambda j, p: (0,) * a.ndim, pipeline_mode=pl.Buffered(1))
    gu_rows = w_gate.shape[0] // n_steps
    d_rows = w_down.shape[0] // (n_steps // 2)
    assert gu_rows * n_steps == w_gate.shape[0] and d_rows * (n_steps // 2) == w_down.shape[0]
    assert gu_rows % BF16_ROWS == 0 and d_rows % BF16_ROWS == 0
    gu_slab = pl.BlockSpec((gu_rows, w_gate.shape[1]), lambda j, p: (j * pairs + p, 0))
    d_slab = pl.BlockSpec((d_rows, w_down.shape[1]), lambda j, p: ((j * pairs + p) // 2, 0))
    return pl.pallas_call(
        functools.partial(_prompt_mixer_kernel, n_tiles=n_tiles),
        grid=(n_tiles, pairs),
        in_specs=[pl.BlockSpec((SEQS_PER_STEP, TILE, D_MODEL), lambda j, p: (p, j, 0)),
                  full(meta), full(g1), full(w_in), full(w_vg_t), full(w_conv), full(gain),
                  full(w_out), full(lg_lane), full(inv_lane), gu_slab, gu_slab, d_slab],
        out_specs=[pl.BlockSpec((SEQS_PER_STEP, TILE, D_MODEL), lambda j, p: (p, j, 0)),
                   pl.BlockSpec((n_seq, 2, CONV_DIM), lambda j, p: (0, 0, 0)),
                   pl.BlockSpec((n_seq, RET_HEADS, HEAD_DIM, HEAD_DIM), lambda j, p: (0, 0, 0, 0)),
                   gu_slab, gu_slab, d_slab],
        out_shape=[jax.ShapeDtypeStruct((n_seq, seq, D_MODEL), F32),
                   jax.ShapeDtypeStruct((n_seq, 2, CONV_DIM), F32),
                   jax.ShapeDtypeStruct((n_seq, RET_HEADS, HEAD_DIM, HEAD_DIM), F32),
                   jax.ShapeDtypeStruct(w_gate.shape, BF16), jax.ShapeDtypeStruct(w_up.shape, BF16),
                   jax.ShapeDtypeStruct(w_down.shape, BF16)],
        scratch_shapes=[
            pltpu.VMEM((TILE, LANES), F32),
            pltpu.VMEM((TILE, LANES), F32),
            pltpu.VMEM((TILE, RET_DIM), F32),
            pltpu.VMEM((TILE, RET_DIM), F32),
            pltpu.VMEM((RET_HEADS, TILE, TILE), F32),
            pltpu.VMEM((GROUP, GROUP), F32),
            pltpu.VMEM((1, RET_DIM), F32),
            pltpu.VMEM((RET_DIM, TILE), F32),
            pltpu.VMEM((n_seq, 2, CONV_DIM), F32),
            pltpu.VMEM((n_seq, N_GROUPS, GROUP, GROUP), F32),
            pltpu.VMEM((SEQS_PER_STEP, TILE + 8, CONV_DIM), F32),
        ],
        compiler_params=pltpu.CompilerParams(
            dimension_semantics=("arbitrary", "arbitrary"), vmem_limit_bytes=VMEM_LIMIT),
        name="prompt_mixer",
    )(x_prompt, meta, g1, w_in, w_vg_t, w_conv, gain, w_out, lg_lane, inv_lane, w_gate, w_up, w_down)


def _ffn_kernel(xp_ref, xs_ref, qd_ref, kd_ref, v_ref, ointra_ref, zg_ref, cvo_ref, s_ref, gain_ref,
                sdec_ref, wout_ref, g2_ref, wg_ref, wu_ref, wd_ref, gf_ref,
                yp_ref, ys_ref, snew_ref, o_ref, x2_ref, *, prompt_steps, n_seq, dec_seq, e_piece):
    i = pl.program_id(0)
    L, B = dec_seq, n_seq
    pieces_per_head = HEAD_DIM // e_piece

    def chunk_phases(load, finish, rows):
        x = load(rows)
        xn = _rmsnorm(x, g2_ref[...]).astype(BF16)
        yield
        gate = _dot(xn, wg_ref[...])
        yield
        up = _dot(xn, wu_ref[...])
        yield
        hidden = (_silu(gate) * up).astype(BF16)
        yield
        finish(rows, x + _dot(hidden, wd_ref[...]))

    def ffn_chunks(load, finish, n_rows, before=(), after=()):
        chunks = [slice(r, r + FFN_CHUNK) for r in range(0, n_rows, FFN_CHUNK)]
        _trace_staggered(list(before) + [chunk_phases(load, finish, rows) for rows in chunks] + list(after),
                         FFN_PHASE_LAG)

    def previous_tile_norm():
        yp_ref[...] = _rmsnorm(x2_ref[...], gf_ref[...])
        yield

    def stash_prenorm(rows, x2):
        x2_ref[rows, :] = x2

    def write_sample(rows, x2):
        y = _rmsnorm(x2, gf_ref[...])
        for k in range(FFN_CHUNK // B):
            ys_ref[:, rows.start // B + k, :] = y[k * B:(k + 1) * B]

    def state_piece():
        h = i // pieces_per_head
        base = pl.multiple_of(h * HEAD_DIM, HEAD_DIM)
        rows = pl.ds(pl.multiple_of(base + (i % pieces_per_head) * e_piece, e_piece), e_piece)
        qd_h = qd_ref[pl.ds(base, HEAD_DIM), :]
        kd_h = kd_ref[pl.ds(base, HEAD_DIM), :]
        state_decay = sdec_ref[h]
        v_blk = [v_ref[rows, j * B:(j + 1) * B] for j in range(L)]
        cross = [jnp.zeros((e_piece, B), F32) for _ in range(L)]
        for d in range(HEAD_DIM):
            s_de = s_ref[0, d]
            new = s_de * state_decay
            for t in range(L):
                lanes = slice(t * B, (t + 1) * B)
                cross[t] = cross[t] + qd_h[d:d + 1, lanes] * s_de
                new = new + kd_h[d:d + 1, lanes] * v_blk[t]
            snew_ref[0, d] = new
        for t in range(L):
            o_ref[rows, t * B:(t + 1) * B] = o_ref[rows, t * B:(t + 1) * B] + cross[t]
        yield

    @pl.when(i == 0)
    def _seed():
        o_ref[...] = ointra_ref[...]
        x2_ref[...] = jnp.zeros_like(x2_ref)

    @pl.when(i < prompt_steps)
    def _prompt_rows():
        ffn_chunks(lambda rows: xp_ref[rows, :], stash_prenorm, FFN_TILE,
                   before=[previous_tile_norm()], after=[state_piece()])

    @pl.when(i == prompt_steps)
    def _sample_rows():
        gain_col = jnp.broadcast_to(gain_ref[...], (L * B, RET_DIM)).T
        for hd in range(RET_HEADS):
            r0 = hd * HEAD_DIM
            o_h = o_ref[r0:r0 + HEAD_DIM, :]
            mu = jnp.sum(o_h, axis=0, keepdims=True) * (1.0 / HEAD_DIM)
            dlt = o_h - mu
            var = jnp.sum(dlt * dlt, axis=0, keepdims=True) * (1.0 / HEAD_DIM)
            o_ref[r0:r0 + HEAD_DIM, :] = dlt * lax.rsqrt(var + GN_EPS) * gain_col[r0:r0 + HEAD_DIM, :]
        ret_out = (_silu(zg_ref[...]) * o_ref[...].T).astype(BF16)
        xs = jnp.concatenate([xs_ref[:, t, :] for t in range(L)], axis=0)
        x1 = (xs + _dot(cvo_ref[...], wout_ref[:CONV_DIM, :])
              + _dot(ret_out, wout_ref[CONV_DIM:, :]))
        ffn_chunks(lambda rows: x1[rows], write_sample, L * B, before=[previous_tile_norm()])


def _ffn(x_p, xs, qd_t, kd_t, v_t, o_intra, zg, cvo, s4, gain, w_out_b, g2, w_gate, w_up, w_down, gf,
         dec_seq):
    prompt_steps = x_p.shape[0] // FFN_TILE
    n_seq = xs.shape[0]
    assert prompt_steps % RET_HEADS == 0 and FFN_CHUNK % n_seq == 0
    e_piece = HEAD_DIM // (prompt_steps // RET_HEADS)
    assert e_piece % 8 == 0 and HEAD_DIM % e_piece == 0
    pieces_per_head = HEAD_DIM // e_piece
    sdec = jnp.asarray(np.exp(np.float32(dec_seq) * _LOG_GAMMA))
    full = lambda a: pl.BlockSpec(a.shape, lambda i: (0,) * a.ndim, pipeline_mode=pl.Buffered(1))
    prompt_tile = pl.BlockSpec((FFN_TILE, D_MODEL), lambda i: (jnp.minimum(i, prompt_steps - 1), 0))
    prompt_out = pl.BlockSpec((FFN_TILE, D_MODEL), lambda i: (jnp.maximum(i - 1, 0), 0))
    sample_out = pl.BlockSpec(xs.shape, lambda i: (0, 0, 0))

    def piece_index(i):
        piece = jnp.minimum(i, prompt_steps - 1)
        return (piece // pieces_per_head, 0, piece % pieces_per_head, 0)

    state_piece = pl.BlockSpec((1, HEAD_DIM, e_piece, n_seq), piece_index)
    return pl.pallas_call(
        functools.partial(_ffn_kernel, prompt_steps=prompt_steps, n_seq=n_seq, dec_seq=dec_seq,
                          e_piece=e_piece),
        grid=(prompt_steps + 1,),
        in_specs=[prompt_tile, full(xs), full(qd_t), full(kd_t), full(v_t), full(o_intra), full(zg),
                  full(cvo), state_piece, full(gain), pl.BlockSpec(memory_space=pltpu.SMEM), full(w_out_b),
                  full(g2),
                  full(w_gate), full(w_up), full(w_down), full(gf)],
        out_specs=[prompt_out, sample_out, state_piece],
        out_shape=[jax.ShapeDtypeStruct(x_p.shape, F32), jax.ShapeDtypeStruct(xs.shape, F32),
                   jax.ShapeDtypeStruct(s4.shape, F32)],
        scratch_shapes=[pltpu.VMEM(o_intra.shape, F32),
                        pltpu.VMEM((FFN_TILE, D_MODEL), F32)],
        compiler_params=pltpu.CompilerParams(
            dimension_semantics=("arbitrary",), vmem_limit_bytes=VMEM_LIMIT),
        name="ffn",
    )(x_p, xs, qd_t, kd_t, v_t, o_intra, zg, cvo, s4, gain, sdec, w_out_b, g2, w_gate, w_up, w_down, gf)


def _gamma_pow(head, power):
    return float(np.exp(np.float32(power) * _LOG_GAMMA[head]))


def _sample_front_kernel(x_ref, st_ref, g1_ref, win32_ref, wconv_ref, wout32_ref, inv_ref,
                         convst_ref, qd_ref, kd_ref, v_ref, o_ref, zg_ref, cvo_ref,
                         win_ref, wout_ref, wvgt_ref, xn_ref, z_ref, *, n_seq, dec_seq, n_sections):
    s = pl.program_id(0)
    L, B = dec_seq, n_seq
    k_scale = HEAD_DIM ** -0.5

    @pl.when(s == 0)
    def _normalize():
        x = jnp.concatenate([x_ref[:, t, :] for t in range(L)], axis=0)
        xn_ref[...] = _rmsnorm(x, g1_ref[...]).astype(BF16)

    wout_ref[...] = wout32_ref[...].astype(BF16)
    z_ref[s] = _dot(xn_ref[...], win32_ref[...].astype(BF16))

    for sec in range(n_sections):
        lo, hi = sec * SECTION, min((sec + 1) * SECTION, OFF_V)
        if lo < hi:
            @pl.when(s == sec)
            def _plain_columns(lo=lo, hi=hi):
                win_ref[:, lo:hi] = win32_ref[:, :hi - lo].astype(BF16)

    @pl.when(s == n_sections - 1)
    def _transposed_columns():
        start = SECTION - 2 * RET_DIM
        for c in range(start, SECTION, GROUP):
            wvgt_ref[c - start:c - start + GROUP, :] = win32_ref[:, c:c + GROUP].T.astype(BF16)

    @pl.when(s == n_sections - 1)
    def _dense_front():
        def proj(off):
            parts = []
            for sec in range(n_sections):
                lo, hi = max(off, sec * SECTION), min(off + 512, (sec + 1) * SECTION)
                if lo < hi:
                    parts.append(z_ref[sec, :, lo - sec * SECTION:hi - sec * SECTION])
            return parts[0] if len(parts) == 1 else jnp.concatenate(parts, axis=1)

        zb = proj(OFF_B)
        u = proj(OFF_C) * proj(OFF_H)
        ext = [st_ref[:, 0, :], st_ref[:, 1, :]] + [u[i * B:(i + 1) * B] for i in range(L)]
        wc = wconv_ref[...]
        for i in range(L):
            conv = wc[0] * ext[i] + wc[1] * ext[i + 1] + wc[2] * ext[i + 2]
            cvo_ref[i * B:(i + 1) * B, :] = (zb[i * B:(i + 1) * B] * conv).astype(BF16)
        convst_ref[:, 0, :] = ext[L]
        convst_ref[:, 1, :] = ext[L + 1]
        zg_ref[...] = proj(OFF_G)

        q_t = proj(OFF_Q).T
        k_t = proj(OFF_K).T
        v_t = proj(OFF_V).T
        v_ref[...] = v_t
        inv = jnp.broadcast_to(inv_ref[...], (B, LANES)).T[:HALF_DIM]
        cos_sin = [(jnp.cos(float(PAST_LEN + i) * inv), jnp.sin(float(PAST_LEN + i) * inv))
                   for i in range(L)]

        def rope(t, r0, i):
            cos, sin = cos_sin[i]
            t1 = t[r0:r0 + HALF_DIM, i * B:(i + 1) * B]
            t2 = t[r0 + HALF_DIM:r0 + HEAD_DIM, i * B:(i + 1) * B]
            return jnp.concatenate([t1 * cos - t2 * sin, t2 * cos + t1 * sin], axis=0)

        for hd in range(RET_HEADS):
            r0 = hd * HEAD_DIM
            qr = [rope(q_t, r0, i) for i in range(L)]
            kr = [rope(k_t, r0, i) for i in range(L)]
            for i in range(L):
                lanes = slice(i * B, (i + 1) * B)
                qd_ref[r0:r0 + HEAD_DIM, lanes] = qr[i] * _gamma_pow(hd, i + 1)
                kd_ref[r0:r0 + HEAD_DIM, lanes] = kr[i] * (_gamma_pow(hd, L - 1 - i) * k_scale)
                intra = jnp.zeros((HEAD_DIM, B), F32)
                for j in range(i + 1):
                    score = jnp.sum(qr[i] * kr[j], axis=0, keepdims=True) * (_gamma_pow(hd, i - j) * k_scale)
                    intra = intra + score * v_t[r0:r0 + HEAD_DIM, j * B:(j + 1) * B]
                o_ref[r0:r0 + HEAD_DIM, lanes] = intra


def _sample_front(xs, st, g1, w_in, w_conv, w_out, inv_lane, dec_seq):
    n_seq = xs.shape[0]
    rows = n_seq * dec_seq
    n_sections = w_in.shape[1] // SECTION
    out_slabs = w_out.shape[0] // CAST_ROWS
    assert n_sections * SECTION == w_in.shape[1] and w_in.shape[1] - OFF_V == 2 * RET_DIM <= SECTION
    assert out_slabs * CAST_ROWS == w_out.shape[0] and out_slabs <= n_sections
    full = lambda a: pl.BlockSpec(a.shape, lambda s: (0,) * a.ndim, pipeline_mode=pl.Buffered(1))
    const = lambda shape: pl.BlockSpec(shape, lambda s: (0, 0))
    out_slab = pl.BlockSpec((CAST_ROWS, w_out.shape[1]), lambda s: (jnp.minimum(s, out_slabs - 1), 0))
    section = pl.BlockSpec((w_in.shape[0], SECTION), lambda s: (0, s))
    feat = (RET_DIM, rows)
    return pl.pallas_call(
        functools.partial(_sample_front_kernel, n_seq=n_seq, dec_seq=dec_seq, n_sections=n_sections),
        grid=(n_sections,),
        in_specs=[full(xs), full(st), full(g1), section, full(w_conv), out_slab, full(inv_lane)],
        out_specs=[pl.BlockSpec(st.shape, lambda s: (0, 0, 0)),
                   const(feat), const(feat), const(feat), const(feat),
                   const((rows, RET_DIM)), const((rows, CONV_DIM)),
                   const((w_in.shape[0], OFF_V)), out_slab, const((2 * RET_DIM, w_in.shape[0]))],
        out_shape=[jax.ShapeDtypeStruct(st.shape, F32),
                   jax.ShapeDtypeStruct(feat, F32), jax.ShapeDtypeStruct(feat, F32),
                   jax.ShapeDtypeStruct(feat, F32), jax.ShapeDtypeStruct(feat, F32),
                   jax.ShapeDtypeStruct((rows, RET_DIM), F32), jax.ShapeDtypeStruct((rows, CONV_DIM), BF16),
                   jax.ShapeDtypeStruct((w_in.shape[0], OFF_V), BF16), jax.ShapeDtypeStruct(w_out.shape, BF16),
                   jax.ShapeDtypeStruct((2 * RET_DIM, w_in.shape[0]), BF16)],
        scratch_shapes=[
            pltpu.VMEM((rows, w_in.shape[0]), BF16),
            pltpu.VMEM((n_sections, rows, SECTION), F32),
        ],
        compiler_params=pltpu.CompilerParams(
            dimension_semantics=("arbitrary",), vmem_limit_bytes=VMEM_LIMIT),
        name="sample_front",
    )(xs, st, g1, w_in, w_conv, w_out, inv_lane)


def kernel(x_prompt, x_sample, state_conv, state_ret, meta_tokens, norm1_g, w_in, w_conv, ret_norm_g,
           w_out, norm2_g, w_gate, w_up, w_down, final_norm_g):
    n_p, seq, _ = x_prompt.shape
    n_s, dec_seq, _ = x_sample.shape
    rows_s = n_s * dec_seq
    assert norm1_g.shape[0] == 1 and seq % TILE == 0 and n_p % SEQS_PER_STEP == 0
    assert (n_p * seq) % FFN_TILE == 0
    assert n_s % LANES == 0 and rows_s % FFN_CHUNK == 0

    g1, g2, gf = norm1_g[0][None], norm2_g[0][None], final_norm_g[None]
    lg_lane = jnp.asarray(_LOG_GAMMA[np.arange(RET_DIM) // HEAD_DIM][None])
    inv = ROPE_BASE ** (-jnp.arange(HALF_DIM, dtype=F32) / HALF_DIM)
    inv_lane = jnp.tile(inv, LANES // HALF_DIM)[None]

    gain = ret_norm_g[0][None]

    w_conv_rows = jnp.transpose(w_conv, (1, 0, 2))
    conv_s, qd_t, kd_t, v_t, o_intra, zg, cvo, w_in_b, w_out_b, w_vg_t = _sample_front(
        x_sample, state_conv[0], g1, w_in[0], w_conv_rows, w_out[0], inv_lane, dec_seq)

    x1_p, conv_p, ret_p, wg_b, wu_b, wd_b = _prompt_mixer(
        x_prompt, meta_tokens, g1, w_in_b, w_vg_t, w_conv_rows, gain, w_out_b, lg_lane, inv_lane,
        w_gate[0], w_up[0], w_down[0])

    s4 = jnp.transpose(state_ret[0], (1, 2, 3, 0))
    y_p, y_sample, s4_new = _ffn(x1_p.reshape(n_p * seq, D_MODEL), x_sample, qd_t, kd_t, v_t, o_intra, zg,
                                 cvo, s4, gain, w_out_b, g2, wg_b, wu_b, wd_b, gf, dec_seq)
    y_prompt = y_p.reshape(n_p, seq, D_MODEL)
    ret_s = jnp.transpose(s4_new, (3, 0, 1, 2))

    return (y_prompt, y_sample, conv_p[None], ret_p[None], conv_s[None], ret_s[None])
```

```python
import functools

import numpy as np
import jax
import jax.numpy as jnp
from jax import lax
from jax.experimental import pallas as pl
from jax.experimental.pallas import tpu as pltpu

D_MODEL = 1024
N_META = 16
CONV_DIM = 512
RET_HEADS = 8
HEAD_DIM = 64
HALF_DIM = HEAD_DIM // 2
RET_DIM = RET_HEADS * HEAD_DIM
PAST_LEN = 16384
ROPE_BASE = 10000.0
EPS = 1e-6
GN_EPS = 1e-5

OFF_B, OFF_C, OFF_H, OFF_Q, OFF_K, OFF_V, OFF_G = (i * 512 for i in range(7))

LANES = 128
GROUP = 256
HEADS_PER_GROUP = GROUP // HEAD_DIM
N_GROUPS = RET_DIM // GROUP
TILE = 256
SEQS_PER_STEP = 4
FFN_TILE = 512
FFN_CHUNK = 256
SECTION = 1792
CAST_ROWS = 512
PHASE_LAG = 3
FFN_PHASE_LAG = 1
BF16_ROWS = 16
VMEM_LIMIT = 56 * 1024 * 1024

F32 = jnp.float32
BF16 = jnp.bfloat16

_LOG_GAMMA = np.log1p(-(2.0 ** (-5.0 - np.arange(RET_HEADS)))).astype(np.float32)


def _dot(a, b):
    return jnp.dot(a, b, preferred_element_type=F32)


def _dot_nt(a, b):
    return lax.dot_general(a, b, (((1,), (1,)), ((), ())), preferred_element_type=F32)


def _dot_tn(a, b):
    return lax.dot_general(a, b, (((0,), (0,)), ((), ())), preferred_element_type=F32)


def _rmsnorm(x, g):
    ms = jnp.mean(x * x, axis=-1, keepdims=True)
    return x * lax.rsqrt(ms + EPS) * g


def _silu(x):
    return x * jax.nn.sigmoid(x)


def _iota(shape, dim):
    return lax.broadcasted_iota(jnp.int32, shape, dim)


def _first_half_lanes():
    return (_iota((1, LANES), 1) & (HEAD_DIM - 1)) < HALF_DIM


def _rope(t, cos, signed_sin):
    first_half = _first_half_lanes()
    out = []
    for b in range(t.shape[1] // LANES):
        blk = t[:, b * LANES:(b + 1) * LANES]
        partner = jnp.where(first_half, pltpu.roll(blk, LANES - HALF_DIM, axis=1),
                            pltpu.roll(blk, HALF_DIM, axis=1))
        out.append(blk * cos + partner * signed_sin)
    return jnp.concatenate(out, axis=1)


def _ones_block():
    r = _iota((GROUP, GROUP), 0) >> 6
    c = _iota((GROUP, GROUP), 1) >> 6
    return jnp.where(r == c, 1.0, 0.0).astype(BF16)


def _trace_staggered(phase_iters, lag):
    live = list(range(len(phase_iters)))
    tick = 0
    while live:
        for k in list(live):
            if tick >= lag * k:
                try:
                    next(phase_iters[k])
                except StopIteration:
                    live.remove(k)
        tick += 1


def _prompt_mixer_kernel(x_ref, meta_ref, g1_ref, win_ref, wvgt_ref, wconv_ref, gain_ref, wout_ref,
                         lg_ref, inv_ref, wg32_ref, wu32_ref, wd32_ref,
                         x1_ref, convst_ref, retst_ref, wg_ref, wu_ref, wd_ref,
                         cos_ref, sin_ref, qdec_ref, kdec_ref, dmask_ref,
                         smask_ref, sdec_ref, gaincol_ref, tail_ref, state_ref, ubuf_ref, *, n_tiles):
    j = pl.program_id(0)
    p = pl.program_id(1)
    C = TILE
    k_scale = HEAD_DIM ** -0.5

    def proj(xn, off):
        return _dot(xn, win_ref[:, off:off + 512])

    @pl.when((j == 0) & (p == 0))
    def _init():
        i_f = _iota((C, 1), 0).astype(F32)
        cos_ref[...] = jnp.cos(i_f * inv_ref[...])
        sin_ref[...] = jnp.sin(i_f * inv_ref[...])
        lg = lg_ref[...]
        qdec_ref[...] = jnp.exp((i_f + 1.0) * lg)
        kdec_ref[...] = jnp.exp((C - 1.0 - i_f) * lg) * k_scale
        sdec_ref[...] = jnp.exp(C * lg)
        gaincol_ref[...] = jnp.broadcast_to(gain_ref[...], (C, RET_DIM)).T
        diff = (_iota((C, C), 1) - _iota((C, C), 0)).astype(F32)
        for h in range(RET_HEADS):
            dmask_ref[h] = jnp.where(
                diff >= 0, jnp.exp(jnp.maximum(diff, 0.0) * float(_LOG_GAMMA[h])) * k_scale, 0.0)
        smask = _ones_block().astype(F32)
        smask_ref[...] = smask

        xm = _rmsnorm(meta_ref[...], g1_ref[...]).astype(BF16)
        um = proj(xm, OFF_C) * proj(xm, OFF_H)
        vm = _dot_nt(xm, wvgt_ref[:RET_DIM, :]).astype(BF16)
        m_f = _iota((N_META, 1), 0).astype(F32)
        ang = m_f * inv_ref[...]
        sin = jnp.sin(ang)
        kmr = _rope(proj(xm, OFF_K), jnp.cos(ang), jnp.where(_first_half_lanes(), -sin, sin))
        kmd = (kmr * (jnp.exp((N_META - 1.0 - m_f) * lg) * k_scale)).astype(BF16)
        for b in range(tail_ref.shape[0]):
            tail_ref[b] = um[N_META - 2:N_META, :]
            for g in range(N_GROUPS):
                sl = slice(g * GROUP, (g + 1) * GROUP)
                state_ref[b, g] = _dot_tn(vm[:, sl], kmd[:, sl]) * smask

    off = (N_META + j * C).astype(F32) * inv_ref[...]
    cos_off, sin_off = jnp.cos(off), jnp.sin(off)
    cos_row, sin_row = cos_ref[...], sin_ref[...]
    cos = cos_row * cos_off - sin_row * sin_off
    sin = sin_row * cos_off + cos_row * sin_off
    sin = jnp.where(_first_half_lanes(), -sin, sin)
    head_of_lane = _iota((1, GROUP), 1) >> 6
    smask = smask_ref[...]
    wc = wconv_ref[...]

    def mix_sequence(slot):
        n = p * SEQS_PER_STEP + slot
        x = x_ref[slot]
        xn = _rmsnorm(x, g1_ref[...]).astype(BF16)

        zb = proj(xn, OFF_B)
        yield
        u = proj(xn, OFF_C) * proj(xn, OFF_H)
        ubuf_ref[slot, 6:8, :] = tail_ref[n]
        ubuf_ref[slot, 8:8 + C, :] = u
        conv = wc[0] * ubuf_ref[slot, 6:6 + C, :] + wc[1] * ubuf_ref[slot, 7:7 + C, :] + wc[2] * u
        conv_out = (zb * conv).astype(BF16)
        new_tail = u[C - 2:C, :]
        tail_ref[n] = new_tail

        yield

        qr = _rope(proj(xn, OFF_Q), cos, sin)
        yield
        kr = _rope(proj(xn, OFF_K), cos, sin)
        yield
        vg_t = _dot_nt(wvgt_ref[...], xn)
        v_t = vg_t[:RET_DIM].astype(BF16)
        yield
        qb = qr.astype(BF16)
        kb = kr.astype(BF16)
        qd = (qr * qdec_ref[...]).astype(BF16)
        kd = (kr * kdec_ref[...]).astype(BF16)
        o_parts = []
        new_states = []
        for g in range(N_GROUPS):
            sl = slice(g * GROUP, (g + 1) * GROUP)
            state_t = state_ref[n, g]
            cross_t = _dot_nt(state_t.astype(BF16), qd[:, sl])
            k_heads = jnp.concatenate(
                [jnp.where(head_of_lane == hh, kb[:, sl], jnp.zeros_like(kb[:, sl]))
                 for hh in range(HEADS_PER_GROUP)], axis=0)
            decay = dmask_ref[g * HEADS_PER_GROUP:(g + 1) * HEADS_PER_GROUP].reshape(HEADS_PER_GROUP * C, C)
            scores_t = (_dot_nt(k_heads, qb[:, sl]) * decay).astype(BF16)
            for hh in range(HEADS_PER_GROUP):
                h = g * HEADS_PER_GROUP + hh
                rows = slice(hh * HEAD_DIM, (hh + 1) * HEAD_DIM)
                o_parts.append(cross_t[rows] + _dot(v_t[h * HEAD_DIM:(h + 1) * HEAD_DIM],
                                                    scores_t[hh * C:(hh + 1) * C]))
            new_state = state_t * sdec_ref[:, sl] + _dot(v_t[sl], kd[:, sl]) * smask
            state_ref[n, g] = new_state
            new_states.append(new_state)
            yield

        normed = []
        for h, o_h in enumerate(o_parts):
            mu = jnp.sum(o_h, axis=0, keepdims=True) * (1.0 / HEAD_DIM)
            dlt = o_h - mu
            var = jnp.sum(dlt * dlt, axis=0, keepdims=True) * (1.0 / HEAD_DIM)
            normed.append(dlt * lax.rsqrt(var + GN_EPS) * gaincol_ref[h * HEAD_DIM:(h + 1) * HEAD_DIM, :])
        ret_out_t = (_silu(vg_t[RET_DIM:]) * jnp.concatenate(normed, axis=0)).astype(BF16)
        yield

        x1_ref[slot] = (x + _dot(conv_out, wout_ref[:CONV_DIM, :])
                        + _dot_tn(ret_out_t, wout_ref[CONV_DIM:, :]))
        finals[slot] = (new_tail, new_states)

    finals = [None] * SEQS_PER_STEP
    _trace_staggered([mix_sequence(slot) for slot in range(SEQS_PER_STEP)], PHASE_LAG)

    wg_ref[...] = wg32_ref[...].astype(BF16)
    wu_ref[...] = wu32_ref[...].astype(BF16)
    wd_ref[...] = wd32_ref[...].astype(BF16)

    @pl.when(j == n_tiles - 1)
    def _final_states():
        for slot, (new_tail, new_states) in enumerate(finals):
            n = p * SEQS_PER_STEP + slot
            convst_ref[n] = new_tail
            for g in range(N_GROUPS):
                state = new_states[g].T
                for hh in range(HEADS_PER_GROUP):
                    blk = slice(hh * HEAD_DIM, (hh + 1) * HEAD_DIM)
                    retst_ref[n, g * HEADS_PER_GROUP + hh] = state[blk, blk]


def _prompt_mixer(x_prompt, meta, g1, w_in, w_vg_t, w_conv, gain, w_out, lg_lane, inv_lane,
                  w_gate, w_up, w_down):
    n_seq, seq, _ = x_prompt.shape
    n_tiles = seq // TILE
    pairs = n_seq // SEQS_PER_STEP
    n_steps = n_tiles * pairs
    full = lambda a: pl.BlockSpec(a.shape, lambda j, p: (0,) * a.ndim, pipeline_mode=pl.Buffered(1))
    gu_rows = w_gate.shape[0] // n_steps
    d_rows = w_down.shape[0] // (n_steps // 2)
    assert gu_rows * n_steps == w_gate.shape[0] and d_rows * (n_steps // 2) == w_down.shape[0]
    assert gu_rows % BF16_ROWS == 0 and d_rows % BF16_ROWS == 0
    gu_slab = pl.BlockSpec((gu_rows, w_gate.shape[1]), lambda j, p: (j * pairs + p, 0))
    d_slab = pl.BlockSpec((d_rows, w_down.shape[1]), lambda j, p: ((j * pairs + p) // 2, 0))
    return pl.pallas_call(
        functools.partial(_prompt_mixer_kernel, n_tiles=n_tiles),
        grid=(n_tiles, pairs),
        in_specs=[pl.BlockSpec((SEQS_PER_STEP, TILE, D_MODEL), lambda j, p: (p, j, 0)),
                  full(meta), full(g1), full(w_in), full(w_vg_t), full(w_conv), full(gain),
                  full(w_out), full(lg_lane), full(inv_lane), gu_slab, gu_slab, d_slab],
        out_specs=[pl.BlockSpec((SEQS_PER_STEP, TILE, D_MODEL), lambda j, p: (p, j, 0)),
                   pl.BlockSpec((n_seq, 2, CONV_DIM), lambda j, p: (0, 0, 0)),
                   pl.BlockSpec((n_seq, RET_HEADS, HEAD_DIM, HEAD_DIM), lambda j, p: (0, 0, 0, 0)),
                   gu_slab, gu_slab, d_slab],
        out_shape=[jax.ShapeDtypeStruct((n_seq, seq, D_MODEL), F32),
                   jax.ShapeDtypeStruct((n_seq, 2, CONV_DIM), F32),
                   jax.ShapeDtypeStruct((n_seq, RET_HEADS, HEAD_DIM, HEAD_DIM), F32),
                   jax.ShapeDtypeStruct(w_gate.shape, BF16), jax.ShapeDtypeStruct(w_up.shape, BF16),
                   jax.ShapeDtypeStruct(w_down.shape, BF16)],
        scratch_shapes=[
            pltpu.VMEM((TILE, LANES), F32),
            pltpu.VMEM((TILE, LANES), F32),
            pltpu.VMEM((TILE, RET_DIM), F32),
            pltpu.VMEM((TILE, RET_DIM), F32),
            pltpu.VMEM((RET_HEADS, TILE, TILE), F32),
            pltpu.VMEM((GROUP, GROUP), F32),
            pltpu.VMEM((1, RET_DIM), F32),
            pltpu.VMEM((RET_DIM, TILE), F32),
            pltpu.VMEM((n_seq, 2, CONV_DIM), F32),
            pltpu.VMEM((n_seq, N_GROUPS, GROUP, GROUP), F32),
            pltpu.VMEM((SEQS_PER_STEP, TILE + 8, CONV_DIM), F32),
        ],
        compiler_params=pltpu.CompilerParams(
            dimension_semantics=("arbitrary", "arbitrary"), vmem_limit_bytes=VMEM_LIMIT),
        name="prompt_mixer",
    )(x_prompt, meta, g1, w_in, w_vg_t, w_conv, gain, w_out, lg_lane, inv_lane, w_gate, w_up, w_down)


def _ffn_kernel(xp_ref, xs_ref, qd_ref, kd_ref, v_ref, ointra_ref, zg_hbm, cvo_hbm, s_ref, gain_ref,
                sdec_ref, wout_hbm, g2_ref, wg_ref, wu_ref, wd_ref, gf_ref,
                yp_ref, ys_ref, snew_ref, o_ref, x2_ref, zg_ref, cvo_ref, wout_ref, late_sem,
                *, prompt_steps, n_seq, dec_seq, e_piece):
    i = pl.program_id(0)
    L, B = dec_seq, n_seq
    pieces_per_head = HEAD_DIM // e_piece

    def chunk_phases(load, finish, rows):
        x = load(rows)
        xn = _rmsnorm(x, g2_ref[...]).astype(BF16)
        yield
        gate = _dot(xn, wg_ref[...])
        yield
        up = _dot(xn, wu_ref[...])
        yield
        hidden = (_silu(gate) * up).astype(BF16)
        yield
        finish(rows, x + _dot(hidden, wd_ref[...]))

    def ffn_chunks(load, finish, n_rows, before=(), after=()):
        chunks = [slice(r, r + FFN_CHUNK) for r in range(0, n_rows, FFN_CHUNK)]
        _trace_staggered(list(before) + [chunk_phases(load, finish, rows) for rows in chunks] + list(after),
                         FFN_PHASE_LAG)

    def previous_tile_norm():
        yp_ref[...] = _rmsnorm(x2_ref[...], gf_ref[...])
        yield

    def stash_prenorm(rows, x2):
        x2_ref[rows, :] = x2

    def write_sample(rows, x2):
        y = _rmsnorm(x2, gf_ref[...])
        for k in range(FFN_CHUNK // B):
            ys_ref[:, rows.start // B + k, :] = y[k * B:(k + 1) * B]

    def state_piece():
        h = i // pieces_per_head
        base = pl.multiple_of(h * HEAD_DIM, HEAD_DIM)
        rows = pl.ds(pl.multiple_of(base + (i % pieces_per_head) * e_piece, e_piece), e_piece)
        qd_h = qd_ref[pl.ds(base, HEAD_DIM), :]
        kd_h = kd_ref[pl.ds(base, HEAD_DIM), :]
        state_decay = sdec_ref[h]
        v_blk = [v_ref[rows, j * B:(j + 1) * B] for j in range(L)]
        cross = [jnp.zeros((e_piece, B), F32) for _ in range(L)]
        for d in range(HEAD_DIM):
            s_de = s_ref[0, d]
            new = s_de * state_decay
            for t in range(L):
                lanes = slice(t * B, (t + 1) * B)
                cross[t] = cross[t] + qd_h[d:d + 1, lanes] * s_de
                new = new + kd_h[d:d + 1, lanes] * v_blk[t]
            snew_ref[0, d] = new
        for t in range(L):
            o_ref[rows, t * B:(t + 1) * B] = o_ref[rows, t * B:(t + 1) * B] + cross[t]
        yield

    def late_copies():
        pairs = ((zg_hbm, zg_ref), (cvo_hbm, cvo_ref), (wout_hbm, wout_ref))
        return [pltpu.make_async_copy(src, dst, late_sem.at[k]) for k, (src, dst) in enumerate(pairs)]

    @pl.when(i == 0)
    def _seed():
        for copy in late_copies():
            copy.start()
        o_ref[...] = ointra_ref[...]
        x2_ref[...] = jnp.zeros_like(x2_ref)

    @pl.when(i < prompt_steps)
    def _prompt_rows():
        ffn_chunks(lambda rows: xp_ref[rows, :], stash_prenorm, FFN_TILE,
                   before=[previous_tile_norm()], after=[state_piece()])

    @pl.when(i == prompt_steps)
    def _sample_rows():
        for copy in late_copies():
            copy.wait()
        gain_col = jnp.broadcast_to(gain_ref[...], (L * B, RET_DIM)).T
        for hd in range(RET_HEADS):
            r0 = hd * HEAD_DIM
            o_h = o_ref[r0:r0 + HEAD_DIM, :]
            mu = jnp.sum(o_h, axis=0, keepdims=True) * (1.0 / HEAD_DIM)
            dlt = o_h - mu
            var = jnp.sum(dlt * dlt, axis=0, keepdims=True) * (1.0 / HEAD_DIM)
            o_ref[r0:r0 + HEAD_DIM, :] = dlt * lax.rsqrt(var + GN_EPS) * gain_col[r0:r0 + HEAD_DIM, :]
        ret_out = (_silu(zg_ref[...]) * o_ref[...].T).astype(BF16)
        xs = jnp.concatenate([xs_ref[:, t, :] for t in range(L)], axis=0)
        x1 = (xs + _dot(cvo_ref[...], wout_ref[:CONV_DIM, :])
              + _dot(ret_out, wout_ref[CONV_DIM:, :]))
        ffn_chunks(lambda rows: x1[rows], write_sample, L * B, before=[previous_tile_norm()])


def _ffn(x_p, xs, qd_t, kd_t, v_t, o_intra, zg, cvo, s4, gain, w_out_b, g2, w_gate, w_up, w_down, gf,
         dec_seq):
    prompt_steps = x_p.shape[0] // FFN_TILE
    n_seq = xs.shape[0]
    assert prompt_steps % RET_HEADS == 0 and FFN_CHUNK % n_seq == 0
    e_piece = HEAD_DIM // (prompt_steps // RET_HEADS)
    assert e_piece % 8 == 0 and HEAD_DIM % e_piece == 0
    pieces_per_head = HEAD_DIM // e_piece
    sdec = jnp.asarray(np.exp(np.float32(dec_seq) * _LOG_GAMMA))
    full = lambda a: pl.BlockSpec(a.shape, lambda i: (0,) * a.ndim, pipeline_mode=pl.Buffered(1))
    prompt_tile = pl.BlockSpec((FFN_TILE, D_MODEL), lambda i: (jnp.minimum(i, prompt_steps - 1), 0))
    prompt_out = pl.BlockSpec((FFN_TILE, D_MODEL), lambda i: (jnp.maximum(i - 1, 0), 0))
    sample_out = pl.BlockSpec(xs.shape, lambda i: (0, 0, 0))
    in_hbm = pl.BlockSpec(memory_space=pl.ANY)

    def piece_index(i):
        piece = jnp.minimum(i, prompt_steps - 1)
        return (piece // pieces_per_head, 0, piece % pieces_per_head, 0)

    state_piece = pl.BlockSpec((1, HEAD_DIM, e_piece, n_seq), piece_index)
    return pl.pallas_call(
        functools.partial(_ffn_kernel, prompt_steps=prompt_steps, n_seq=n_seq, dec_seq=dec_seq,
                          e_piece=e_piece),
        grid=(prompt_steps + 1,),
        in_specs=[prompt_tile, full(xs), full(qd_t), full(kd_t), full(v_t), full(o_intra), in_hbm,
                  in_hbm, state_piece, full(gain), pl.BlockSpec(memory_space=pltpu.SMEM), in_hbm,
                  full(g2),
                  full(w_gate), full(w_up), full(w_down), full(gf)],
        out_specs=[prompt_out, sample_out, state_piece],
        out_shape=[jax.ShapeDtypeStruct(x_p.shape, F32), jax.ShapeDtypeStruct(xs.shape, F32),
                   jax.ShapeDtypeStruct(s4.shape, F32)],
        scratch_shapes=[pltpu.VMEM(o_intra.shape, F32),
                        pltpu.VMEM((FFN_TILE, D_MODEL), F32),
                        pltpu.VMEM(zg.shape, zg.dtype), pltpu.VMEM(cvo.shape, cvo.dtype),
                        pltpu.VMEM(w_out_b.shape, w_out_b.dtype),
                        pltpu.SemaphoreType.DMA((3,))],
        compiler_params=pltpu.CompilerParams(
            dimension_semantics=("arbitrary",), vmem_limit_bytes=VMEM_LIMIT),
        name="ffn",
    )(x_p, xs, qd_t, kd_t, v_t, o_intra, zg, cvo, s4, gain, sdec, w_out_b, g2, w_gate, w_up, w_down, gf)


def _gamma_pow(head, power):
    return float(np.exp(np.float32(power) * _LOG_GAMMA[head]))


def _sample_front_kernel(x_ref, st_ref, g1_ref, win32_ref, wconv_ref, wout32_ref, inv_ref,
                         convst_ref, qd_ref, kd_ref, v_ref, o_ref, zg_ref, cvo_ref,
                         win_ref, wout_ref, wvgt_ref, xn_ref, z_ref, *, n_seq, dec_seq, n_sections):
    s = pl.program_id(0)
    L, B = dec_seq, n_seq
    k_scale = HEAD_DIM ** -0.5

    @pl.when(s == 0)
    def _normalize():
        x = jnp.concatenate([x_ref[:, t, :] for t in range(L)], axis=0)
        xn_ref[...] = _rmsnorm(x, g1_ref[...]).astype(BF16)

    wout_ref[...] = wout32_ref[...].astype(BF16)
    z_ref[s] = _dot(xn_ref[...], win32_ref[...].astype(BF16))

    for sec in range(n_sections):
        lo, hi = sec * SECTION, min((sec + 1) * SECTION, OFF_V)
        if lo < hi:
            @pl.when(s == sec)
            def _plain_columns(lo=lo, hi=hi):
                win_ref[:, lo:hi] = win32_ref[:, :hi - lo].astype(BF16)

    @pl.when(s == n_sections - 1)
    def _transposed_columns():
        start = SECTION - 2 * RET_DIM
        for c in range(start, SECTION, GROUP):
            wvgt_ref[c - start:c - start + GROUP, :] = win32_ref[:, c:c + GROUP].T.astype(BF16)

    @pl.when(s == n_sections - 1)
    def _dense_front():
        def proj(off):
            parts = []
            for sec in range(n_sections):
                lo, hi = max(off, sec * SECTION), min(off + 512, (sec + 1) * SECTION)
                if lo < hi:
                    parts.append(z_ref[sec, :, lo - sec * SECTION:hi - sec * SECTION])
            return parts[0] if len(parts) == 1 else jnp.concatenate(parts, axis=1)

        zb = proj(OFF_B)
        u = proj(OFF_C) * proj(OFF_H)
        ext = [st_ref[:, 0, :], st_ref[:, 1, :]] + [u[i * B:(i + 1) * B] for i in range(L)]
        wc = wconv_ref[...]
        for i in range(L):
            conv = wc[0] * ext[i] + wc[1] * ext[i + 1] + wc[2] * ext[i + 2]
            cvo_ref[i * B:(i + 1) * B, :] = (zb[i * B:(i + 1) * B] * conv).astype(BF16)
        convst_ref[:, 0, :] = ext[L]
        convst_ref[:, 1, :] = ext[L + 1]
        zg_ref[...] = proj(OFF_G)

        q_t = proj(OFF_Q).T
        k_t = proj(OFF_K).T
        v_t = proj(OFF_V).T
        v_ref[...] = v_t
        inv = jnp.broadcast_to(inv_ref[...], (B, LANES)).T[:HALF_DIM]
        cos_sin = [(jnp.cos(float(PAST_LEN + i) * inv), jnp.sin(float(PAST_LEN + i) * inv))
                   for i in range(L)]

        def rope(t, r0, i):
            cos, sin = cos_sin[i]
            t1 = t[r0:r0 + HALF_DIM, i * B:(i + 1) * B]
            t2 = t[r0 + HALF_DIM:r0 + HEAD_DIM, i * B:(i + 1) * B]
            return jnp.concatenate([t1 * cos - t2 * sin, t2 * cos + t1 * sin], axis=0)

        for hd in range(RET_HEADS):
            r0 = hd * HEAD_DIM
            qr = [rope(q_t, r0, i) for i in range(L)]
            kr = [rope(k_t, r0, i) for i in range(L)]
            for i in range(L):
                lanes = slice(i * B, (i + 1) * B)
                qd_ref[r0:r0 + HEAD_DIM, lanes] = qr[i] * _gamma_pow(hd, i + 1)
                kd_ref[r0:r0 + HEAD_DIM, lanes] = kr[i] * (_gamma_pow(hd, L - 1 - i) * k_scale)
                intra = jnp.zeros((HEAD_DIM, B), F32)
                for j in range(i + 1):
                    score = jnp.sum(qr[i] * kr[j], axis=0, keepdims=True) * (_gamma_pow(hd, i - j) * k_scale)
                    intra = intra + score * v_t[r0:r0 + HEAD_DIM, j * B:(j + 1) * B]
                o_ref[r0:r0 + HEAD_DIM, lanes] = intra


def _sample_front(xs, st, g1, w_in, w_conv, w_out, inv_lane, dec_seq):
    n_seq = xs.shape[0]
    rows = n_seq * dec_seq
    n_sections = w_in.shape[1] // SECTION
    out_slabs = w_out.shape[0] // CAST_ROWS
    assert n_sections * SECTION == w_in.shape[1] and w_in.shape[1] - OFF_V == 2 * RET_DIM <= SECTION
    assert out_slabs * CAST_ROWS == w_out.shape[0] and out_slabs <= n_sections
    full = lambda a: pl.BlockSpec(a.shape, lambda s: (0,) * a.ndim, pipeline_mode=pl.Buffered(1))
    const = lambda shape: pl.BlockSpec(shape, lambda s: (0, 0))
    out_slab = pl.BlockSpec((CAST_ROWS, w_out.shape[1]), lambda s: (jnp.minimum(s, out_slabs - 1), 0))
    section = pl.BlockSpec((w_in.shape[0], SECTION), lambda s: (0, s))
    feat = (RET_DIM, rows)
    return pl.pallas_call(
        functools.partial(_sample_front_kernel, n_seq=n_seq, dec_seq=dec_seq, n_sections=n_sections),
        grid=(n_sections,),
        in_specs=[full(xs), full(st), full(g1), section, full(w_conv), out_slab, full(inv_lane)],
        out_specs=[pl.BlockSpec(st.shape, lambda s: (0, 0, 0)),
                   const(feat), const(feat), const(feat), const(feat),
                   const((rows, RET_DIM)), const((rows, CONV_DIM)),
                   const((w_in.shape[0], OFF_V)), out_slab, const((2 * RET_DIM, w_in.shape[0]))],
        out_shape=[jax.ShapeDtypeStruct(st.shape, F32),
                   jax.ShapeDtypeStruct(feat, F32), jax.ShapeDtypeStruct(feat, F32),
                   jax.ShapeDtypeStruct(feat, F32), jax.ShapeDtypeStruct(feat, F32),
                   jax.ShapeDtypeStruct((rows, RET_DIM), F32), jax.ShapeDtypeStruct((rows, CONV_DIM), BF16),
                   jax.ShapeDtypeStruct((w_in.shape[0], OFF_V), BF16), jax.ShapeDtypeStruct(w_out.shape, BF16),
                   jax.ShapeDtypeStruct((2 * RET_DIM, w_in.shape[0]), BF16)],
        scratch_shapes=[
            pltpu.VMEM((rows, w_in.shape[0]), BF16),
            pltpu.VMEM((n_sections, rows, SECTION), F32),
        ],
        compiler_params=pltpu.CompilerParams(
            dimension_semantics=("arbitrary",), vmem_limit_bytes=VMEM_LIMIT),
        name="sample_front",
    )(xs, st, g1, w_in, w_conv, w_out, inv_lane)


def kernel(x_prompt, x_sample, state_conv, state_ret, meta_tokens, norm1_g, w_in, w_conv, ret_norm_g,
           w_out, norm2_g, w_gate, w_up, w_down, final_norm_g):
    n_p, seq, _ = x_prompt.shape
    n_s, dec_seq, _ = x_sample.shape
    rows_s = n_s * dec_seq
    assert norm1_g.shape[0] == 1 and seq % TILE == 0 and n_p % SEQS_PER_STEP == 0
    assert (n_p * seq) % FFN_TILE == 0
    assert n_s % LANES == 0 and rows_s % FFN_CHUNK == 0

    g1, g2, gf = norm1_g[0][None], norm2_g[0][None], final_norm_g[None]
    lg_lane = jnp.asarray(_LOG_GAMMA[np.arange(RET_DIM) // HEAD_DIM][None])
    inv = ROPE_BASE ** (-jnp.arange(HALF_DIM, dtype=F32) / HALF_DIM)
    inv_lane = jnp.tile(inv, LANES // HALF_DIM)[None]

    gain = ret_norm_g[0][None]

    w_conv_rows = jnp.transpose(w_conv, (1, 0, 2))
    conv_s, qd_t, kd_t, v_t, o_intra, zg, cvo, w_in_b, w_out_b, w_vg_t = _sample_front(
        x_sample, state_conv[0], g1, w_in[0], w_conv_rows, w_out[0], inv_lane, dec_seq)

    x1_p, conv_p, ret_p, wg_b, wu_b, wd_b = _prompt_mixer(
        x_prompt, meta_tokens, g1, w_in_b, w_vg_t, w_conv_rows, gain, w_out_b, lg_lane, inv_lane,
        w_gate[0], w_up[0], w_down[0])

    s4 = jnp.transpose(state_ret[0], (1, 2, 3, 0))
    y_p, y_sample, s4_new = _ffn(x1_p.reshape(n_p * seq, D_MODEL), x_sample, qd_t, kd_t, v_t, o_intra, zg,
                                 cvo, s4, gain, w_out_b, g2, wg_b, wu_b, wd_b, gf, dec_seq)
    y_prompt = y_p.reshape(n_p, seq, D_MODEL)
    ret_s = jnp.transpose(s4_new, (3, 0, 1, 2))

    return (y_prompt, y_sample, conv_p[None], ret_p[None], conv_s[None], ret_s[None])
```

```python
import functools

import numpy as np
import jax
import jax.numpy as jnp
from jax import lax
from jax.experimental import pallas as pl
from jax.experimental.pallas import tpu as pltpu

D_MODEL = 1024
N_META = 16
CONV_DIM = 512
RET_HEADS = 8
HEAD_DIM = 64
HALF_DIM = HEAD_DIM // 2
RET_DIM = RET_HEADS * HEAD_DIM
PAST_LEN = 16384
ROPE_BASE = 10000.0
EPS = 1e-6
GN_EPS = 1e-5

OFF_B, OFF_C, OFF_H, OFF_Q, OFF_K, OFF_V, OFF_G = (i * 512 for i in range(7))

LANES = 128
GROUP = 256
HEADS_PER_GROUP = GROUP // HEAD_DIM
N_GROUPS = RET_DIM // GROUP
TILE = 256
SEQS_PER_STEP = 4
FFN_TILE = 512
FFN_CHUNK = 256
SECTION = 1792
CAST_ROWS = 512
PHASE_LAG = 3
FFN_PHASE_LAG = 1
BF16_ROWS = 16
VMEM_LIMIT = 56 * 1024 * 1024

F32 = jnp.float32
BF16 = jnp.bfloat16

_LOG_GAMMA = np.log1p(-(2.0 ** (-5.0 - np.arange(RET_HEADS)))).astype(np.float32)


def _dot(a, b):
    return jnp.dot(a, b, preferred_element_type=F32)


def _dot_nt(a, b):
    return lax.dot_general(a, b, (((1,), (1,)), ((), ())), preferred_element_type=F32)


def _dot_tn(a, b):
    return lax.dot_general(a, b, (((0,), (0,)), ((), ())), preferred_element_type=F32)


def _rmsnorm(x, g):
    ms = jnp.mean(x * x, axis=-1, keepdims=True)
    return x * lax.rsqrt(ms + EPS) * g


def _silu(x):
    return x * jax.nn.sigmoid(x)


def _iota(shape, dim):
    return lax.broadcasted_iota(jnp.int32, shape, dim)


def _first_half_lanes():
    return (_iota((1, LANES), 1) & (HEAD_DIM - 1)) < HALF_DIM


def _rope(t, cos, signed_sin):
    first_half = _first_half_lanes()
    out = []
    for b in range(t.shape[1] // LANES):
        blk = t[:, b * LANES:(b + 1) * LANES]
        partner = jnp.where(first_half, pltpu.roll(blk, LANES - HALF_DIM, axis=1),
                            pltpu.roll(blk, HALF_DIM, axis=1))
        out.append(blk * cos + partner * signed_sin)
    return jnp.concatenate(out, axis=1)


def _ones_block():
    r = _iota((GROUP, GROUP), 0) >> 6
    c = _iota((GROUP, GROUP), 1) >> 6
    return jnp.where(r == c, 1.0, 0.0).astype(BF16)


def _trace_staggered(phase_iters, lag):
    live = list(range(len(phase_iters)))
    tick = 0
    while live:
        for k in list(live):
            if tick >= lag * k:
                try:
                    next(phase_iters[k])
                except StopIteration:
                    live.remove(k)
        tick += 1


def _prompt_mixer_kernel(x_ref, meta_ref, g1_ref, win_ref, wvgt_ref, wconv_ref, gain_ref, wout_ref,
                         lg_ref, inv_ref, wg32_ref, wu32_ref, wd32_ref,
                         x1_ref, convst_ref, retst_ref, wg_ref, wu_ref, wd_ref,
                         cos_ref, sin_ref, qdec_ref, kdec_ref, dmask_ref,
                         smask_ref, sdec_ref, gaincol_ref, tail_ref, state_ref, ubuf_ref, *, n_tiles):
    j = pl.program_id(0)
    p = pl.program_id(1)
    C = TILE
    k_scale = HEAD_DIM ** -0.5

    def proj(xn, off):
        return _dot(xn, win_ref[:, off:off + 512])

    @pl.when((j == 0) & (p == 0))
    def _init():
        i_f = _iota((C, 1), 0).astype(F32)
        cos_ref[...] = jnp.cos(i_f * inv_ref[...])
        sin_ref[...] = jnp.sin(i_f * inv_ref[...])
        lg = lg_ref[...]
        qdec_ref[...] = jnp.exp((i_f + 1.0) * lg)
        kdec_ref[...] = jnp.exp((C - 1.0 - i_f) * lg) * k_scale
        sdec_ref[...] = jnp.exp(C * lg)
        gaincol_ref[...] = jnp.broadcast_to(gain_ref[...], (C, RET_DIM)).T
        diff = (_iota((C, C), 1) - _iota((C, C), 0)).astype(F32)
        for h in range(RET_HEADS):
            dmask_ref[h] = jnp.where(
                diff >= 0, jnp.exp(jnp.maximum(diff, 0.0) * float(_LOG_GAMMA[h])) * k_scale, 0.0)
        smask = _ones_block().astype(F32)
        smask_ref[...] = smask

        xm = _rmsnorm(meta_ref[...], g1_ref[...]).astype(BF16)
        um = proj(xm, OFF_C) * proj(xm, OFF_H)
        vm = _dot_nt(xm, wvgt_ref[:RET_DIM, :]).astype(BF16)
        m_f = _iota((N_META, 1), 0).astype(F32)
        ang = m_f * inv_ref[...]
        sin = jnp.sin(ang)
        kmr = _rope(proj(xm, OFF_K), jnp.cos(ang), jnp.where(_first_half_lanes(), -sin, sin))
        kmd = (kmr * (jnp.exp((N_META - 1.0 - m_f) * lg) * k_scale)).astype(BF16)
        for b in range(tail_ref.shape[0]):
            tail_ref[b] = um[N_META - 2:N_META, :]
            for g in range(N_GROUPS):
                sl = slice(g * GROUP, (g + 1) * GROUP)
                state_ref[b, g] = _dot_tn(vm[:, sl], kmd[:, sl]) * smask

    off = (N_META + j * C).astype(F32) * inv_ref[...]
    cos_off, sin_off = jnp.cos(off), jnp.sin(off)
    cos_row, sin_row = cos_ref[...], sin_ref[...]
    cos = cos_row * cos_off - sin_row * sin_off
    sin = sin_row * cos_off + cos_row * sin_off
    sin = jnp.where(_first_half_lanes(), -sin, sin)
    head_of_lane = _iota((1, GROUP), 1) >> 6
    smask = smask_ref[...]
    wc = wconv_ref[...]

    def mix_sequence(slot):
        n = p * SEQS_PER_STEP + slot
        x = x_ref[slot]
        xn = _rmsnorm(x, g1_ref[...]).astype(BF16)

        zb = proj(xn, OFF_B)
        yield
        u = proj(xn, OFF_C) * proj(xn, OFF_H)
        ubuf_ref[slot, 6:8, :] = tail_ref[n]
        ubuf_ref[slot, 8:8 + C, :] = u
        conv = wc[0] * ubuf_ref[slot, 6:6 + C, :] + wc[1] * ubuf_ref[slot, 7:7 + C, :] + wc[2] * u
        conv_out = (zb * conv).astype(BF16)
        new_tail = u[C - 2:C, :]
        tail_ref[n] = new_tail

        yield

        qr = _rope(proj(xn, OFF_Q), cos, sin)
        yield
        kr = _rope(proj(xn, OFF_K), cos, sin)
        yield
        vg_t = _dot_nt(wvgt_ref[...], xn)
        v_t = vg_t[:RET_DIM].astype(BF16)
        yield
        qb = qr.astype(BF16)
        kb = kr.astype(BF16)
        qd = (qr * qdec_ref[...]).astype(BF16)
        kd = (kr * kdec_ref[...]).astype(BF16)
        o_parts = []
        new_states = []
        for g in range(N_GROUPS):
            sl = slice(g * GROUP, (g + 1) * GROUP)
            state_t = state_ref[n, g]
            cross_t = _dot_nt(state_t.astype(BF16), qd[:, sl])
            k_heads = jnp.concatenate(
                [jnp.where(head_of_lane == hh, kb[:, sl], jnp.zeros_like(kb[:, sl]))
                 for hh in range(HEADS_PER_GROUP)], axis=0)
            decay = dmask_ref[g * HEADS_PER_GROUP:(g + 1) * HEADS_PER_GROUP].reshape(HEADS_PER_GROUP * C, C)
            scores_t = (_dot_nt(k_heads, qb[:, sl]) * decay).astype(BF16)
            for hh in range(HEADS_PER_GROUP):
                h = g * HEADS_PER_GROUP + hh
                rows = slice(hh * HEAD_DIM, (hh + 1) * HEAD_DIM)
                o_parts.append(cross_t[rows] + _dot(v_t[h * HEAD_DIM:(h + 1) * HEAD_DIM],
                                                    scores_t[hh * C:(hh + 1) * C]))
            new_state = state_t * sdec_ref[:, sl] + _dot(v_t[sl], kd[:, sl]) * smask
            state_ref[n, g] = new_state
            new_states.append(new_state)
            yield

        normed = []
        for h, o_h in enumerate(o_parts):
            mu = jnp.sum(o_h, axis=0, keepdims=True) * (1.0 / HEAD_DIM)
            dlt = o_h - mu
            var = jnp.sum(dlt * dlt, axis=0, keepdims=True) * (1.0 / HEAD_DIM)
            normed.append(dlt * lax.rsqrt(var + GN_EPS) * gaincol_ref[h * HEAD_DIM:(h + 1) * HEAD_DIM, :])
        ret_out_t = (_silu(vg_t[RET_DIM:]) * jnp.concatenate(normed, axis=0)).astype(BF16)
        yield

        x1_ref[slot] = (x + _dot(conv_out, wout_ref[:CONV_DIM, :])
                        + _dot_tn(ret_out_t, wout_ref[CONV_DIM:, :]))
        finals[slot] = (new_tail, new_states)

    finals = [None] * SEQS_PER_STEP
    _trace_staggered([mix_sequence(slot) for slot in range(SEQS_PER_STEP)], PHASE_LAG)

    wg_ref[...] = wg32_ref[...].astype(BF16)
    wu_ref[...] = wu32_ref[...].astype(BF16)
    wd_ref[...] = wd32_ref[...].astype(BF16)

    @pl.when(j == n_tiles - 1)
    def _final_states():
        for slot, (new_tail, new_states) in enumerate(finals):
            n = p * SEQS_PER_STEP + slot
            convst_ref[n] = new_tail
            for g in range(N_GROUPS):
                state = new_states[g].T
                for hh in range(HEADS_PER_GROUP):
                    blk = slice(hh * HEAD_DIM, (hh + 1) * HEAD_DIM)
                    retst_ref[n, g * HEADS_PER_GROUP + hh] = state[blk, blk]


def _prompt_mixer(x_prompt, meta, g1, w_in, w_vg_t, w_conv, gain, w_out, lg_lane, inv_lane,
                  w_gate, w_up, w_down):
    n_seq, seq, _ = x_prompt.shape
    n_tiles = seq // TILE
    pairs = n_seq // SEQS_PER_STEP
    n_steps = n_tiles * pairs
    full = lambda a: pl.BlockSpec(a.shape, lambda j, p: (0,) * a.ndim, pipeline_mode=pl.Buffered(1))
    gu_rows = w_gate.shape[0] // n_steps
    d_rows = w_down.shape[0] // (n_steps // 2)
    assert gu_rows * n_steps == w_gate.shape[0] and d_rows * (n_steps // 2) == w_down.shape[0]
    assert gu_rows % BF16_ROWS == 0 and d_rows % BF16_ROWS == 0
    gu_slab = pl.BlockSpec((gu_rows, w_gate.shape[1]), lambda j, p: (j * pairs + p, 0))
    d_slab = pl.BlockSpec((d_rows, w_down.shape[1]), lambda j, p: ((j * pairs + p) // 2, 0))
    return pl.pallas_call(
        functools.partial(_prompt_mixer_kernel, n_tiles=n_tiles),
        grid=(n_tiles, pairs),
        in_specs=[pl.BlockSpec((SEQS_PER_STEP, TILE, D_MODEL), lambda j, p: (p, j, 0)),
                  full(meta), full(g1), full(w_in), full(w_vg_t), full(w_conv), full(gain),
                  full(w_out), full(lg_lane), full(inv_lane), gu_slab, gu_slab, d_slab],
        out_specs=[pl.BlockSpec((SEQS_PER_STEP, TILE, D_MODEL), lambda j, p: (p, j, 0)),
                   pl.BlockSpec((n_seq, 2, CONV_DIM), lambda j, p: (0, 0, 0)),
                   pl.BlockSpec((n_seq, RET_HEADS, HEAD_DIM, HEAD_DIM), lambda j, p: (0, 0, 0, 0)),
                   gu_slab, gu_slab, d_slab],
        out_shape=[jax.ShapeDtypeStruct((n_seq, seq, D_MODEL), F32),
                   jax.ShapeDtypeStruct((n_seq, 2, CONV_DIM), F32),
                   jax.ShapeDtypeStruct((n_seq, RET_HEADS, HEAD_DIM, HEAD_DIM), F32),
                   jax.ShapeDtypeStruct(w_gate.shape, BF16), jax.ShapeDtypeStruct(w_up.shape, BF16),
                   jax.ShapeDtypeStruct(w_down.shape, BF16)],
        scratch_shapes=[
            pltpu.VMEM((TILE, LANES), F32),
            pltpu.VMEM((TILE, LANES), F32),
            pltpu.VMEM((TILE, RET_DIM), F32),
            pltpu.VMEM((TILE, RET_DIM), F32),
            pltpu.VMEM((RET_HEADS, TILE, TILE), F32),
            pltpu.VMEM((GROUP, GROUP), F32),
            pltpu.VMEM((1, RET_DIM), F32),
            pltpu.VMEM((RET_DIM, TILE), F32),
            pltpu.VMEM((n_seq, 2, CONV_DIM), F32),
            pltpu.VMEM((n_seq, N_GROUPS, GROUP, GROUP), F32),
            pltpu.VMEM((SEQS_PER_STEP, TILE + 8, CONV_DIM), F32),
        ],
        compiler_params=pltpu.CompilerParams(
            dimension_semantics=("arbitrary", "arbitrary"), vmem_limit_bytes=VMEM_LIMIT),
        name="prompt_mixer",
    )(x_prompt, meta, g1, w_in, w_vg_t, w_conv, gain, w_out, lg_lane, inv_lane, w_gate, w_up, w_down)


def _ffn_kernel(xp_ref, xs_hbm, qd_ref, kd_ref, v_ref, ointra_ref, zg_hbm, cvo_hbm, s_ref, gain_ref,
                sdec_ref, wout_hbm, g2_ref, wg_ref, wu_ref, wd_ref, gf_ref,
                yp_ref, ys_ref, snew_ref, o_ref, x2_ref, zg_ref, cvo_ref, wout_ref, xs_ref, late_sem,
                *, prompt_steps, n_seq, dec_seq, e_piece):
    i = pl.program_id(0)
    L, B = dec_seq, n_seq
    pieces_per_head = HEAD_DIM // e_piece

    def chunk_phases(load, finish, rows):
        x = load(rows)
        xn = _rmsnorm(x, g2_ref[...]).astype(BF16)
        yield
        gate = _dot(xn, wg_ref[...])
        yield
        up = _dot(xn, wu_ref[...])
        yield
        hidden = (_silu(gate) * up).astype(BF16)
        yield
        finish(rows, x + _dot(hidden, wd_ref[...]))

    def ffn_chunks(load, finish, n_rows, before=(), after=()):
        chunks = [slice(r, r + FFN_CHUNK) for r in range(0, n_rows, FFN_CHUNK)]
        _trace_staggered(list(before) + [chunk_phases(load, finish, rows) for rows in chunks] + list(after),
                         FFN_PHASE_LAG)

    def previous_tile_norm():
        yp_ref[...] = _rmsnorm(x2_ref[...], gf_ref[...])
        yield

    def stash_prenorm(rows, x2):
        x2_ref[rows, :] = x2

    def write_sample(rows, x2):
        y = _rmsnorm(x2, gf_ref[...])
        for k in range(FFN_CHUNK // B):
            ys_ref[:, rows.start // B + k, :] = y[k * B:(k + 1) * B]

    def state_piece():
        h = i // pieces_per_head
        base = pl.multiple_of(h * HEAD_DIM, HEAD_DIM)
        rows = pl.ds(pl.multiple_of(base + (i % pieces_per_head) * e_piece, e_piece), e_piece)
        qd_h = qd_ref[pl.ds(base, HEAD_DIM), :]
        kd_h = kd_ref[pl.ds(base, HEAD_DIM), :]
        state_decay = sdec_ref[h]
        v_blk = [v_ref[rows, j * B:(j + 1) * B] for j in range(L)]
        cross = [jnp.zeros((e_piece, B), F32) for _ in range(L)]
        for d in range(HEAD_DIM):
            s_de = s_ref[0, d]
            new = s_de * state_decay
            for t in range(L):
                lanes = slice(t * B, (t + 1) * B)
                cross[t] = cross[t] + qd_h[d:d + 1, lanes] * s_de
                new = new + kd_h[d:d + 1, lanes] * v_blk[t]
            snew_ref[0, d] = new
        for t in range(L):
            o_ref[rows, t * B:(t + 1) * B] = o_ref[rows, t * B:(t + 1) * B] + cross[t]
        yield

    def late_copies():
        pairs = ((zg_hbm, zg_ref), (cvo_hbm, cvo_ref), (wout_hbm, wout_ref), (xs_hbm, xs_ref))
        return [pltpu.make_async_copy(src, dst, late_sem.at[k]) for k, (src, dst) in enumerate(pairs)]

    @pl.when(i == 0)
    def _seed():
        for copy in late_copies():
            copy.start()
        o_ref[...] = ointra_ref[...]
        x2_ref[...] = jnp.zeros_like(x2_ref)

    @pl.when(i < prompt_steps)
    def _prompt_rows():
        ffn_chunks(lambda rows: xp_ref[rows, :], stash_prenorm, FFN_TILE,
                   before=[previous_tile_norm()], after=[state_piece()])

    @pl.when(i == prompt_steps)
    def _sample_rows():
        for copy in late_copies():
            copy.wait()
        gain_col = jnp.broadcast_to(gain_ref[...], (L * B, RET_DIM)).T
        for hd in range(RET_HEADS):
            r0 = hd * HEAD_DIM
            o_h = o_ref[r0:r0 + HEAD_DIM, :]
            mu = jnp.sum(o_h, axis=0, keepdims=True) * (1.0 / HEAD_DIM)
            dlt = o_h - mu
            var = jnp.sum(dlt * dlt, axis=0, keepdims=True) * (1.0 / HEAD_DIM)
            o_ref[r0:r0 + HEAD_DIM, :] = dlt * lax.rsqrt(var + GN_EPS) * gain_col[r0:r0 + HEAD_DIM, :]
        ret_out = (_silu(zg_ref[...]) * o_ref[...].T).astype(BF16)
        xs = jnp.concatenate([xs_ref[:, t, :] for t in range(L)], axis=0)
        x1 = (xs + _dot(cvo_ref[...], wout_ref[:CONV_DIM, :])
              + _dot(ret_out, wout_ref[CONV_DIM:, :]))
        ffn_chunks(lambda rows: x1[rows], write_sample, L * B, before=[previous_tile_norm()])


def _ffn(x_p, xs, qd_t, kd_t, v_t, o_intra, zg, cvo, s4, gain, w_out_b, g2, w_gate, w_up, w_down, gf,
         dec_seq):
    prompt_steps = x_p.shape[0] // FFN_TILE
    n_seq = xs.shape[0]
    assert prompt_steps % RET_HEADS == 0 and FFN_CHUNK % n_seq == 0
    e_piece = HEAD_DIM // (prompt_steps // RET_HEADS)
    assert e_piece % 8 == 0 and HEAD_DIM % e_piece == 0
    pieces_per_head = HEAD_DIM // e_piece
    sdec = jnp.asarray(np.exp(np.float32(dec_seq) * _LOG_GAMMA))
    full = lambda a: pl.BlockSpec(a.shape, lambda i: (0,) * a.ndim, pipeline_mode=pl.Buffered(1))
    prompt_tile = pl.BlockSpec((FFN_TILE, D_MODEL), lambda i: (jnp.minimum(i, prompt_steps - 1), 0))
    prompt_out = pl.BlockSpec((FFN_TILE, D_MODEL), lambda i: (jnp.maximum(i - 1, 0), 0))
    sample_out = pl.BlockSpec(xs.shape, lambda i: (0, 0, 0))
    in_hbm = pl.BlockSpec(memory_space=pl.ANY)

    def piece_index(i):
        piece = jnp.minimum(i, prompt_steps - 1)
        return (piece // pieces_per_head, 0, piece % pieces_per_head, 0)

    state_piece = pl.BlockSpec((1, HEAD_DIM, e_piece, n_seq), piece_index)
    return pl.pallas_call(
        functools.partial(_ffn_kernel, prompt_steps=prompt_steps, n_seq=n_seq, dec_seq=dec_seq,
                          e_piece=e_piece),
        grid=(prompt_steps + 1,),
        in_specs=[prompt_tile, in_hbm, full(qd_t), full(kd_t), full(v_t), full(o_intra), in_hbm,
                  in_hbm, state_piece, full(gain), pl.BlockSpec(memory_space=pltpu.SMEM), in_hbm,
                  full(g2),
                  full(w_gate), full(w_up), full(w_down), full(gf)],
        out_specs=[prompt_out, sample_out, state_piece],
        out_shape=[jax.ShapeDtypeStruct(x_p.shape, F32), jax.ShapeDtypeStruct(xs.shape, F32),
                   jax.ShapeDtypeStruct(s4.shape, F32)],
        scratch_shapes=[pltpu.VMEM(o_intra.shape, F32),
                        pltpu.VMEM((FFN_TILE, D_MODEL), F32),
                        pltpu.VMEM(zg.shape, zg.dtype), pltpu.VMEM(cvo.shape, cvo.dtype),
                        pltpu.VMEM(w_out_b.shape, w_out_b.dtype),
                        pltpu.VMEM(xs.shape, xs.dtype),
                        pltpu.SemaphoreType.DMA((4,))],
        compiler_params=pltpu.CompilerParams(
            dimension_semantics=("arbitrary",), vmem_limit_bytes=VMEM_LIMIT),
        name="ffn",
    )(x_p, xs, qd_t, kd_t, v_t, o_intra, zg, cvo, s4, gain, sdec, w_out_b, g2, w_gate, w_up, w_down, gf)


def _gamma_pow(head, power):
    return float(np.exp(np.float32(power) * _LOG_GAMMA[head]))


def _sample_front_kernel(x_ref, st_ref, g1_ref, win32_ref, wconv_ref, wout32_ref, inv_ref,
                         convst_ref, qd_ref, kd_ref, v_ref, o_ref, zg_ref, cvo_ref,
                         win_ref, wout_ref, wvgt_ref, xn_ref, z_ref, *, n_seq, dec_seq, n_sections):
    s = pl.program_id(0)
    L, B = dec_seq, n_seq
    k_scale = HEAD_DIM ** -0.5

    @pl.when(s == 0)
    def _normalize():
        x = jnp.concatenate([x_ref[:, t, :] for t in range(L)], axis=0)
        xn_ref[...] = _rmsnorm(x, g1_ref[...]).astype(BF16)

    wout_ref[...] = wout32_ref[...].astype(BF16)
    z_ref[s] = _dot(xn_ref[...], win32_ref[...].astype(BF16))

    for sec in range(n_sections):
        lo, hi = sec * SECTION, min((sec + 1) * SECTION, OFF_V)
        if lo < hi:
            @pl.when(s == sec)
            def _plain_columns(lo=lo, hi=hi):
                win_ref[:, lo:hi] = win32_ref[:, :hi - lo].astype(BF16)

    @pl.when(s == n_sections - 1)
    def _transposed_columns():
        start = SECTION - 2 * RET_DIM
        for c in range(start, SECTION, GROUP):
            wvgt_ref[c - start:c - start + GROUP, :] = win32_ref[:, c:c + GROUP].T.astype(BF16)

    @pl.when(s == n_sections - 1)
    def _dense_front():
        def proj(off):
            parts = []
            for sec in range(n_sections):
                lo, hi = max(off, sec * SECTION), min(off + 512, (sec + 1) * SECTION)
                if lo < hi:
                    parts.append(z_ref[sec, :, lo - sec * SECTION:hi - sec * SECTION])
            return parts[0] if len(parts) == 1 else jnp.concatenate(parts, axis=1)

        zb = proj(OFF_B)
        u = proj(OFF_C) * proj(OFF_H)
        ext = [st_ref[:, 0, :], st_ref[:, 1, :]] + [u[i * B:(i + 1) * B] for i in range(L)]
        wc = wconv_ref[...]
        for i in range(L):
            conv = wc[0] * ext[i] + wc[1] * ext[i + 1] + wc[2] * ext[i + 2]
            cvo_ref[i * B:(i + 1) * B, :] = (zb[i * B:(i + 1) * B] * conv).astype(BF16)
        convst_ref[:, 0, :] = ext[L]
        convst_ref[:, 1, :] = ext[L + 1]
        zg_ref[...] = proj(OFF_G)

        q_t = proj(OFF_Q).T
        k_t = proj(OFF_K).T
        v_t = proj(OFF_V).T
        v_ref[...] = v_t
        inv = jnp.broadcast_to(inv_ref[...], (B, LANES)).T[:HALF_DIM]
        cos_sin = [(jnp.cos(float(PAST_LEN + i) * inv), jnp.sin(float(PAST_LEN + i) * inv))
                   for i in range(L)]

        def rope(t, r0, i):
            cos, sin = cos_sin[i]
            t1 = t[r0:r0 + HALF_DIM, i * B:(i + 1) * B]
            t2 = t[r0 + HALF_DIM:r0 + HEAD_DIM, i * B:(i + 1) * B]
            return jnp.concatenate([t1 * cos - t2 * sin, t2 * cos + t1 * sin], axis=0)

        for hd in range(RET_HEADS):
            r0 = hd * HEAD_DIM
            qr = [rope(q_t, r0, i) for i in range(L)]
            kr = [rope(k_t, r0, i) for i in range(L)]
            for i in range(L):
                lanes = slice(i * B, (i + 1) * B)
                qd_ref[r0:r0 + HEAD_DIM, lanes] = qr[i] * _gamma_pow(hd, i + 1)
                kd_ref[r0:r0 + HEAD_DIM, lanes] = kr[i] * (_gamma_pow(hd, L - 1 - i) * k_scale)
                intra = jnp.zeros((HEAD_DIM, B), F32)
                for j in range(i + 1):
                    score = jnp.sum(qr[i] * kr[j], axis=0, keepdims=True) * (_gamma_pow(hd, i - j) * k_scale)
                    intra = intra + score * v_t[r0:r0 + HEAD_DIM, j * B:(j + 1) * B]
                o_ref[r0:r0 + HEAD_DIM, lanes] = intra


def _sample_front(xs, st, g1, w_in, w_conv, w_out, inv_lane, dec_seq):
    n_seq = xs.shape[0]
    rows = n_seq * dec_seq
    n_sections = w_in.shape[1] // SECTION
    out_slabs = w_out.shape[0] // CAST_ROWS
    assert n_sections * SECTION == w_in.shape[1] and w_in.shape[1] - OFF_V == 2 * RET_DIM <= SECTION
    assert out_slabs * CAST_ROWS == w_out.shape[0] and out_slabs <= n_sections
    full = lambda a: pl.BlockSpec(a.shape, lambda s: (0,) * a.ndim, pipeline_mode=pl.Buffered(1))
    const = lambda shape: pl.BlockSpec(shape, lambda s: (0, 0))
    out_slab = pl.BlockSpec((CAST_ROWS, w_out.shape[1]), lambda s: (jnp.minimum(s, out_slabs - 1), 0))
    section = pl.BlockSpec((w_in.shape[0], SECTION), lambda s: (0, s))
    feat = (RET_DIM, rows)
    return pl.pallas_call(
        functools.partial(_sample_front_kernel, n_seq=n_seq, dec_seq=dec_seq, n_sections=n_sections),
        grid=(n_sections,),
        in_specs=[full(xs), full(st), full(g1), section, full(w_conv), out_slab, full(inv_lane)],
        out_specs=[pl.BlockSpec(st.shape, lambda s: (0, 0, 0)),
                   const(feat), const(feat), const(feat), const(feat),
                   const((rows, RET_DIM)), const((rows, CONV_DIM)),
                   const((w_in.shape[0], OFF_V)), out_slab, const((2 * RET_DIM, w_in.shape[0]))],
        out_shape=[jax.ShapeDtypeStruct(st.shape, F32),
                   jax.ShapeDtypeStruct(feat, F32), jax.ShapeDtypeStruct(feat, F32),
                   jax.ShapeDtypeStruct(feat, F32), jax.ShapeDtypeStruct(feat, F32),
                   jax.ShapeDtypeStruct((rows, RET_DIM), F32), jax.ShapeDtypeStruct((rows, CONV_DIM), BF16),
                   jax.ShapeDtypeStruct((w_in.shape[0], OFF_V), BF16), jax.ShapeDtypeStruct(w_out.shape, BF16),
                   jax.ShapeDtypeStruct((2 * RET_DIM, w_in.shape[0]), BF16)],
        scratch_shapes=[
            pltpu.VMEM((rows, w_in.shape[0]), BF16),
            pltpu.VMEM((n_sections, rows, SECTION), F32),
        ],
        compiler_params=pltpu.CompilerParams(
            dimension_semantics=("arbitrary",), vmem_limit_bytes=VMEM_LIMIT),
        name="sample_front",
    )(xs, st, g1, w_in, w_conv, w_out, inv_lane)


def kernel(x_prompt, x_sample, state_conv, state_ret, meta_tokens, norm1_g, w_in, w_conv, ret_norm_g,
           w_out, norm2_g, w_gate, w_up, w_down, final_norm_g):
    n_p, seq, _ = x_prompt.shape
    n_s, dec_seq, _ = x_sample.shape
    rows_s = n_s * dec_seq
    assert norm1_g.shape[0] == 1 and seq % TILE == 0 and n_p % SEQS_PER_STEP == 0
    assert (n_p * seq) % FFN_TILE == 0
    assert n_s % LANES == 0 and rows_s % FFN_CHUNK == 0

    g1, g2, gf = norm1_g[0][None], norm2_g[0][None], final_norm_g[None]
    lg_lane = jnp.asarray(_LOG_GAMMA[np.arange(RET_DIM) // HEAD_DIM][None])
    inv = ROPE_BASE ** (-jnp.arange(HALF_DIM, dtype=F32) / HALF_DIM)
    inv_lane = jnp.tile(inv, LANES // HALF_DIM)[None]

    gain = ret_norm_g[0][None]

    w_conv_rows = jnp.transpose(w_conv, (1, 0, 2))
    conv_s, qd_t, kd_t, v_t, o_intra, zg, cvo, w_in_b, w_out_b, w_vg_t = _sample_front(
        x_sample, state_conv[0], g1, w_in[0], w_conv_rows, w_out[0], inv_lane, dec_seq)

    x1_p, conv_p, ret_p, wg_b, wu_b, wd_b = _prompt_mixer(
        x_prompt, meta_tokens, g1, w_in_b, w_vg_t, w_conv_rows, gain, w_out_b, lg_lane, inv_lane,
        w_gate[0], w_up[0], w_down[0])

    s4 = jnp.transpose(state_ret[0], (1, 2, 3, 0))
    y_p, y_sample, s4_new = _ffn(x1_p.reshape(n_p * seq, D_MODEL), x_sample, qd_t, kd_t, v_t, o_intra, zg,
                                 cvo, s4, gain, w_out_b, g2, wg_b, wu_b, wd_b, gf, dec_seq)
    y_prompt = y_p.reshape(n_p, seq, D_MODEL)
    ret_s = jnp.transpose(s4_new, (3, 0, 1, 2))

    return (y_prompt, y_sample, conv_p[None], ret_p[None], conv_s[None], ret_s[None])
```
